```python
import math
import jax, jax.numpy as jnp
from jax import lax
import numpy as np

D_MODEL = 2048
BATCH = 2
SEQ = 8192
DEPTH = 1

HEAD_DIM = 64
A_Q_HEADS = 16
A_KV_HEADS = 4
A_GROUP = A_Q_HEADS // A_KV_HEADS
A_WINDOW = 128
A_WIDTH = A_Q_HEADS * HEAD_DIM
A_KV_WIDTH = A_KV_HEADS * HEAD_DIM
B_HEADS = 12
B_WIDTH = B_HEADS * HEAD_DIM
B_PATTERNS = ((128, 1), (512, 4), (2048, 16))
BLOCK = 128
IN_SPLITS = (A_WIDTH, A_WIDTH + A_KV_WIDTH, A_WIDTH + 2 * A_KV_WIDTH,
             A_WIDTH + 2 * A_KV_WIDTH + B_WIDTH, A_WIDTH + 2 * A_KV_WIDTH + 2 * B_WIDTH)
IN_COLS = A_WIDTH + 2 * A_KV_WIDTH + 3 * B_WIDTH
N_REL_HEADS = A_Q_HEADS + B_HEADS
REL_BUCKETS = 32
REL_MAX_DIST = 2048
X_HEADS = 4
X_HEAD_DIM = 128
X_WIDTH = X_HEADS * X_HEAD_DIM
MEM_LEN = 256
N_GROUPS = 4
EXP_PER_GROUP = 8
N_EXPERTS = N_GROUPS * EXP_PER_GROUP
TOP_K = 2
D_FF_EXPERT = 512
MOE_BLOCK = 128
EPS = 1e-6
NEG = -1e30

kernel_name = "hybrid_gated_swa_dilated_hmoe"


def rmsnorm(x, g):
    xf = x.astype(jnp.float32)
    y = xf * lax.rsqrt(jnp.mean(xf * xf, axis=-1, keepdims=True) + EPS)
    return (y * g.astype(jnp.float32)).astype(x.dtype)


def t5_bucket(dist):
    max_exact = REL_BUCKETS // 2
    d = jnp.maximum(dist, 0)
    df = jnp.maximum(d, 1).astype(jnp.float32)
    large = max_exact + (jnp.log(df / max_exact) / math.log(REL_MAX_DIST / max_exact)
                         * (REL_BUCKETS - max_exact)).astype(jnp.int32)
    large = jnp.minimum(large, REL_BUCKETS - 1)
    return jnp.where(d < max_exact, d, large)


def banded_attention(q, k, v, bias_table, max_off, step, sinks=None):
    bt, n, hk, g, dh = q.shape
    nb = n // BLOCK
    qb = q.reshape(bt, nb, BLOCK, hk, g, dh)
    pad = ((0, 0), (BLOCK, 0), (0, 0), (0, 0))
    kb = jnp.pad(k, pad).reshape(bt, nb + 1, BLOCK, hk, dh)
    vb = jnp.pad(v, pad).reshape(bt, nb + 1, BLOCK, hk, dh)
    kk = jnp.concatenate([kb[:, :-1], kb[:, 1:]], axis=2)
    vv = jnp.concatenate([vb[:, :-1], vb[:, 1:]], axis=2)
    qi = jnp.arange(BLOCK)[:, None]
    ki = jnp.arange(2 * BLOCK)[None, :]
    dist = qi + BLOCK - ki
    bias = bias_table[t5_bucket(dist * step)]
    bias = jnp.transpose(bias, (2, 3, 0, 1)).astype(jnp.float32)
    key_abs = jnp.arange(nb)[:, None] * BLOCK + ki - BLOCK
    mask = ((dist >= 0) & (dist <= max_off))[None] & (key_abs >= 0)[:, None, :]
    s = jnp.einsum('bnqhgd,bnkhd->bnhgqk', qb, kk).astype(jnp.float32) + bias
    s = jnp.where(mask[None, :, None, None], s, NEG)
    m = jnp.max(s, axis=-1, keepdims=True)
    if sinks is not None:
        sk = sinks.astype(jnp.float32)[None, None, :, :, None, None]
        m = jnp.maximum(m, sk)
    p = jnp.exp(s - m)
    l = jnp.sum(p, axis=-1, keepdims=True)
    if sinks is not None:
        l = l + jnp.exp(sk - m)
    o = jnp.einsum('bnhgqk,bnkhd->bnqhgd', (p / l).astype(v.dtype), vv)
    lse = jnp.transpose((m + jnp.log(l))[..., 0], (0, 1, 4, 2, 3))
    return o.reshape(bt, n, hk, g, dh), lse.reshape(bt, n, hk, g)


def dilated_attention(q, k, v, bias_table):
    b, s, h, dh = q.shape
    outs, lses = [], []
    for w, r in B_PATTERNS:
        span = r * BLOCK
        s_pad = -(-s // span) * span
        n = s_pad // r

        def to_sub(t):
            t = jnp.pad(t, ((0, 0), (0, s_pad - s), (0, 0), (0, 0))).reshape(b, n, r, h, dh)
            return jnp.transpose(t, (0, 2, 1, 3, 4)).reshape(b * r, n, h, dh)

        qs, ks, vs = to_sub(q), to_sub(k), to_sub(v)
        o, lse = banded_attention(qs[:, :, :, None], ks, vs, bias_table[:, :, None], w // r, r)
        o = jnp.transpose(o.reshape(b, r, n, h, dh), (0, 2, 1, 3, 4)).reshape(b, s_pad, h, dh)[:, :s]
        lse = jnp.transpose(lse.reshape(b, r, n, h), (0, 2, 1, 3)).reshape(b, s_pad, h)[:, :s]
        outs.append(o)
        lses.append(lse)
    wts = jax.nn.softmax(jnp.stack(lses, axis=0), axis=0)
    o = jnp.sum(wts[..., None] * jnp.stack(outs, axis=0).astype(jnp.float32), axis=0)
    return o.astype(q.dtype)


def cross_attention(h, mem_n, w_xq, w_xk, w_xv, w_xo):
    b, s, _ = h.shape
    ml = mem_n.shape[1]
    q = (h @ w_xq).reshape(b, s, X_HEADS, X_HEAD_DIM) * (X_HEAD_DIM ** -0.5)
    k = (mem_n @ w_xk).reshape(b, ml, X_HEADS, X_HEAD_DIM)
    v = (mem_n @ w_xv).reshape(b, ml, X_HEADS, X_HEAD_DIM)
    p = jax.nn.softmax(jnp.einsum('bshd,bmhd->bhsm', q, k).astype(jnp.float32), axis=-1)
    o = jnp.einsum('bhsm,bmhd->bshd', p.astype(v.dtype), v).reshape(b, s, X_WIDTH)
    return o @ w_xo


def hier_moe(h, w_rg, w_re, w1, w3, w2):
    b, s, d = h.shape
    t = b * s
    hf = h.reshape(t, d)
    g_logits = (hf @ w_rg).astype(jnp.float32)
    g_prob = jax.nn.softmax(g_logits, axis=-1)
    g_gate, g_idx = lax.top_k(g_prob, 1)
    e_logits = (hf @ w_re).astype(jnp.float32).reshape(t, N_GROUPS, EXP_PER_GROUP)
    e_logits = jnp.take_along_axis(e_logits, g_idx[:, :, None], axis=1)[:, 0]
    top_v, top_i = lax.top_k(e_logits, TOP_K)
    gate = jax.nn.softmax(top_v, axis=-1) * g_gate
    expert = g_idx * EXP_PER_GROUP + top_i

    n_assign = t * TOP_K
    e_flat = expert.reshape(-1)
    tok_flat = jnp.broadcast_to(jnp.arange(t, dtype=jnp.int32)[:, None], (t, TOP_K)).reshape(-1)
    w_flat = gate.reshape(-1)
    order = jnp.argsort(e_flat)
    e_sorted = e_flat[order]
    counts = jnp.bincount(e_flat, length=N_EXPERTS)
    padded = (counts + MOE_BLOCK - 1) // MOE_BLOCK * MOE_BLOCK
    start = jnp.cumsum(counts) - counts
    pend = jnp.cumsum(padded)
    pstart = pend - padded
    dest = pstart[e_sorted] + (jnp.arange(n_assign) - start[e_sorted])
    n_rows = -(-(n_assign + N_EXPERTS * (MOE_BLOCK - 1)) // MOE_BLOCK) * MOE_BLOCK
    n_blocks = n_rows // MOE_BLOCK
    row_tok = jnp.full((n_rows,), t, jnp.int32).at[dest].set(tok_flat[order])
    row_w = jnp.zeros((n_rows,), jnp.float32).at[dest].set(w_flat[order])
    block_e = jnp.minimum(jnp.searchsorted(pend, jnp.arange(n_blocks) * MOE_BLOCK, side='right'),
                          N_EXPERTS - 1)
    h_pad = jnp.concatenate([hf, jnp.zeros((1, d), hf.dtype)], axis=0)
    xb = h_pad[row_tok].reshape(n_blocks, MOE_BLOCK, d)

    def expert_block(args):
        xk, e = args
        return (jax.nn.silu(xk @ w1[e]) * (xk @ w3[e])) @ w2[e]

    yb = lax.map(expert_block, (xb, block_e)).reshape(n_rows, d)
    y = jnp.zeros((t + 1, d), jnp.float32).at[row_tok].add(yb.astype(jnp.float32) * row_w[:, None])
    return y[:t].reshape(b, s, d).astype(h.dtype)


def setup_inputs(seed: int = 0) -> dict:
    key = jax.random.key(seed)
    ks = jax.random.split(key, 24)
    f32 = jnp.float32
    L = DEPTH

    def nrm(k, shape, scale):
        return jax.random.normal(k, shape, f32) * scale

    return {
        "x": nrm(ks[0], (BATCH, SEQ, D_MODEL), 1.0),
        "mem": nrm(ks[1], (BATCH, MEM_LEN, D_MODEL), 1.0),
        "rel_bias": nrm(ks[2], (REL_BUCKETS, N_REL_HEADS), 0.5),
        "g_mix": 1.0 + nrm(ks[3], (L, D_MODEL), 0.02),
        "w_in": nrm(ks[4], (L, D_MODEL, IN_COLS), D_MODEL ** -0.5),
        "sinks_a": nrm(ks[5], (L, A_Q_HEADS), 0.5),
        "w_a_out": nrm(ks[6], (L, A_WIDTH, D_MODEL), A_WIDTH ** -0.5),
        "w_b_out": nrm(ks[7], (L, B_WIDTH, D_MODEL), B_WIDTH ** -0.5),
        "w_gate": nrm(ks[8], (L, D_MODEL, 2 * D_MODEL), D_MODEL ** -0.5),
        "b_gate": nrm(ks[9], (L, 2 * D_MODEL), 0.1),
        "w_o": nrm(ks[10], (L, D_MODEL, D_MODEL), D_MODEL ** -0.5),
        "g_x": 1.0 + nrm(ks[11], (L, D_MODEL), 0.02),
        "g_mem": 1.0 + nrm(ks[12], (L, D_MODEL), 0.02),
        "w_xq": nrm(ks[13], (L, D_MODEL, X_WIDTH), D_MODEL ** -0.5),
        "w_xk": nrm(ks[14], (L, D_MODEL, X_WIDTH), D_MODEL ** -0.5),
        "w_xv": nrm(ks[15], (L, D_MODEL, X_WIDTH), D_MODEL ** -0.5),
        "w_xo": nrm(ks[16], (L, X_WIDTH, D_MODEL), X_WIDTH ** -0.5),
        "g_moe": 1.0 + nrm(ks[17], (L, D_MODEL), 0.02),
        "w_rg": nrm(ks[18], (L, D_MODEL, N_GROUPS), D_MODEL ** -0.5),
        "w_re": nrm(ks[19], (L, D_MODEL, N_EXPERTS), D_MODEL ** -0.5),
        "w1": nrm(ks[20], (L, N_EXPERTS, D_MODEL, D_FF_EXPERT), D_MODEL ** -0.5),
        "w3": nrm(ks[21], (L, N_EXPERTS, D_MODEL, D_FF_EXPERT), D_MODEL ** -0.5),
        "w2": nrm(ks[22], (L, N_EXPERTS, D_FF_EXPERT, D_MODEL), D_FF_EXPERT ** -0.5),
        "g_final": 1.0 + nrm(ks[23], (D_MODEL,), 0.02),
    }


def reference(x, mem, rel_bias, g_mix, w_in, sinks_a, w_a_out, w_b_out, w_gate, b_gate, w_o,
              g_x, g_mem, w_xq, w_xk, w_xv, w_xo, g_moe, w_rg, w_re, w1, w3, w2, g_final):
    b, s, d = x.shape
    scale = HEAD_DIM ** -0.5
    bias_a = rel_bias[:, :A_Q_HEADS].reshape(REL_BUCKETS, A_KV_HEADS, A_GROUP)
    bias_b = rel_bias[:, A_Q_HEADS:]
    for l in range(DEPTH):
        h = rmsnorm(x, g_mix[l])
        qa, ka, va, qb, kb, vb = jnp.split(h @ w_in[l], IN_SPLITS, axis=-1)
        qa = qa.reshape(b, s, A_KV_HEADS, A_GROUP, HEAD_DIM) * scale
        ka = ka.reshape(b, s, A_KV_HEADS, HEAD_DIM)
        va = va.reshape(b, s, A_KV_HEADS, HEAD_DIM)
        oa, _ = banded_attention(qa, ka, va, bias_a, A_WINDOW - 1, 1,
                                 sinks=sinks_a[l].reshape(A_KV_HEADS, A_GROUP))
        ya = oa.reshape(b, s, A_WIDTH) @ w_a_out[l]
        ob = dilated_attention(qb.reshape(b, s, B_HEADS, HEAD_DIM) * scale,
                               kb.reshape(b, s, B_HEADS, HEAD_DIM),
                               vb.reshape(b, s, B_HEADS, HEAD_DIM), bias_b)
        yb = ob.reshape(b, s, B_WIDTH) @ w_b_out[l]
        gates = jax.nn.sigmoid((h @ w_gate[l] + b_gate[l]).astype(jnp.float32)).astype(x.dtype)
        ga, gb = gates[..., :D_MODEL], gates[..., D_MODEL:]
        x = x + (ga * ya + gb * yb) @ w_o[l]
        x = x + cross_attention(rmsnorm(x, g_x[l]), rmsnorm(mem, g_mem[l]),
                                w_xq[l], w_xk[l], w_xv[l], w_xo[l])
        x = x + hier_moe(rmsnorm(x, g_moe[l]), w_rg[l], w_re[l], w1[l], w3[l], w2[l])
    return rmsnorm(x, g_final)
```

```python
import functools
import math

import jax
import jax.numpy as jnp
from jax import lax
from jax.experimental import pallas as pl
from jax.experimental.pallas import tpu as pltpu

F32 = jnp.float32
BF16 = jnp.bfloat16
I32 = jnp.int32
U32 = jnp.uint32

HEAD_DIM = 64
PAIR = 2 * HEAD_DIM
A_Q_HEADS = 16
A_KV_HEADS = 4
A_GROUP = A_Q_HEADS // A_KV_HEADS
A_WIDTH = A_Q_HEADS * HEAD_DIM
A_KV_WIDTH = A_KV_HEADS * HEAD_DIM
A_WINDOW = 128
B_HEADS = 12
B_WIDTH = B_HEADS * HEAD_DIM
B_PATTERNS = ((128, 1), (512, 4), (2048, 16))
BLOCK = 128
IN_COLS = A_WIDTH + 2 * A_KV_WIDTH + 3 * B_WIDTH
REL_BUCKETS = 32
REL_MAX_DIST = 2048
X_HEADS = 4
X_HEAD_DIM = 128
X_WIDTH = X_HEADS * X_HEAD_DIM
N_GROUPS = 4
EXP_PER_GROUP = 8
N_EXPERTS = N_GROUPS * EXP_PER_GROUP
TOP_K = 2
D_FF = 512
EPS = 1e-6
NEG = -1e30
ROUTER_ROWS = 40
MOE_ROWS = 256
VMEM_LIMIT = 56 * 1024 * 1024


def _cparams(n_axes):
    return pltpu.CompilerParams(dimension_semantics=("arbitrary",) * n_axes,
                                vmem_limit_bytes=VMEM_LIMIT)


def _rms(xf, g):
    return xf * lax.rsqrt(jnp.mean(xf * xf, axis=-1, keepdims=True) + EPS) * g


def _dot_nt(a, b):
    return lax.dot_general(a, b, (((1,), (1,)), ((), ())), preferred_element_type=F32)


def _norm_proj_kernel(x_ref, g_ref, w_ref, b_ref, o_ref, h_ref, *, gate):
    @pl.when(pl.program_id(1) == 0)
    def _():
        h_ref[...] = _rms(x_ref[...], g_ref[...]).astype(BF16)

    acc = jnp.dot(h_ref[...], w_ref[...], preferred_element_type=F32)
    if gate:
        acc = jax.nn.sigmoid(acc + b_ref[...])
    o_ref[...] = acc.astype(o_ref.dtype)


def _norm_proj(x2d, g, w, b, *, gate, tm, tn, name):
    t, d = x2d.shape
    n = w.shape[1]
    return pl.pallas_call(
        functools.partial(_norm_proj_kernel, gate=gate),
        grid=(t // tm, n // tn),
        in_specs=[
            pl.BlockSpec((tm, d), lambda i, j: (i, 0)),
            pl.BlockSpec((1, d), lambda i, j: (0, 0)),
            pl.BlockSpec((d, tn), lambda i, j: (0, j)),
            pl.BlockSpec((1, tn), lambda i, j: (0, j)),
        ],
        out_specs=pl.BlockSpec((tm, tn), lambda i, j: (i, j)),
        out_shape=jax.ShapeDtypeStruct((t, n), BF16),
        scratch_shapes=[pltpu.VMEM((tm, d), BF16)],
        compiler_params=_cparams(2),
        name=name,
    )(x2d, g.reshape(1, d), w, b.reshape(1, n))


def t5_bucket(dist):
    max_exact = REL_BUCKETS // 2
    d = jnp.maximum(dist, 0)
    df = jnp.maximum(d, 1).astype(jnp.float32)
    large = max_exact + (jnp.log(df / max_exact) / math.log(REL_MAX_DIST / max_exact)
                         * (REL_BUCKETS - max_exact)).astype(jnp.int32)
    large = jnp.minimum(large, REL_BUCKETS - 1)
    return jnp.where(d < max_exact, d, large)


def _band_bias(table, step, max_off):
    qi = jnp.arange(BLOCK)[:, None]
    ki = jnp.arange(2 * BLOCK)[None, :]
    dist = qi + BLOCK - ki
    bias = jnp.transpose(table[t5_bucket(dist * step)], (2, 0, 1)).astype(F32)
    band = (dist >= 0) & (dist <= max_off)
    rest = jnp.where(band[None], bias, NEG)
    first = jnp.where((band & (ki >= BLOCK))[None], bias, NEG)
    return jnp.stack([first, rest])


def _softmax_pv(s, v_pair, sink):
    m = jnp.max(s, axis=-1, keepdims=True)
    if sink is not None:
        m = jnp.maximum(m, sink)
    p = jnp.exp(s - m)
    l = jnp.sum(p, axis=-1, keepdims=True)
    if sink is not None:
        l = l + jnp.exp(sink - m)
    o = jnp.dot(p.astype(BF16), v_pair, preferred_element_type=F32)
    return o, m, l


def _swa_kernel(sink_ref, q_ref, kp_ref, kc_ref, vp_ref, vc_ref, bias_ref, o_ref):
    q = q_ref[0] * jnp.asarray(HEAD_DIM ** -0.5, BF16)
    k = jnp.concatenate([kp_ref[0], kc_ref[0]], axis=0)
    v = jnp.concatenate([vp_ref[0], vc_ref[0]], axis=0)
    lane = lax.broadcasted_iota(I32, (1, PAIR), 1)
    low = lane < HEAD_DIM
    for kv_pair in range(A_KV_HEADS // 2):
        k_pair = k[:, kv_pair * PAIR:(kv_pair + 1) * PAIR]
        v_pair = v[:, kv_pair * PAIR:(kv_pair + 1) * PAIR]
        k_half = (jnp.where(low, k_pair, jnp.zeros_like(k_pair)),
                  jnp.where(low, jnp.zeros_like(k_pair), k_pair))
        for g in range(A_GROUP):
            col = (kv_pair * A_GROUP + g) * PAIR
            q_pair = q[:, col:col + PAIR]
            outs = []
            for half in range(2):
                head = (2 * kv_pair + half) * A_GROUP + g
                s = _dot_nt(q_pair, k_half[half]) + bias_ref[0, head]
                o, _, l = _softmax_pv(s, v_pair, sink_ref[head])
                outs.append(o * (1.0 / l))
            o_ref[0, :, col:col + PAIR] = jnp.where(low, outs[0], outs[1]).astype(o_ref.dtype)


def _swa_attention(qkv3, bias, sinks):
    b, s, _ = qkv3.shape
    nb = s // BLOCK
    kcol = A_WIDTH // A_KV_WIDTH
    prev = lambda i: jnp.maximum(i - 1, 0)
    return pl.pallas_call(
        _swa_kernel,
        grid=(b, nb),
        in_specs=[
            pl.BlockSpec(memory_space=pltpu.SMEM),
            pl.BlockSpec((1, BLOCK, A_WIDTH), lambda bi, i: (bi, i, 0)),
            pl.BlockSpec((1, BLOCK, A_KV_WIDTH), lambda bi, i: (bi, prev(i), kcol)),
            pl.BlockSpec((1, BLOCK, A_KV_WIDTH), lambda bi, i: (bi, i, kcol)),
            pl.BlockSpec((1, BLOCK, A_KV_WIDTH), lambda bi, i: (bi, prev(i), kcol + 1)),
            pl.BlockSpec((1, BLOCK, A_KV_WIDTH), lambda bi, i: (bi, i, kcol + 1)),
            pl.BlockSpec((1, A_Q_HEADS, BLOCK, 2 * BLOCK),
                         lambda bi, i: (jnp.minimum(i, 1), 0, 0, 0)),
        ],
        out_specs=pl.BlockSpec((1, BLOCK, A_WIDTH), lambda bi, i: (bi, i, 0)),
        out_shape=jax.ShapeDtypeStruct((b, s, A_WIDTH), BF16),
        compiler_params=_cparams(2),
        name="swa_attention",
    )(sinks, qkv3, qkv3, qkv3, qkv3, qkv3, bias)


def _dilated_kernel(q_ref, kp_ref, kc_ref, vp_ref, vc_ref, bias_ref, o_ref, lse_ref):
    q = q_ref[0] * jnp.asarray(HEAD_DIM ** -0.5, BF16)
    k = jnp.concatenate([kp_ref[0], kc_ref[0]], axis=0)
    v = jnp.concatenate([vp_ref[0], vc_ref[0]], axis=0)
    lane = lax.broadcasted_iota(I32, (1, PAIR), 1)
    low = lane < HEAD_DIM
    for pair in range(B_HEADS // 2):
        col = pair * PAIR
        q_pair = q[:, col:col + PAIR]
        k_pair = k[:, col:col + PAIR]
        v_pair = v[:, col:col + PAIR]
        k_half = (jnp.where(low, k_pair, jnp.zeros_like(k_pair)),
                  jnp.where(low, jnp.zeros_like(k_pair), k_pair))
        outs, lses = [], []
        for half in range(2):
            s = _dot_nt(q_pair, k_half[half]) + bias_ref[0, 2 * pair + half]
            o, m, l = _softmax_pv(s, v_pair, None)
            outs.append(o * (1.0 / l))
            lses.append(m + jnp.log(l))
        o_ref[0, :, col:col + PAIR] = jnp.where(low, outs[0], outs[1])
        lse_ref[0, :, col:col + PAIR] = jnp.where(low, lses[0], lses[1])


def _dilated_attention(qkv3, bias, r):
    b, s, cols = qkv3.shape
    n = s // r
    nb = n // BLOCK
    view = qkv3.reshape(b, n, r * cols)
    per = cols // B_WIDTH
    qcol = (A_WIDTH + 2 * A_KV_WIDTH) // B_WIDTH
    prev = lambda j: jnp.maximum(j - 1, 0)
    out_shape = jax.ShapeDtypeStruct((b, n, r * B_WIDTH), F32)
    out_spec = pl.BlockSpec((1, BLOCK, B_WIDTH), lambda bi, c, j: (bi, j, c))
    o, lse = pl.pallas_call(
        _dilated_kernel,
        grid=(b, r, nb),
        in_specs=[
            pl.BlockSpec((1, BLOCK, B_WIDTH), lambda bi, c, j: (bi, j, c * per + qcol)),
            pl.BlockSpec((1, BLOCK, B_WIDTH), lambda bi, c, j: (bi, prev(j), c * per + qcol + 1)),
            pl.BlockSpec((1, BLOCK, B_WIDTH), lambda bi, c, j: (bi, j, c * per + qcol + 1)),
            pl.BlockSpec((1, BLOCK, B_WIDTH), lambda bi, c, j: (bi, prev(j), c * per + qcol + 2)),
            pl.BlockSpec((1, BLOCK, B_WIDTH), lambda bi, c, j: (bi, j, c * per + qcol + 2)),
            pl.BlockSpec((1, B_HEADS, BLOCK, 2 * BLOCK),
                         lambda bi, c, j: (jnp.minimum(j, 1), 0, 0, 0)),
        ],
        out_specs=[out_spec, out_spec],
        out_shape=[out_shape, out_shape],
        compiler_params=_cparams(3),
        name=f"dilated_attention_r{r}",
    )(view, view, view, view, view, bias)
    return o.reshape(b * s, B_WIDTH), lse.reshape(b * s, B_WIDTH)


def _mix_out_kernel(x_ref, oa_ref, o1_ref, o2_ref, o3_ref, l1_ref, l2_ref, l3_ref, ga_ref, gb_ref,
                    wa_ref, wb_ref, wo_ref, out_ref):
    l1, l2, l3 = l1_ref[...], l2_ref[...], l3_ref[...]
    m = jnp.maximum(jnp.maximum(l1, l2), l3)
    e1, e2, e3 = jnp.exp(l1 - m), jnp.exp(l2 - m), jnp.exp(l3 - m)
    inv = 1.0 / (e1 + e2 + e3)
    ob = (e1 * inv) * o1_ref[...] + (e2 * inv) * o2_ref[...] + (e3 * inv) * o3_ref[...]
    ya = jnp.dot(oa_ref[...], wa_ref[...], preferred_element_type=F32)
    yb = jnp.dot(ob.astype(BF16), wb_ref[...], preferred_element_type=F32)
    mixed = ga_ref[...].astype(F32) * ya + gb_ref[...].astype(F32) * yb
    out_ref[...] = x_ref[...] + jnp.dot(mixed.astype(BF16), wo_ref[...], preferred_element_type=F32)


def _mix_out(x2d, oa, o_list, lse_list, gates, wa, wb, wo, *, tm):
    t, d = x2d.shape
    row = lambda w: pl.BlockSpec((tm, w), lambda i: (i, 0))
    const = lambda shape: pl.BlockSpec(shape, lambda i: (0, 0))
    return pl.pallas_call(
        _mix_out_kernel,
        grid=(t // tm,),
        in_specs=[row(d), row(A_WIDTH)] + [row(B_WIDTH)] * 6
                 + [pl.BlockSpec((tm, d), lambda i: (i, 0)), pl.BlockSpec((tm, d), lambda i: (i, 1)),
                    const(wa.shape), const(wb.shape), const(wo.shape)],
        out_specs=row(d),
        out_shape=jax.ShapeDtypeStruct((t, d), F32),
        compiler_params=_cparams(1),
        name="mix_out",
    )(x2d, oa, *o_list, *lse_list, gates, gates, wa, wb, wo)


def _mem_kv_kernel(mem_ref, g_ref, wk_ref, wv_ref, k_ref, v_ref):
    hn = _rms(mem_ref[0], g_ref[...]).astype(BF16)
    k_ref[0] = jnp.dot(hn, wk_ref[...], preferred_element_type=F32).astype(BF16)
    v_ref[0] = jnp.dot(hn, wv_ref[...], preferred_element_type=F32).astype(BF16)


def _mem_kv(mem, g, wk, wv):
    b, ml, d = mem.shape
    const = lambda shape: pl.BlockSpec(shape, lambda bi: (0,) * len(shape))
    out = jax.ShapeDtypeStruct((b, ml, X_WIDTH), BF16)
    blk = pl.BlockSpec((1, ml, X_WIDTH), lambda bi: (bi, 0, 0))
    return pl.pallas_call(
        _mem_kv_kernel,
        grid=(b,),
        in_specs=[pl.BlockSpec((1, ml, d), lambda bi: (bi, 0, 0)), const((1, d)),
                  const(wk.shape), const(wv.shape)],
        out_specs=[blk, blk],
        out_shape=[out, out],
        compiler_params=_cparams(1),
        name="mem_kv",
    )(mem, g.reshape(1, d), wk, wv)


def _cross_kernel(x_ref, g_ref, wq_ref, k_ref, v_ref, wo_ref, out_ref):
    x = x_ref[0]
    h = _rms(x, g_ref[...]).astype(BF16)
    q = (jnp.dot(h, wq_ref[...], preferred_element_type=F32) * (X_HEAD_DIM ** -0.5)).astype(BF16)
    k, v = k_ref[0], v_ref[0]
    outs = []
    for hd in range(X_HEADS):
        sl = slice(hd * X_HEAD_DIM, (hd + 1) * X_HEAD_DIM)
        s = _dot_nt(q[:, sl], k[:, sl])
        m = jnp.max(s, axis=-1, keepdims=True)
        p = jnp.exp(s - m)
        l = jnp.sum(p, axis=-1, keepdims=True)
        o = jnp.dot(p.astype(BF16), v[:, sl], preferred_element_type=F32)
        outs.append((o * (1.0 / l)).astype(BF16))
    o = jnp.concatenate(outs, axis=1)
    out_ref[0] = x + jnp.dot(o, wo_ref[...], preferred_element_type=F32)


def _cross_attention(x3, g, wq, k, v, wo, *, tm):
    b, s, d = x3.shape
    ml = k.shape[1]
    const = lambda shape: pl.BlockSpec(shape, lambda bi, i: (0,) * len(shape))
    return pl.pallas_call(
        _cross_kernel,
        grid=(b, s // tm),
        in_specs=[pl.BlockSpec((1, tm, d), lambda bi, i: (bi, i, 0)), const((1, d)), const(wq.shape),
                  pl.BlockSpec((1, ml, X_WIDTH), lambda bi, i: (bi, 0, 0)),
                  pl.BlockSpec((1, ml, X_WIDTH), lambda bi, i: (bi, 0, 0)),
                  const(wo.shape)],
        out_specs=pl.BlockSpec((1, tm, d), lambda bi, i: (bi, i, 0)),
        out_shape=jax.ShapeDtypeStruct((b, s, d), F32),
        compiler_params=_cparams(2),
        name="cross_attention",
    )(x3, g.reshape(1, d), wq, k, v, wo)


def _first_argmax(vals, rows, n):
    m = jnp.max(vals, axis=0, keepdims=True)
    idx = jnp.min(jnp.where(vals == m, rows, n), axis=0, keepdims=True)
    return m, idx


def _router_kernel(x_ref, g_ref, wr_ref, hp_ref, e_ref, gate_ref, cnt_ref):
    h = _rms(x_ref[...], g_ref[...])
    tm, d = h.shape
    logits = lax.dot_general(wr_ref[...], h, (((1,), (1,)), ((), ())),
                             precision=lax.Precision.HIGHEST, preferred_element_type=F32)
    rows8 = lax.broadcasted_iota(I32, (EXP_PER_GROUP, tm), 0)
    gl = jnp.where(rows8 < N_GROUPS, logits[N_EXPERTS:N_EXPERTS + 8], -jnp.inf)
    gmax, gidx = _first_argmax(gl, rows8, 8)
    g_gate = 1.0 / jnp.sum(jnp.exp(gl - gmax), axis=0, keepdims=True)
    sel = jnp.zeros((EXP_PER_GROUP, tm), F32)
    for grp in range(N_GROUPS):
        sel = jnp.where(gidx == grp, logits[grp * EXP_PER_GROUP:(grp + 1) * EXP_PER_GROUP], sel)
    v1, i1 = _first_argmax(sel, rows8, 8)
    sel2 = jnp.where(rows8 == i1, -jnp.inf, sel)
    v2, i2 = _first_argmax(sel2, rows8, 8)
    e2x = jnp.exp(v2 - v1)
    den = 1.0 + e2x
    e_ref[...] = jnp.concatenate([gidx * EXP_PER_GROUP + i1, gidx * EXP_PER_GROUP + i2], axis=0)
    gate_ref[...] = jnp.concatenate([(1.0 / den) * g_gate, (e2x / den) * g_gate], axis=0)

    bits = lax.bitcast_convert_type(h.astype(BF16).astype(F32), U32)
    hp_ref[...] = (bits[:, :d // 2] >> 16) | (bits[:, d // 2:] & jnp.uint32(0xFFFF0000))

    @pl.when(pl.program_id(0) == 0)
    def _():
        cnt_ref[...] = jnp.zeros_like(cnt_ref)

    rows32 = lax.broadcasted_iota(I32, (N_EXPERTS, tm), 0)
    e = e_ref[...]
    hits = (rows32 == e[0:1]).astype(F32) + (rows32 == e[1:2]).astype(F32)
    cnt_ref[...] += jnp.sum(hits, axis=1, keepdims=True)


def _router(x2d, g, wr_t, *, tm):
    t, d = x2d.shape
    return pl.pallas_call(
        _router_kernel,
        grid=(t // tm,),
        in_specs=[pl.BlockSpec((tm, d), lambda i: (i, 0)),
                  pl.BlockSpec((1, d), lambda i: (0, 0)),
                  pl.BlockSpec((ROUTER_ROWS, d), lambda i: (0, 0))],
        out_specs=[pl.BlockSpec((tm, d // 2), lambda i: (i, 0)),
                   pl.BlockSpec((TOP_K, tm), lambda i: (0, i)),
                   pl.BlockSpec((TOP_K, tm), lambda i: (0, i)),
                   pl.BlockSpec((N_EXPERTS, 128), lambda i: (0, 0))],
        out_shape=[jax.ShapeDtypeStruct((t, d // 2), U32),
                   jax.ShapeDtypeStruct((TOP_K, t), I32),
                   jax.ShapeDtypeStruct((TOP_K, t), F32),
                   jax.ShapeDtypeStruct((N_EXPERTS, 128), F32)],
        compiler_params=_cparams(1),
        name="moe_router",
    )(x2d, g.reshape(1, d), wr_t)


def _dest_kernel(e_ref, start_ref, d_ref, carry_ref):
    @pl.when(pl.program_id(0) == 0)
    def _():
        carry_ref[...] = jnp.zeros_like(carry_ref)

    e = e_ref[...]
    tm = e.shape[1]
    rows32 = lax.broadcasted_iota(I32, (N_EXPERTS, tm), 0)
    oh0 = (rows32 == e[0:1]).astype(F32)
    oh1 = (rows32 == e[1:2]).astype(F32)
    hits = oh0 + oh1
    earlier = (lax.broadcasted_iota(I32, (tm, tm), 0) < lax.broadcasted_iota(I32, (tm, tm), 1))
    prefix = jnp.dot(hits.astype(BF16), earlier.astype(BF16), preferred_element_type=F32)
    base = prefix + carry_ref[:, 0:1] + start_ref[:, 0:1]
    d0 = jnp.sum(oh0 * base, axis=0, keepdims=True)
    d1 = jnp.sum(oh1 * base, axis=0, keepdims=True)
    d_ref[...] = jnp.concatenate([d0, d1], axis=0).astype(I32)
    carry_ref[...] += jnp.sum(hits, axis=1, keepdims=True)


def _assignment_rows(e_t, start, *, tm):
    t = e_t.shape[1]
    return pl.pallas_call(
        _dest_kernel,
        grid=(t // tm,),
        in_specs=[pl.BlockSpec((TOP_K, tm), lambda i: (0, i)),
                  pl.BlockSpec((N_EXPERTS, 128), lambda i: (0, 0))],
        out_specs=pl.BlockSpec((TOP_K, tm), lambda i: (0, i)),
        out_shape=jax.ShapeDtypeStruct((TOP_K, t), I32),
        scratch_shapes=[pltpu.VMEM((N_EXPERTS, 128), F32)],
        compiler_params=_cparams(1),
        name="moe_assignment_rows",
    )(e_t, start)


def _row_copy(src, dst, sem):
    return pltpu.make_async_copy(src, dst, sem)


def _dispatch_kernel(d_ref, hp_ref, xg_in_ref, xg_ref, sem):
    del xg_in_ref
    tm = hp_ref.shape[0]

    def start(t, c):
        for k in range(TOP_K):
            _row_copy(hp_ref.at[pl.ds(t, 1)], xg_ref.at[pl.ds(d_ref[k, t], 1)], sem).start()
        return c

    lax.fori_loop(0, tm, start, 0)

    def wait(t, c):
        for k in range(TOP_K):
            _row_copy(hp_ref.at[pl.ds(0, 1)], xg_ref.at[pl.ds(0, 1)], sem).wait()
        return c

    lax.fori_loop(0, tm, wait, 0)


def _dispatch(dest, hp, n_rows, *, tm):
    t, w = hp.shape
    return pl.pallas_call(
        _dispatch_kernel,
        grid=(t // tm,),
        in_specs=[pl.BlockSpec((TOP_K, tm), lambda i: (0, i), memory_space=pltpu.SMEM),
                  pl.BlockSpec((tm, w), lambda i: (i, 0)),
                  pl.BlockSpec(memory_space=pl.ANY)],
        out_specs=pl.BlockSpec(memory_space=pl.ANY),
        out_shape=jax.ShapeDtypeStruct((n_rows, w), U32),
        scratch_shapes=[pltpu.SemaphoreType.DMA(())],
        input_output_aliases={2: 0},
        compiler_params=_cparams(1),
        name="moe_dispatch",
    )(dest, hp, jnp.zeros((n_rows, w), U32))


def _expert_kernel(be_ref, nu_ref, xg_ref, w13_ref, w2_ref, y_ref):
    del be_ref

    @pl.when(pl.program_id(0) < nu_ref[0])
    def _():
        pk = xg_ref[...]
        lo = lax.bitcast_convert_type(pk << 16, F32)
        hi = lax.bitcast_convert_type(pk & jnp.uint32(0xFFFF0000), F32)
        x = jnp.concatenate([lo, hi], axis=1).astype(BF16)
        h = jnp.dot(x, w13_ref[0], preferred_element_type=F32)
        a = (jax.nn.silu(h[:, :D_FF]) * h[:, D_FF:]).astype(BF16)
        y_ref[...] = jnp.dot(a, w2_ref[0], preferred_element_type=F32)

    @pl.when(pl.program_id(0) >= nu_ref[0])
    def _():
        y_ref[...] = jnp.zeros_like(y_ref)


def _experts(block_e, n_used, xg, w13, w2):
    n_rows, w = xg.shape
    d = w2.shape[2]
    nblk = n_rows // MOE_ROWS
    live = lambda i, nu: jnp.minimum(i, nu[0] - 1)
    return pl.pallas_call(
        _expert_kernel,
        grid_spec=pltpu.PrefetchScalarGridSpec(
            num_scalar_prefetch=2,
            grid=(nblk,),
            in_specs=[pl.BlockSpec((MOE_ROWS, w), lambda i, be, nu: (live(i, nu), 0)),
                      pl.BlockSpec((1, d, 2 * D_FF), lambda i, be, nu: (be[live(i, nu)], 0, 0)),
                      pl.BlockSpec((1, D_FF, d), lambda i, be, nu: (be[live(i, nu)], 0, 0))],
            out_specs=pl.BlockSpec((MOE_ROWS, d), lambda i, be, nu: (i, 0)),
        ),
        out_shape=jax.ShapeDtypeStruct((n_rows, d), F32),
        compiler_params=_cparams(1),
        name="moe_experts",
    )(block_e, n_used, xg, w13, w2)


def _combine_kernel(dc_ref, dn_ref, x_ref, gate_ref, gf_ref, y_ref, out_ref, buf, sem, *, final):
    i = pl.program_id(0)
    n = pl.num_programs(0)
    tm = x_ref.shape[0]

    def issue(d_ref, slot):
        def body(t, c):
            for k in range(TOP_K):
                _row_copy(y_ref.at[pl.ds(d_ref[k, t], 1)], buf.at[slot, k, pl.ds(t, 1)],
                          sem.at[slot]).start()
            return c
        lax.fori_loop(0, tm, body, 0)

    @pl.when(i == 0)
    def _():
        issue(dc_ref, 0)

    @pl.when(i + 1 < n)
    def _():
        issue(dn_ref, (i + 1) % 2)

    slot = i % 2

    def wait(t, c):
        for k in range(TOP_K):
            _row_copy(y_ref.at[pl.ds(0, 1)], buf.at[slot, k, pl.ds(0, 1)], sem.at[slot]).wait()
        return c

    lax.fori_loop(0, tm, wait, 0)
    g = gate_ref[...]
    y = x_ref[...] + (g[:, 0:1] * buf[slot, 0] + g[:, 1:2] * buf[slot, 1])
    out_ref[...] = _rms(y, gf_ref[...]) if final else y


def _combine(dest, x2d, gates_tok, g_final, y, *, final, tm):
    t, d = x2d.shape
    nt = t // tm
    return pl.pallas_call(
        functools.partial(_combine_kernel, final=final),
        grid=(nt,),
        in_specs=[pl.BlockSpec((TOP_K, tm), lambda i: (0, i), memory_space=pltpu.SMEM),
                  pl.BlockSpec((TOP_K, tm), lambda i: (0, jnp.minimum(i + 1, nt - 1)),
                               memory_space=pltpu.SMEM),
                  pl.BlockSpec((tm, d), lambda i: (i, 0)),
                  pl.BlockSpec((tm, TOP_K), lambda i: (i, 0)),
                  pl.BlockSpec((1, d), lambda i: (0, 0)),
                  pl.BlockSpec(memory_space=pl.ANY)],
        out_specs=pl.BlockSpec((tm, d), lambda i: (i, 0)),
        out_shape=jax.ShapeDtypeStruct((t, d), F32),
        scratch_shapes=[pltpu.VMEM((2, TOP_K, tm, d), F32), pltpu.SemaphoreType.DMA((2,))],
        compiler_params=_cparams(1),
        name="moe_combine",
    )(dest, dest, x2d, gates_tok, g_final.reshape(1, d), y)


def _swa_q_order():
    heads = []
    for kv_pair in range(A_KV_HEADS // 2):
        for g in range(A_GROUP):
            heads += [(2 * kv_pair) * A_GROUP + g, (2 * kv_pair + 1) * A_GROUP + g]
    return jnp.asarray([h * HEAD_DIM + c for h in heads for c in range(HEAD_DIM)], I32)


def kernel(x, mem, rel_bias, g_mix, w_in, sinks_a, w_a_out, w_b_out, w_gate, b_gate, w_o,
           g_x, g_mem, w_xq, w_xk, w_xv, w_xo, g_moe, w_rg, w_re, w1, w3, w2, g_final):
    b, s, d = x.shape
    t = b * s
    depth = g_mix.shape[0]
    perm = _swa_q_order()
    bias_a = _band_bias(rel_bias[:, :A_Q_HEADS], 1, A_WINDOW - 1)
    bias_b = [_band_bias(rel_bias[:, A_Q_HEADS:], r, w // r) for w, r in B_PATTERNS]
    n_assign = t * TOP_K
    n_rows = -(-(n_assign + N_EXPERTS * (MOE_ROWS - 1)) // MOE_ROWS) * MOE_ROWS
    n_blocks = n_rows // MOE_ROWS

    x2d = x.reshape(t, d)
    for l in range(depth):
        w_in_l = jnp.concatenate([w_in[l][:, :A_WIDTH][:, perm], w_in[l][:, A_WIDTH:]], axis=1).astype(BF16)
        qkv = _norm_proj(x2d, g_mix[l], w_in_l, jnp.zeros((IN_COLS,), F32),
                         gate=False, tm=1024, tn=768, name="in_proj")
        gates = _norm_proj(x2d, g_mix[l], w_gate[l].astype(BF16), b_gate[l],
                           gate=True, tm=1024, tn=1024, name="gate_proj")
        qkv3 = qkv.reshape(b, s, IN_COLS)
        oa = _swa_attention(qkv3, bias_a, sinks_a[l]).reshape(t, A_WIDTH)
        o_list, lse_list = [], []
        for (w, r), bias in zip(B_PATTERNS, bias_b):
            o, lse = _dilated_attention(qkv3, bias, r)
            o_list.append(o)
            lse_list.append(lse)
        x2d = _mix_out(x2d, oa, o_list, lse_list, gates, w_a_out[l][perm].astype(BF16),
                       w_b_out[l].astype(BF16), w_o[l].astype(BF16), tm=256)
        k_mem, v_mem = _mem_kv(mem, g_mem[l], w_xk[l].astype(BF16), w_xv[l].astype(BF16))
        x2d = _cross_attention(x2d.reshape(b, s, d), g_x[l], w_xq[l].astype(BF16), k_mem, v_mem,
                               w_xo[l].astype(BF16), tm=512).reshape(t, d)
        wr_t = jnp.concatenate([w_re[l].T, w_rg[l].T,
                                jnp.zeros((ROUTER_ROWS - N_EXPERTS - N_GROUPS, d), F32)], axis=0)
        hp, e_t, gate_t, cnt = _router(x2d, g_moe[l], wr_t, tm=512)
        counts = cnt[:, 0].astype(I32)
        padded = (counts + MOE_ROWS - 1) // MOE_ROWS * MOE_ROWS
        seg_end = jnp.cumsum(padded)
        seg_start = seg_end - padded
        n_used = (seg_end[-1] // MOE_ROWS).astype(I32).reshape(1)
        block_e = jnp.minimum(jnp.searchsorted(seg_end, jnp.arange(n_blocks, dtype=I32) * MOE_ROWS,
                                               side='right'), N_EXPERTS - 1).astype(I32)
        start = jnp.broadcast_to(seg_start.astype(F32)[:, None], (N_EXPERTS, 128))
        dest = _assignment_rows(e_t, start, tm=512)
        xg = _dispatch(dest, hp, n_rows, tm=256)
        w13 = jnp.concatenate([w1[l], w3[l]], axis=2).astype(BF16)
        y = _experts(block_e, n_used, xg, w13, w2[l].astype(BF16))
        x2d = _combine(dest, x2d, gate_t.T, g_final, y, final=(l + 1 == depth), tm=128)
    return x2d.reshape(b, s, d)
```

```python
import functools
import math

import jax
import jax.numpy as jnp
from jax import lax
from jax.experimental import pallas as pl
from jax.experimental.pallas import tpu as pltpu

F32 = jnp.float32
BF16 = jnp.bfloat16
I32 = jnp.int32
U32 = jnp.uint32

HEAD_DIM = 64
PAIR = 2 * HEAD_DIM
A_Q_HEADS = 16
A_KV_HEADS = 4
A_GROUP = A_Q_HEADS // A_KV_HEADS
A_WIDTH = A_Q_HEADS * HEAD_DIM
A_KV_WIDTH = A_KV_HEADS * HEAD_DIM
A_WINDOW = 128
B_HEADS = 12
B_WIDTH = B_HEADS * HEAD_DIM
LANES = 128
LANE_CHUNKS = B_WIDTH // LANES
D_MODEL = 2048
PACKED_TILE_ROWS = D_MODEL // 2 // LANES
F32_TILE_ROWS = D_MODEL // LANES
B_PATTERNS = ((128, 1), (512, 4), (2048, 16))
BLOCK = 128
IN_COLS = A_WIDTH + 2 * A_KV_WIDTH + 3 * B_WIDTH
REL_BUCKETS = 32
REL_MAX_DIST = 2048
X_HEADS = 4
X_HEAD_DIM = 128
X_WIDTH = X_HEADS * X_HEAD_DIM
N_GROUPS = 4
EXP_PER_GROUP = 8
N_EXPERTS = N_GROUPS * EXP_PER_GROUP
TOP_K = 2
D_FF = 512
EPS = 1e-6
NEG = -1e30
ROUTER_ROWS = 40
MOE_ROWS = 256
VMEM_LIMIT = 56 * 1024 * 1024


def _cparams(n_axes):
    return pltpu.CompilerParams(dimension_semantics=("arbitrary",) * n_axes,
                                vmem_limit_bytes=VMEM_LIMIT)


def _rms(xf, g):
    return xf * lax.rsqrt(jnp.mean(xf * xf, axis=-1, keepdims=True) + EPS) * g


def _dot_nt(a, b):
    return lax.dot_general(a, b, (((1,), (1,)), ((), ())), preferred_element_type=F32)


def _in_proj_kernel(x_ref, g_ref, w_ref, oa_ref, ob_ref, q4_ref, k4_ref, v4_ref, q16_ref, k16_ref,
                    v16_ref, h_ref, acc_ref):
    j = pl.program_id(1)
    tm = x_ref.shape[0]

    @pl.when(j == 0)
    def _():
        h_ref[...] = _rms(x_ref[...], g_ref[...]).astype(BF16)

    acc = jnp.dot(h_ref[...], w_ref[...], preferred_element_type=F32)

    @pl.when(j < 2)
    def _():
        oa_ref[...] = acc.astype(BF16)

    @pl.when(j >= 2)
    def _():
        ob_ref[...] = acc.astype(BF16)
        for cc in range(LANE_CHUNKS):
            acc_ref[cc] = acc[:, cc * LANES:(cc + 1) * LANES]

    for part, refs in enumerate(((q4_ref, q16_ref), (k4_ref, k16_ref), (v4_ref, v16_ref))):
        @pl.when(j == 2 + part)
        def _():
            for r, ref in zip((4, 16), refs):
                for c in range(r):
                    ref[0, c] = jnp.concatenate(
                        [acc_ref[cc, pl.ds(c, tm // r, stride=r), :] for cc in range(LANE_CHUNKS)],
                        axis=1).astype(BF16)


def _in_proj(x2d, g, w, b, s, *, tm):
    t, d = x2d.shape
    tn = B_WIDTH
    tiles = s // tm
    res_shape = lambda r: jax.ShapeDtypeStruct((b, r, s // r, B_WIDTH), BF16)
    res_spec = lambda r: pl.BlockSpec((1, r, tm // r, B_WIDTH),
                                      lambda i, j: (i // tiles, 0, i % tiles, 0))
    return pl.pallas_call(
        _in_proj_kernel,
        grid=(t // tm, IN_COLS // tn),
        in_specs=[
            pl.BlockSpec((tm, d), lambda i, j: (i, 0)),
            pl.BlockSpec((1, d), lambda i, j: (0, 0)),
            pl.BlockSpec((d, tn), lambda i, j: (0, j)),
        ],
        out_specs=[pl.BlockSpec((tm, tn), lambda i, j: (i, jnp.minimum(j, 1))),
                   pl.BlockSpec((tm, tn), lambda i, j: (i, jnp.maximum(j - 2, 0)))]
                  + [res_spec(4)] * 3 + [res_spec(16)] * 3,
        out_shape=[jax.ShapeDtypeStruct((t, 2 * tn), BF16), jax.ShapeDtypeStruct((t, 3 * tn), BF16)]
                  + [res_shape(4)] * 3 + [res_shape(16)] * 3,
        scratch_shapes=[pltpu.VMEM((tm, d), BF16), pltpu.VMEM((LANE_CHUNKS, tm, LANES), F32)],
        compiler_params=_cparams(2),
        name="in_proj",
    )(x2d, g.reshape(1, d), w)


def _gate_proj_kernel(x_ref, g_ref, w_ref, b_ref, o_ref, h_ref):
    @pl.when(pl.program_id(1) == 0)
    def _():
        h_ref[...] = _rms(x_ref[...], g_ref[...]).astype(BF16)

    acc = jnp.dot(h_ref[...], w_ref[...], preferred_element_type=F32)
    o_ref[...] = jax.nn.sigmoid(acc + b_ref[...]).astype(o_ref.dtype)


def _gate_proj(x2d, g, w, b, *, tm, tn):
    t, d = x2d.shape
    n = w.shape[1]
    return pl.pallas_call(
        _gate_proj_kernel,
        grid=(t // tm, n // tn),
        in_specs=[
            pl.BlockSpec((tm, d), lambda i, j: (i, 0)),
            pl.BlockSpec((1, d), lambda i, j: (0, 0)),
            pl.BlockSpec((d, tn), lambda i, j: (0, j)),
            pl.BlockSpec((1, tn), lambda i, j: (0, j)),
        ],
        out_specs=pl.BlockSpec((tm, tn), lambda i, j: (i, j)),
        out_shape=jax.ShapeDtypeStruct((t, n), BF16),
        scratch_shapes=[pltpu.VMEM((tm, d), BF16)],
        compiler_params=_cparams(2),
        name="gate_proj",
    )(x2d, g.reshape(1, d), w, b.reshape(1, n))


def t5_bucket(dist):
    max_exact = REL_BUCKETS // 2
    d = jnp.maximum(dist, 0)
    df = jnp.maximum(d, 1).astype(jnp.float32)
    large = max_exact + (jnp.log(df / max_exact) / math.log(REL_MAX_DIST / max_exact)
                         * (REL_BUCKETS - max_exact)).astype(jnp.int32)
    large = jnp.minimum(large, REL_BUCKETS - 1)
    return jnp.where(d < max_exact, d, large)


def _band_bias(table, step, max_off):
    heads = table.shape[1]
    span = 3 * BLOCK
    dist = 2 * BLOCK - 1 - jnp.arange(span)
    z = jnp.where(((dist >= 0) & (dist <= max_off))[:, None], table[t5_bucket(dist * step)], NEG)
    z = z.T.astype(F32)
    rows = jnp.tile(z, (1, BLOCK))[:, :BLOCK * (span - 1)].reshape(heads, BLOCK, span - 1)
    rest = rows[:, :, BLOCK - 1:3 * BLOCK - 1]
    first = jnp.where(jnp.arange(2 * BLOCK)[None, None, :] >= BLOCK, rest, NEG)
    return jnp.stack([first, rest])


def _softmax_pv(s, v_pair, sink):
    m = jnp.max(s, axis=-1, keepdims=True)
    if sink is not None:
        m = jnp.maximum(m, sink)
    p = jnp.exp(s - m)
    l = jnp.sum(p, axis=-1, keepdims=True)
    if sink is not None:
        l = l + jnp.exp(sink - m)
    o = jnp.dot(p.astype(BF16), v_pair, preferred_element_type=F32)
    return o, m, l


def _swa_kernel(sink_ref, q_ref, kp_ref, kc_ref, vp_ref, vc_ref, bias_ref, o_ref):
    q = q_ref[0] * jnp.asarray(HEAD_DIM ** -0.5, BF16)
    k = jnp.concatenate([kp_ref[0], kc_ref[0]], axis=0)
    v = jnp.concatenate([vp_ref[0], vc_ref[0]], axis=0)
    lane = lax.broadcasted_iota(I32, (1, PAIR), 1)
    low = lane < HEAD_DIM
    for kv_pair in range(A_KV_HEADS // 2):
        k_pair = k[:, kv_pair * PAIR:(kv_pair + 1) * PAIR]
        v_pair = v[:, kv_pair * PAIR:(kv_pair + 1) * PAIR]
        k_half = (jnp.where(low, k_pair, jnp.zeros_like(k_pair)),
                  jnp.where(low, jnp.zeros_like(k_pair), k_pair))
        for g in range(A_GROUP):
            col = (kv_pair * A_GROUP + g) * PAIR
            q_pair = q[:, col:col + PAIR]
            outs = []
            for half in range(2):
                head = (2 * kv_pair + half) * A_GROUP + g
                s = _dot_nt(q_pair, k_half[half]) + bias_ref[0, head]
                o, _, l = _softmax_pv(s, v_pair, sink_ref[head])
                outs.append(o * (1.0 / l))
            o_ref[0, :, col:col + PAIR] = jnp.where(low, outs[0], outs[1]).astype(o_ref.dtype)


def _swa_attention(qkv3, bias, sinks):
    b, s, _ = qkv3.shape
    nb = s // BLOCK
    kcol = A_WIDTH // A_KV_WIDTH
    prev = lambda i: jnp.maximum(i - 1, 0)
    return pl.pallas_call(
        _swa_kernel,
        grid=(b, nb),
        in_specs=[
            pl.BlockSpec(memory_space=pltpu.SMEM),
            pl.BlockSpec((1, BLOCK, A_WIDTH), lambda bi, i: (bi, i, 0)),
            pl.BlockSpec((1, BLOCK, A_KV_WIDTH), lambda bi, i: (bi, prev(i), kcol)),
            pl.BlockSpec((1, BLOCK, A_KV_WIDTH), lambda bi, i: (bi, i, kcol)),
            pl.BlockSpec((1, BLOCK, A_KV_WIDTH), lambda bi, i: (bi, prev(i), kcol + 1)),
            pl.BlockSpec((1, BLOCK, A_KV_WIDTH), lambda bi, i: (bi, i, kcol + 1)),
            pl.BlockSpec((1, A_Q_HEADS, BLOCK, 2 * BLOCK),
                         lambda bi, i: (jnp.minimum(i, 1), 0, 0, 0)),
        ],
        out_specs=pl.BlockSpec((1, BLOCK, A_WIDTH), lambda bi, i: (bi, i, 0)),
        out_shape=jax.ShapeDtypeStruct((b, s, A_WIDTH), BF16),
        compiler_params=_cparams(2),
        name="swa_attention",
    )(sinks, qkv3, qkv3, qkv3, qkv3, qkv3, bias)


def _dilated_kernel(q_ref, kp_ref, kc_ref, vp_ref, vc_ref, bias_ref, o_ref, lse_ref):
    q = q_ref[...] * jnp.asarray(HEAD_DIM ** -0.5, BF16)
    k = jnp.concatenate([kp_ref[...], kc_ref[...]], axis=0)
    v = jnp.concatenate([vp_ref[...], vc_ref[...]], axis=0)
    lane = lax.broadcasted_iota(I32, (1, PAIR), 1)
    low = lane < HEAD_DIM
    for pair in range(B_HEADS // 2):
        col = pair * PAIR
        q_pair = q[:, col:col + PAIR]
        k_pair = k[:, col:col + PAIR]
        v_pair = v[:, col:col + PAIR]
        k_half = (jnp.where(low, k_pair, jnp.zeros_like(k_pair)),
                  jnp.where(low, jnp.zeros_like(k_pair), k_pair))
        outs, lses = [], []
        for half in range(2):
            s = _dot_nt(q_pair, k_half[half]) + bias_ref[2 * pair + half]
            o, m, l = _softmax_pv(s, v_pair, None)
            outs.append(o * (1.0 / l))
            lses.append(m + jnp.log(l))
        o_ref[:, col:col + PAIR] = jnp.where(low, outs[0], outs[1])
        lse_ref[:, col:col + PAIR] = jnp.where(low, lses[0], lses[1])


def _dilated_attention(q, k, v, cols, bias):
    b, r, n, _ = q.shape
    nb = n // BLOCK
    qc, kc, vc = cols
    prev = lambda j: jnp.maximum(j - 1, 0)
    blk = lambda col, row: pl.BlockSpec((None, None, BLOCK, B_WIDTH),
                                        lambda bi, c, j: (bi, c, row(j), col))
    out_shape = jax.ShapeDtypeStruct((b, r, n, B_WIDTH), F32)
    out_spec = blk(0, lambda j: j)
    return pl.pallas_call(
        _dilated_kernel,
        grid=(b, r, nb),
        in_specs=[blk(qc, lambda j: j), blk(kc, prev), blk(kc, lambda j: j), blk(vc, prev),
                  blk(vc, lambda j: j),
                  pl.BlockSpec((None, B_HEADS, BLOCK, 2 * BLOCK),
                               lambda bi, c, j: (jnp.minimum(j, 1), 0, 0, 0))],
        out_specs=[out_spec, out_spec],
        out_shape=[out_shape, out_shape],
        compiler_params=_cparams(3),
        name=f"dilated_attention_r{r}",
    )(q, k, k, v, v, bias)


def _mix_out_kernel(x_ref, oa_ref, o1_ref, l1_ref, o4_ref, l4_ref, o16_ref, l16_ref, ga_ref, gb_ref,
                    wa_ref, wb_ref, wo_ref, out_ref, seq_ref):
    tm = x_ref.shape[0]
    seq = []
    for slot, (r, ref) in enumerate(((4, o4_ref), (4, l4_ref), (16, o16_ref), (16, l16_ref))):
        for c in range(r):
            for cc in range(LANE_CHUNKS):
                seq_ref[slot * LANE_CHUNKS + cc, pl.ds(c, tm // r, stride=r), :] = (
                    ref[0, c, :, cc * LANES:(cc + 1) * LANES])
        seq.append(jnp.concatenate([seq_ref[slot * LANE_CHUNKS + cc] for cc in range(LANE_CHUNKS)],
                                   axis=1))
    o2, l2, o3, l3 = seq
    l1 = l1_ref[...]
    m = jnp.maximum(jnp.maximum(l1, l2), l3)
    e1, e2, e3 = jnp.exp(l1 - m), jnp.exp(l2 - m), jnp.exp(l3 - m)
    inv = 1.0 / (e1 + e2 + e3)
    ob = (e1 * inv) * o1_ref[...] + (e2 * inv) * o2 + (e3 * inv) * o3
    ya = jnp.dot(oa_ref[...], wa_ref[...], preferred_element_type=F32)
    yb = jnp.dot(ob.astype(BF16), wb_ref[...], preferred_element_type=F32)
    mixed = ga_ref[...].astype(F32) * ya + gb_ref[...].astype(F32) * yb
    out_ref[...] = x_ref[...] + jnp.dot(mixed.astype(BF16), wo_ref[...], preferred_element_type=F32)


def _mix_out(x2d, oa, o1, l1, o4, l4, o16, l16, gates, wa, wb, wo, s, *, tm):
    t, d = x2d.shape
    tiles = s // tm
    row = lambda w: pl.BlockSpec((tm, w), lambda i: (i, 0))
    res = lambda r: pl.BlockSpec((1, r, tm // r, B_WIDTH), lambda i: (i // tiles, 0, i % tiles, 0))
    const = lambda shape: pl.BlockSpec(shape, lambda i: (0, 0))
    return pl.pallas_call(
        _mix_out_kernel,
        grid=(t // tm,),
        in_specs=[row(d), row(A_WIDTH), row(B_WIDTH), row(B_WIDTH), res(4), res(4), res(16), res(16),
                  pl.BlockSpec((tm, d), lambda i: (i, 0)), pl.BlockSpec((tm, d), lambda i: (i, 1)),
                  const(wa.shape), const(wb.shape), const(wo.shape)],
        out_specs=row(d),
        out_shape=jax.ShapeDtypeStruct((t, d), F32),
        scratch_shapes=[pltpu.VMEM((4 * LANE_CHUNKS, tm, LANES), F32)],
        compiler_params=_cparams(1),
        name="mix_out",
    )(x2d, oa, o1, l1, o4, l4, o16, l16, gates, gates, wa, wb, wo)


def _mem_kv_kernel(mem_ref, g_ref, wk_ref, wv_ref, k_ref, v_ref):
    hn = _rms(mem_ref[0], g_ref[...]).astype(BF16)
    k_ref[0] = jnp.dot(hn, wk_ref[...], preferred_element_type=F32).astype(BF16)
    v_ref[0] = jnp.dot(hn, wv_ref[...], preferred_element_type=F32).astype(BF16)


def _mem_kv(mem, g, wk, wv):
    b, ml, d = mem.shape
    const = lambda shape: pl.BlockSpec(shape, lambda bi: (0,) * len(shape))
    out = jax.ShapeDtypeStruct((b, ml, X_WIDTH), BF16)
    blk = pl.BlockSpec((1, ml, X_WIDTH), lambda bi: (bi, 0, 0))
    return pl.pallas_call(
        _mem_kv_kernel,
        grid=(b,),
        in_specs=[pl.BlockSpec((1, ml, d), lambda bi: (bi, 0, 0)), const((1, d)),
                  const(wk.shape), const(wv.shape)],
        out_specs=[blk, blk],
        out_shape=[out, out],
        compiler_params=_cparams(1),
        name="mem_kv",
    )(mem, g.reshape(1, d), wk, wv)


def _cross_kernel(x_ref, g_ref, wq_ref, k_ref, v_ref, wo_ref, out_ref):
    x = x_ref[0]
    h = _rms(x, g_ref[...]).astype(BF16)
    q = (jnp.dot(h, wq_ref[...], preferred_element_type=F32) * (X_HEAD_DIM ** -0.5)).astype(BF16)
    k, v = k_ref[0], v_ref[0]
    outs = []
    for hd in range(X_HEADS):
        sl = slice(hd * X_HEAD_DIM, (hd + 1) * X_HEAD_DIM)
        s = _dot_nt(q[:, sl], k[:, sl])
        m = jnp.max(s, axis=-1, keepdims=True)
        p = jnp.exp(s - m)
        l = jnp.sum(p, axis=-1, keepdims=True)
        o = jnp.dot(p.astype(BF16), v[:, sl], preferred_element_type=F32)
        outs.append((o * (1.0 / l)).astype(BF16))
    o = jnp.concatenate(outs, axis=1)
    out_ref[0] = x + jnp.dot(o, wo_ref[...], preferred_element_type=F32)


def _cross_attention(x3, g, wq, k, v, wo, *, tm):
    b, s, d = x3.shape
    ml = k.shape[1]
    const = lambda shape: pl.BlockSpec(shape, lambda bi, i: (0,) * len(shape))
    return pl.pallas_call(
        _cross_kernel,
        grid=(b, s // tm),
        in_specs=[pl.BlockSpec((1, tm, d), lambda bi, i: (bi, i, 0)), const((1, d)), const(wq.shape),
                  pl.BlockSpec((1, ml, X_WIDTH), lambda bi, i: (bi, 0, 0)),
                  pl.BlockSpec((1, ml, X_WIDTH), lambda bi, i: (bi, 0, 0)),
                  const(wo.shape)],
        out_specs=pl.BlockSpec((1, tm, d), lambda bi, i: (bi, i, 0)),
        out_shape=jax.ShapeDtypeStruct((b, s, d), F32),
        compiler_params=_cparams(2),
        name="cross_attention",
    )(x3, g.reshape(1, d), wq, k, v, wo)


def _first_argmax(vals, rows, n):
    m = jnp.max(vals, axis=0, keepdims=True)
    idx = jnp.min(jnp.where(vals == m, rows, n), axis=0, keepdims=True)
    return m, idx


def _router_kernel(x_ref, g_ref, wr_ref, hp_ref, e_ref, gate_ref, cnt_ref):
    h = _rms(x_ref[...], g_ref[...])
    tm, d = h.shape
    logits = lax.dot_general(wr_ref[...], h, (((1,), (1,)), ((), ())),
                             precision=lax.Precision.HIGHEST, preferred_element_type=F32)
    rows8 = lax.broadcasted_iota(I32, (EXP_PER_GROUP, tm), 0)
    gl = jnp.where(rows8 < N_GROUPS, logits[N_EXPERTS:N_EXPERTS + 8], -jnp.inf)
    gmax, gidx = _first_argmax(gl, rows8, 8)
    g_gate = 1.0 / jnp.sum(jnp.exp(gl - gmax), axis=0, keepdims=True)
    sel = jnp.zeros((EXP_PER_GROUP, tm), F32)
    for grp in range(N_GROUPS):
        sel = jnp.where(gidx == grp, logits[grp * EXP_PER_GROUP:(grp + 1) * EXP_PER_GROUP], sel)
    v1, i1 = _first_argmax(sel, rows8, 8)
    sel2 = jnp.where(rows8 == i1, -jnp.inf, sel)
    v2, i2 = _first_argmax(sel2, rows8, 8)
    e2x = jnp.exp(v2 - v1)
    den = 1.0 + e2x
    e_ref[...] = jnp.concatenate([gidx * EXP_PER_GROUP + i1, gidx * EXP_PER_GROUP + i2], axis=0)
    gate_ref[...] = jnp.concatenate([(1.0 / den) * g_gate, (e2x / den) * g_gate], axis=0)

    bits = lax.bitcast_convert_type(h.astype(BF16).astype(F32), U32)
    packed = (bits[:, :d // 2] >> 16) | (bits[:, d // 2:] & jnp.uint32(0xFFFF0000))
    for c in range(PACKED_TILE_ROWS):
        hp_ref[pl.ds(c, tm, stride=PACKED_TILE_ROWS), :] = packed[:, c * LANES:(c + 1) * LANES]

    @pl.when(pl.program_id(0) == 0)
    def _():
        cnt_ref[...] = jnp.zeros_like(cnt_ref)

    rows32 = lax.broadcasted_iota(I32, (N_EXPERTS, tm), 0)
    e = e_ref[...]
    hits = (rows32 == e[0:1]).astype(F32) + (rows32 == e[1:2]).astype(F32)
    cnt_ref[...] += jnp.sum(hits, axis=1, keepdims=True)


def _router(x2d, g, wr_t, *, tm):
    t, d = x2d.shape
    return pl.pallas_call(
        _router_kernel,
        grid=(t // tm,),
        in_specs=[pl.BlockSpec((tm, d), lambda i: (i, 0)),
                  pl.BlockSpec((1, d), lambda i: (0, 0)),
                  pl.BlockSpec((ROUTER_ROWS, d), lambda i: (0, 0))],
        out_specs=[pl.BlockSpec((tm * PACKED_TILE_ROWS, LANES), lambda i: (i, 0)),
                   pl.BlockSpec((TOP_K, tm), lambda i: (0, i)),
                   pl.BlockSpec((TOP_K, tm), lambda i: (0, i)),
                   pl.BlockSpec((N_EXPERTS, 128), lambda i: (0, 0))],
        out_shape=[jax.ShapeDtypeStruct((t * PACKED_TILE_ROWS, LANES), U32),
                   jax.ShapeDtypeStruct((TOP_K, t), I32),
                   jax.ShapeDtypeStruct((TOP_K, t), F32),
                   jax.ShapeDtypeStruct((N_EXPERTS, 128), F32)],
        compiler_params=_cparams(1),
        name="moe_router",
    )(x2d, g.reshape(1, d), wr_t)


def _dest_kernel(e_ref, start_ref, d_ref, carry_ref):
    @pl.when(pl.program_id(0) == 0)
    def _():
        carry_ref[...] = jnp.zeros_like(carry_ref)

    e = e_ref[...]
    tm = e.shape[1]
    rows32 = lax.broadcasted_iota(I32, (N_EXPERTS, tm), 0)
    oh0 = (rows32 == e[0:1]).astype(F32)
    oh1 = (rows32 == e[1:2]).astype(F32)
    hits = oh0 + oh1
    earlier = (lax.broadcasted_iota(I32, (tm, tm), 0) < lax.broadcasted_iota(I32, (tm, tm), 1))
    prefix = jnp.dot(hits.astype(BF16), earlier.astype(BF16), preferred_element_type=F32)
    base = prefix + carry_ref[:, 0:1] + start_ref[:, 0:1]
    d0 = jnp.sum(oh0 * base, axis=0, keepdims=True)
    d1 = jnp.sum(oh1 * base, axis=0, keepdims=True)
    d_ref[...] = jnp.concatenate([d0, d1], axis=0).astype(I32)
    carry_ref[...] += jnp.sum(hits, axis=1, keepdims=True)


def _assignment_rows(e_t, start, *, tm):
    t = e_t.shape[1]
    return pl.pallas_call(
        _dest_kernel,
        grid=(t // tm,),
        in_specs=[pl.BlockSpec((TOP_K, tm), lambda i: (0, i)),
                  pl.BlockSpec((N_EXPERTS, 128), lambda i: (0, 0))],
        out_specs=pl.BlockSpec((TOP_K, tm), lambda i: (0, i)),
        out_shape=jax.ShapeDtypeStruct((TOP_K, t), I32),
        scratch_shapes=[pltpu.VMEM((N_EXPERTS, 128), F32)],
        compiler_params=_cparams(1),
        name="moe_assignment_rows",
    )(e_t, start)


def _token_rows(ref, token, rows):
    return ref.at[pl.ds(pl.multiple_of(token * rows, rows), rows)]


def _dispatch_kernel(d_ref, hp_ref, xg_in_ref, xg_ref, sem):
    del xg_in_ref
    rows = PACKED_TILE_ROWS
    tm = hp_ref.shape[0] // rows
    for t in range(tm):
        for k in range(TOP_K):
            pltpu.make_async_copy(_token_rows(hp_ref, t, rows), _token_rows(xg_ref, d_ref[k, t], rows),
                                  sem).start()
    for k in range(TOP_K):
        pltpu.make_async_copy(hp_ref, xg_ref.at[pl.ds(0, tm * rows)], sem).wait()


def _dispatch(dest, hp, n_rows, *, tm):
    rows = PACKED_TILE_ROWS
    t = hp.shape[0] // rows
    return pl.pallas_call(
        _dispatch_kernel,
        grid=(t // tm,),
        in_specs=[pl.BlockSpec((TOP_K, tm), lambda i: (0, i), memory_space=pltpu.SMEM),
                  pl.BlockSpec((tm * rows, LANES), lambda i: (i, 0)),
                  pl.BlockSpec(memory_space=pl.ANY)],
        out_specs=pl.BlockSpec(memory_space=pl.ANY),
        out_shape=jax.ShapeDtypeStruct((n_rows * rows, LANES), U32),
        scratch_shapes=[pltpu.SemaphoreType.DMA(())],
        input_output_aliases={2: 0},
        compiler_params=_cparams(1),
        name="moe_dispatch",
    )(dest, hp, jnp.zeros((n_rows * rows, LANES), U32))


def _expert_kernel(be_ref, nu_ref, xg_ref, w1_ref, w3_ref, w2_ref, y_ref, w1b_ref, w3b_ref, w2b_ref):
    i = pl.program_id(0)

    @pl.when(i < nu_ref[0])
    def _():
        @pl.when((i == 0) | (be_ref[i] != be_ref[jnp.maximum(i - 1, 0)]))
        def _():
            w1b_ref[...] = w1_ref[0].astype(BF16)
            w3b_ref[...] = w3_ref[0].astype(BF16)
            w2b_ref[...] = w2_ref[0].astype(BF16)

        pk = [xg_ref[pl.ds(c, MOE_ROWS, stride=PACKED_TILE_ROWS), :] for c in range(PACKED_TILE_ROWS)]
        lo = [lax.bitcast_convert_type(p << 16, F32) for p in pk]
        hi = [lax.bitcast_convert_type(p & jnp.uint32(0xFFFF0000), F32) for p in pk]
        x = jnp.concatenate(lo + hi, axis=1).astype(BF16)
        h1 = jnp.dot(x, w1b_ref[...], preferred_element_type=F32)
        h3 = jnp.dot(x, w3b_ref[...], preferred_element_type=F32)
        a = (jax.nn.silu(h1) * h3).astype(BF16)
        y = jnp.dot(a, w2b_ref[...], preferred_element_type=F32)
        for c in range(F32_TILE_ROWS):
            y_ref[pl.ds(c, MOE_ROWS, stride=F32_TILE_ROWS), :] = y[:, c * LANES:(c + 1) * LANES]

    @pl.when(i >= nu_ref[0])
    def _():
        y_ref[...] = jnp.zeros_like(y_ref)


def _experts(block_e, n_used, xg, w1, w3, w2):
    n_rows = xg.shape[0] // PACKED_TILE_ROWS
    _, d, dff = w1.shape
    nblk = n_rows // MOE_ROWS
    live = lambda i, nu: jnp.minimum(i, nu[0] - 1)
    up = pl.BlockSpec((1, d, dff), lambda i, be, nu: (be[live(i, nu)], 0, 0))
    return pl.pallas_call(
        _expert_kernel,
        grid_spec=pltpu.PrefetchScalarGridSpec(
            num_scalar_prefetch=2,
            grid=(nblk,),
            in_specs=[pl.BlockSpec((MOE_ROWS * PACKED_TILE_ROWS, LANES),
                                   lambda i, be, nu: (live(i, nu), 0)), up, up,
                      pl.BlockSpec((1, dff, d), lambda i, be, nu: (be[live(i, nu)], 0, 0))],
            out_specs=pl.BlockSpec((MOE_ROWS * F32_TILE_ROWS, LANES), lambda i, be, nu: (i, 0)),
            scratch_shapes=[pltpu.VMEM((d, dff), BF16), pltpu.VMEM((d, dff), BF16),
                            pltpu.VMEM((dff, d), BF16)],
        ),
        out_shape=jax.ShapeDtypeStruct((n_rows * F32_TILE_ROWS, LANES), F32),
        compiler_params=_cparams(1),
        name="moe_experts",
    )(block_e, n_used, xg, w1, w3, w2)


def _combine_kernel(dc_ref, dn_ref, x_ref, gate_ref, gf_ref, y_ref, out_ref, buf, sem, *, final):
    i = pl.program_id(0)
    n = pl.num_programs(0)
    tm = x_ref.shape[0]
    rows = F32_TILE_ROWS

    def issue(d_ref, slot):
        for t in range(tm):
            for k in range(TOP_K):
                pltpu.make_async_copy(_token_rows(y_ref, d_ref[k, t], rows),
                                      _token_rows(buf.at[slot, k], t, rows), sem.at[slot]).start()

    @pl.when(i == 0)
    def _():
        issue(dc_ref, 0)

    @pl.when(i + 1 < n)
    def _():
        issue(dn_ref, (i + 1) % 2)

    slot = i % 2
    for k in range(TOP_K):
        pltpu.make_async_copy(y_ref.at[pl.ds(0, tm * rows)], buf.at[slot, k], sem.at[slot]).wait()
    g = gate_ref[...]
    g0, g1 = g[:, 0:1], g[:, 1:2]
    chunks = []
    for c in range(rows):
        e0 = buf[slot, 0, pl.ds(c, tm, stride=rows), :]
        e1 = buf[slot, 1, pl.ds(c, tm, stride=rows), :]
        chunks.append(x_ref[:, c * LANES:(c + 1) * LANES] + (g0 * e0 + g1 * e1))
    y = jnp.concatenate(chunks, axis=1)
    out_ref[...] = _rms(y, gf_ref[...]) if final else y


def _combine(dest, x2d, gates_tok, g_final, y, *, final, tm):
    t, d = x2d.shape
    nt = t // tm
    return pl.pallas_call(
        functools.partial(_combine_kernel, final=final),
        grid=(nt,),
        in_specs=[pl.BlockSpec((TOP_K, tm), lambda i: (0, i), memory_space=pltpu.SMEM),
                  pl.BlockSpec((TOP_K, tm), lambda i: (0, jnp.minimum(i + 1, nt - 1)),
                               memory_space=pltpu.SMEM),
                  pl.BlockSpec((tm, d), lambda i: (i, 0)),
                  pl.BlockSpec((tm, TOP_K), lambda i: (i, 0)),
                  pl.BlockSpec((1, d), lambda i: (0, 0)),
                  pl.BlockSpec(memory_space=pl.ANY)],
        out_specs=pl.BlockSpec((tm, d), lambda i: (i, 0)),
        out_shape=jax.ShapeDtypeStruct((t, d), F32),
        scratch_shapes=[pltpu.VMEM((2, TOP_K, tm * F32_TILE_ROWS, LANES), F32),
                        pltpu.SemaphoreType.DMA((2,))],
        compiler_params=_cparams(1),
        name="moe_combine",
    )(dest, dest, x2d, gates_tok, g_final.reshape(1, d), y)


def _swa_q_order():
    heads = []
    for kv_pair in range(A_KV_HEADS // 2):
        for g in range(A_GROUP):
            heads += [(2 * kv_pair) * A_GROUP + g, (2 * kv_pair + 1) * A_GROUP + g]
    return jnp.asarray([h * HEAD_DIM + c for h in heads for c in range(HEAD_DIM)], I32)


def kernel(x, mem, rel_bias, g_mix, w_in, sinks_a, w_a_out, w_b_out, w_gate, b_gate, w_o,
           g_x, g_mem, w_xq, w_xk, w_xv, w_xo, g_moe, w_rg, w_re, w1, w3, w2, g_final):
    b, s, d = x.shape
    t = b * s
    depth = g_mix.shape[0]
    perm = _swa_q_order()
    bias_a = _band_bias(rel_bias[:, :A_Q_HEADS], 1, A_WINDOW - 1)
    bias_b = [_band_bias(rel_bias[:, A_Q_HEADS:], r, w // r) for w, r in B_PATTERNS]
    n_assign = t * TOP_K
    n_rows = -(-(n_assign + N_EXPERTS * (MOE_ROWS - 1)) // MOE_ROWS) * MOE_ROWS
    n_blocks = n_rows // MOE_ROWS

    x2d = x.reshape(t, d)
    for l in range(depth):
        w_in_l = jnp.concatenate([w_in[l][:, :A_WIDTH][:, perm], w_in[l][:, A_WIDTH:]], axis=1).astype(BF16)
        qkv_a, qkv_b, q4, k4, v4, q16, k16, v16 = _in_proj(x2d, g_mix[l], w_in_l, b, s, tm=512)
        gates = _gate_proj(x2d, g_mix[l], w_gate[l].astype(BF16), b_gate[l], tm=1024, tn=1024)
        oa = _swa_attention(qkv_a.reshape(b, s, 2 * B_WIDTH), bias_a, sinks_a[l]).reshape(t, A_WIDTH)
        qkv_b4 = qkv_b.reshape(b, 1, s, 3 * B_WIDTH)
        o1, l1 = _dilated_attention(qkv_b4, qkv_b4, qkv_b4, (0, 1, 2), bias_b[0])
        o4, l4 = _dilated_attention(q4, k4, v4, (0, 0, 0), bias_b[1])
        o16, l16 = _dilated_attention(q16, k16, v16, (0, 0, 0), bias_b[2])
        x2d = _mix_out(x2d, oa, o1.reshape(t, B_WIDTH), l1.reshape(t, B_WIDTH), o4, l4, o16, l16, gates,
                       w_a_out[l][perm].astype(BF16), w_b_out[l].astype(BF16), w_o[l].astype(BF16),
                       s, tm=256)
        k_mem, v_mem = _mem_kv(mem, g_mem[l], w_xk[l].astype(BF16), w_xv[l].astype(BF16))
        x2d = _cross_attention(x2d.reshape(b, s, d), g_x[l], w_xq[l].astype(BF16), k_mem, v_mem,
                               w_xo[l].astype(BF16), tm=512).reshape(t, d)
        wr_t = jnp.concatenate([w_re[l].T, w_rg[l].T,
                                jnp.zeros((ROUTER_ROWS - N_EXPERTS - N_GROUPS, d), F32)], axis=0)
        hp, e_t, gate_t, cnt = _router(x2d, g_moe[l], wr_t, tm=512)
        counts = cnt[:, 0].astype(I32)
        padded = (counts + MOE_ROWS - 1) // MOE_ROWS * MOE_ROWS
        seg_end = jnp.cumsum(padded)
        seg_start = seg_end - padded
        n_used = (seg_end[-1] // MOE_ROWS).astype(I32).reshape(1)
        block_row = jnp.arange(n_blocks, dtype=I32) * MOE_ROWS
        block_e = jnp.minimum(jnp.sum(seg_end[None, :] <= block_row[:, None], axis=1),
                              N_EXPERTS - 1).astype(I32)
        start = jnp.broadcast_to(seg_start.astype(F32)[:, None], (N_EXPERTS, 128))
        dest = _assignment_rows(e_t, start, tm=512)
        xg = _dispatch(dest, hp, n_rows, tm=128)
        y = _experts(block_e, n_used, xg, w1[l], w3[l], w2[l])
        x2d = _combine(dest, x2d, gate_t.T, g_final, y, final=(l + 1 == depth), tm=128)
    return x2d.reshape(b, s, d)
```

```python
import functools
import math

import jax
import jax.numpy as jnp
from jax import lax
from jax.experimental import pallas as pl
from jax.experimental.pallas import tpu as pltpu

F32 = jnp.float32
BF16 = jnp.bfloat16
I32 = jnp.int32
U32 = jnp.uint32

HEAD_DIM = 64
PAIR = 2 * HEAD_DIM
A_Q_HEADS = 16
A_KV_HEADS = 4
A_GROUP = A_Q_HEADS // A_KV_HEADS
A_WIDTH = A_Q_HEADS * HEAD_DIM
A_KV_WIDTH = A_KV_HEADS * HEAD_DIM
A_WINDOW = 128
B_HEADS = 12
B_WIDTH = B_HEADS * HEAD_DIM
LANES = 128
LANE_CHUNKS = B_WIDTH // LANES
D_MODEL = 2048
PACKED_TILE_ROWS = D_MODEL // 2 // LANES
B_PATTERNS = ((128, 1), (512, 4), (2048, 16))
BLOCK = 128
Q_BLOCKS = 2
IN_COLS = A_WIDTH + 2 * A_KV_WIDTH + 3 * B_WIDTH
REL_BUCKETS = 32
REL_MAX_DIST = 2048
X_HEADS = 4
X_HEAD_DIM = 128
X_WIDTH = X_HEADS * X_HEAD_DIM
N_GROUPS = 4
EXP_PER_GROUP = 8
N_EXPERTS = N_GROUPS * EXP_PER_GROUP
TOP_K = 2
D_FF = 512
EPS = 1e-6
NEG = -1e30
ROUTER_ROWS = 40
MOE_ROWS = 256
VMEM_LIMIT = 56 * 1024 * 1024


def _cparams(n_axes):
    return pltpu.CompilerParams(dimension_semantics=("arbitrary",) * n_axes,
                                vmem_limit_bytes=VMEM_LIMIT)


def _rms(xf, g):
    return xf * lax.rsqrt(jnp.mean(xf * xf, axis=-1, keepdims=True) + EPS) * g


def _dot_nt(a, b):
    return lax.dot_general(a, b, (((1,), (1,)), ((), ())), preferred_element_type=F32)


def _in_proj_kernel(x_ref, g_ref, w_ref, oa_ref, ob_ref, q4_ref, k4_ref, v4_ref, q16_ref, k16_ref,
                    v16_ref, h_ref, acc_ref, mod4_ref):
    j = pl.program_id(1)
    tm = x_ref.shape[0]

    @pl.when(j == 0)
    def _():
        h_ref[...] = _rms(x_ref[...], g_ref[...]).astype(BF16)

    acc = jnp.dot(h_ref[...], w_ref[...], preferred_element_type=F32)

    @pl.when(j < 2)
    def _():
        oa_ref[...] = acc.astype(BF16)

    @pl.when(j >= 2)
    def _():
        ob_ref[...] = acc.astype(BF16)
        for cc in range(LANE_CHUNKS):
            acc_ref[cc] = acc[:, cc * LANES:(cc + 1) * LANES]

    for part, (r4_ref, r16_ref) in enumerate(((q4_ref, q16_ref), (k4_ref, k16_ref), (v4_ref, v16_ref))):
        @pl.when(j == 2 + part)
        def _():
            n4 = tm // 4
            for c in range(4):
                rows = [acc_ref[cc, pl.ds(c, n4, stride=4), :] for cc in range(LANE_CHUNKS)]
                r4_ref[0, c] = jnp.concatenate(rows, axis=1).astype(BF16)
                for cc in range(LANE_CHUNKS):
                    mod4_ref[cc, c * n4:(c + 1) * n4, :] = rows[cc]
            for c in range(16):
                rows = [mod4_ref[cc, pl.ds((c % 4) * n4 + c // 4, tm // 16, stride=4), :]
                        for cc in range(LANE_CHUNKS)]
                r16_ref[0, c] = jnp.concatenate(rows, axis=1).astype(BF16)


def _in_proj(x2d, g, w, b, s, *, tm):
    t, d = x2d.shape
    tn = B_WIDTH
    tiles = s // tm
    res_shape = lambda r: jax.ShapeDtypeStruct((b, r, s // r, B_WIDTH), BF16)
    res_spec = lambda r: pl.BlockSpec((1, r, tm // r, B_WIDTH),
                                      lambda i, j: (i // tiles, 0, i % tiles, 0))
    return pl.pallas_call(
        _in_proj_kernel,
        grid=(t // tm, IN_COLS // tn),
        in_specs=[
            pl.BlockSpec((tm, d), lambda i, j: (i, 0)),
            pl.BlockSpec((1, d), lambda i, j: (0, 0)),
            pl.BlockSpec((d, tn), lambda i, j: (0, j)),
        ],
        out_specs=[pl.BlockSpec((tm, tn), lambda i, j: (i, jnp.minimum(j, 1))),
                   pl.BlockSpec((tm, tn), lambda i, j: (i, jnp.maximum(j - 2, 0)))]
                  + [res_spec(4)] * 3 + [res_spec(16)] * 3,
        out_shape=[jax.ShapeDtypeStruct((t, 2 * tn), BF16), jax.ShapeDtypeStruct((t, 3 * tn), BF16)]
                  + [res_shape(4)] * 3 + [res_shape(16)] * 3,
        scratch_shapes=[pltpu.VMEM((tm, d), BF16), pltpu.VMEM((LANE_CHUNKS, tm, LANES), F32),
                        pltpu.VMEM((LANE_CHUNKS, tm, LANES), F32)],
        compiler_params=_cparams(2),
        name="in_proj",
    )(x2d, g.reshape(1, d), w)


def _gate_proj_kernel(x_ref, g_ref, w_ref, b_ref, o_ref, h_ref):
    @pl.when(pl.program_id(1) == 0)
    def _():
        h_ref[...] = _rms(x_ref[...], g_ref[...]).astype(BF16)

    acc = jnp.dot(h_ref[...], w_ref[...], preferred_element_type=F32)
    o_ref[...] = jax.nn.sigmoid(acc + b_ref[...]).astype(o_ref.dtype)


def _gate_proj(x2d, g, w, b, *, tm, tn):
    t, d = x2d.shape
    n = w.shape[1]
    return pl.pallas_call(
        _gate_proj_kernel,
        grid=(t // tm, n // tn),
        in_specs=[
            pl.BlockSpec((tm, d), lambda i, j: (i, 0)),
            pl.BlockSpec((1, d), lambda i, j: (0, 0)),
            pl.BlockSpec((d, tn), lambda i, j: (0, j)),
            pl.BlockSpec((1, tn), lambda i, j: (0, j)),
        ],
        out_specs=pl.BlockSpec((tm, tn), lambda i, j: (i, j)),
        out_shape=jax.ShapeDtypeStruct((t, n), BF16),
        scratch_shapes=[pltpu.VMEM((tm, d), BF16)],
        compiler_params=_cparams(2),
        name="gate_proj",
    )(x2d, g.reshape(1, d), w, b.reshape(1, n))


def t5_bucket(dist):
    max_exact = REL_BUCKETS // 2
    d = jnp.maximum(dist, 0)
    df = jnp.maximum(d, 1).astype(jnp.float32)
    large = max_exact + (jnp.log(df / max_exact) / math.log(REL_MAX_DIST / max_exact)
                         * (REL_BUCKETS - max_exact)).astype(jnp.int32)
    large = jnp.minimum(large, REL_BUCKETS - 1)
    return jnp.where(d < max_exact, d, large)


def _band_bias(table, step, max_off):
    heads = table.shape[1]
    span = 3 * BLOCK
    dist = 2 * BLOCK - 1 - jnp.arange(span)
    z = jnp.where(((dist >= 0) & (dist <= max_off))[:, None], table[t5_bucket(dist * step)], NEG)
    z = z.T.astype(F32)
    rows = jnp.tile(z, (1, BLOCK))[:, :BLOCK * (span - 1)].reshape(heads, BLOCK, span - 1)
    rest = rows[:, :, BLOCK - 1:3 * BLOCK - 1]
    first = jnp.where(jnp.arange(2 * BLOCK)[None, None, :] >= BLOCK, rest, NEG)
    return jnp.stack([first, rest])


def _softmax_pv(s, v_pair, sink):
    m = jnp.max(s, axis=-1, keepdims=True)
    if sink is not None:
        m = jnp.maximum(m, sink)
    p = jnp.exp(s - m)
    l = jnp.sum(p, axis=-1, keepdims=True)
    if sink is not None:
        l = l + jnp.exp(sink - m)
    o = jnp.dot(p.astype(BF16), v_pair, preferred_element_type=F32)
    return o, m, l


def _swa_kernel(sink_ref, q_ref, kp_ref, kc_ref, vp_ref, vc_ref, bias_ref, o_ref):
    first = jnp.minimum(pl.program_id(1), 1)
    q_all = q_ref[0] * jnp.asarray(HEAD_DIM ** -0.5, BF16)
    k_all = jnp.concatenate([kp_ref[0], kc_ref[0]], axis=0)
    v_all = jnp.concatenate([vp_ref[0], vc_ref[0]], axis=0)
    lane = lax.broadcasted_iota(I32, (1, PAIR), 1)
    low = lane < HEAD_DIM
    for blk in range(Q_BLOCKS):
        rows = slice(blk * BLOCK, (blk + 1) * BLOCK)
        q = q_all[rows]
        k = k_all[blk * BLOCK:(blk + 2) * BLOCK]
        v = v_all[blk * BLOCK:(blk + 2) * BLOCK]
        variant = first if blk == 0 else 1
        for kv_pair in range(A_KV_HEADS // 2):
            k_pair = k[:, kv_pair * PAIR:(kv_pair + 1) * PAIR]
            v_pair = v[:, kv_pair * PAIR:(kv_pair + 1) * PAIR]
            k_half = (jnp.where(low, k_pair, jnp.zeros_like(k_pair)),
                      jnp.where(low, jnp.zeros_like(k_pair), k_pair))
            for g in range(A_GROUP):
                col = (kv_pair * A_GROUP + g) * PAIR
                q_pair = q[:, col:col + PAIR]
                outs = []
                for half in range(2):
                    head = (2 * kv_pair + half) * A_GROUP + g
                    s = _dot_nt(q_pair, k_half[half]) + bias_ref[variant, head]
                    o, _, l = _softmax_pv(s, v_pair, sink_ref[head])
                    outs.append(o * (1.0 / l))
                o_ref[0, rows, col:col + PAIR] = jnp.where(low, outs[0], outs[1]).astype(o_ref.dtype)


def _swa_attention(qkv3, bias, sinks):
    b, s, _ = qkv3.shape
    step = Q_BLOCKS * BLOCK
    kcol = A_WIDTH // A_KV_WIDTH
    prev = lambda i: jnp.maximum(i * Q_BLOCKS - 1, 0)
    return pl.pallas_call(
        _swa_kernel,
        grid=(b, s // step),
        in_specs=[
            pl.BlockSpec(memory_space=pltpu.SMEM),
            pl.BlockSpec((1, step, A_WIDTH), lambda bi, i: (bi, i, 0)),
            pl.BlockSpec((1, BLOCK, A_KV_WIDTH), lambda bi, i: (bi, prev(i), kcol)),
            pl.BlockSpec((1, step, A_KV_WIDTH), lambda bi, i: (bi, i, kcol)),
            pl.BlockSpec((1, BLOCK, A_KV_WIDTH), lambda bi, i: (bi, prev(i), kcol + 1)),
            pl.BlockSpec((1, step, A_KV_WIDTH), lambda bi, i: (bi, i, kcol + 1)),
            pl.BlockSpec((2, A_Q_HEADS, BLOCK, 2 * BLOCK), lambda bi, i: (0, 0, 0, 0)),
        ],
        out_specs=pl.BlockSpec((1, step, A_WIDTH), lambda bi, i: (bi, i, 0)),
        out_shape=jax.ShapeDtypeStruct((b, s, A_WIDTH), BF16),
        compiler_params=_cparams(2),
        name="swa_attention",
    )(sinks, qkv3, qkv3, qkv3, qkv3, qkv3, bias)


def _dilated_kernel(q_ref, kp_ref, kc_ref, vp_ref, vc_ref, bias_ref, o_ref, lse_ref):
    first = jnp.minimum(pl.program_id(2), 1)
    q_all = q_ref[...] * jnp.asarray(HEAD_DIM ** -0.5, BF16)
    k_all = jnp.concatenate([kp_ref[...], kc_ref[...]], axis=0)
    v_all = jnp.concatenate([vp_ref[...], vc_ref[...]], axis=0)
    lane = lax.broadcasted_iota(I32, (1, PAIR), 1)
    low = lane < HEAD_DIM
    for blk in range(Q_BLOCKS):
        rows = slice(blk * BLOCK, (blk + 1) * BLOCK)
        variant = first if blk == 0 else 1
        lse_tile = jnp.zeros((BLOCK, LANES), F32)
        for pair in range(B_HEADS // 2):
            col = pair * PAIR
            q_pair = q_all[rows, col:col + PAIR]
            k_pair = k_all[blk * BLOCK:(blk + 2) * BLOCK, col:col + PAIR]
            v_pair = v_all[blk * BLOCK:(blk + 2) * BLOCK, col:col + PAIR]
            k_half = (jnp.where(low, k_pair, jnp.zeros_like(k_pair)),
                      jnp.where(low, jnp.zeros_like(k_pair), k_pair))
            outs = []
            for half in range(2):
                head = 2 * pair + half
                s = _dot_nt(q_pair, k_half[half]) + bias_ref[variant, head]
                o, m, l = _softmax_pv(s, v_pair, None)
                outs.append(o * (1.0 / l))
                lse_tile = jnp.where(lane == head, m + jnp.log(l), lse_tile)
            o_ref[rows, col:col + PAIR] = jnp.where(low, outs[0], outs[1]).astype(o_ref.dtype)
        lse_ref[rows, :] = lse_tile


def _dilated_attention(q, k, v, cols, bias):
    b, r, n, _ = q.shape
    step = Q_BLOCKS * BLOCK
    qc, kc, vc = cols
    prev = lambda j: jnp.maximum(j * Q_BLOCKS - 1, 0)
    cur = lambda col, w=B_WIDTH: pl.BlockSpec((None, None, step, w), lambda bi, c, j: (bi, c, j, col))
    old = lambda col: pl.BlockSpec((None, None, BLOCK, B_WIDTH), lambda bi, c, j: (bi, c, prev(j), col))
    return pl.pallas_call(
        _dilated_kernel,
        grid=(b, r, n // step),
        in_specs=[cur(qc), old(kc), cur(kc), old(vc), cur(vc),
                  pl.BlockSpec((2, B_HEADS, BLOCK, 2 * BLOCK), lambda bi, c, j: (0, 0, 0, 0))],
        out_specs=[cur(0), cur(0, LANES)],
        out_shape=[jax.ShapeDtypeStruct((b, r, n, B_WIDTH), BF16),
                   jax.ShapeDtypeStruct((b, r, n, LANES), F32)],
        compiler_params=_cparams(3),
        name=f"dilated_attention_r{r}",
    )(q, k, k, v, v, bias)


def _mix_out_kernel(x_ref, oa_ref, o1_ref, l1_ref, o4_ref, l4_ref, o16_ref, l16_ref, ga_ref, gb_ref,
                    wa_ref, wb_ref, wo_ref, ex_ref, out_ref, seq_ref):
    tm = x_ref.shape[0]
    seq = []
    base = 0
    for r, ref in ((4, o4_ref), (4, l4_ref), (16, o16_ref), (16, l16_ref)):
        chunks = ref.shape[-1] // LANES
        for c in range(r):
            rows = ref[0, c].astype(F32)
            for cc in range(chunks):
                seq_ref[base + cc, pl.ds(c, tm // r, stride=r), :] = rows[:, cc * LANES:(cc + 1) * LANES]
        seq.append(jnp.concatenate([seq_ref[base + cc] for cc in range(chunks)], axis=1))
        base += chunks
    o2, l2, o3, l3 = seq
    l1 = l1_ref[...]
    m = jnp.maximum(jnp.maximum(l1, l2), l3)
    e1, e2, e3 = jnp.exp(l1 - m), jnp.exp(l2 - m), jnp.exp(l3 - m)
    inv = 1.0 / (e1 + e2 + e3)

    def widen(w):
        hi = w.astype(BF16)
        lo = (w - hi.astype(F32)).astype(BF16)
        return (jnp.dot(hi, ex_ref[...], preferred_element_type=F32)
                + jnp.dot(lo, ex_ref[...], preferred_element_type=F32))

    ob = (widen(e1 * inv) * o1_ref[...].astype(F32) + widen(e2 * inv) * o2 + widen(e3 * inv) * o3)
    ya = jnp.dot(oa_ref[...], wa_ref[...], preferred_element_type=F32)
    yb = jnp.dot(ob.astype(BF16), wb_ref[...], preferred_element_type=F32)
    mixed = ga_ref[...].astype(F32) * ya + gb_ref[...].astype(F32) * yb
    out_ref[...] = x_ref[...] + jnp.dot(mixed.astype(BF16), wo_ref[...], preferred_element_type=F32)


def _mix_out(x2d, oa, o1, l1, o4, l4, o16, l16, gates, wa, wb, wo, s, *, tm):
    t, d = x2d.shape
    tiles = s // tm
    row = lambda w: pl.BlockSpec((tm, w), lambda i: (i, 0))
    res = lambda r, w: pl.BlockSpec((1, r, tm // r, w), lambda i: (i // tiles, 0, i % tiles, 0))
    const = lambda shape: pl.BlockSpec(shape, lambda i: (0, 0))
    head_of_col = jnp.arange(B_WIDTH, dtype=I32)[None, :] // HEAD_DIM
    expand = (jnp.arange(LANES, dtype=I32)[:, None] == head_of_col).astype(BF16)
    return pl.pallas_call(
        _mix_out_kernel,
        grid=(t // tm,),
        in_specs=[row(d), row(A_WIDTH), row(B_WIDTH), row(LANES), res(4, B_WIDTH), res(4, LANES),
                  res(16, B_WIDTH), res(16, LANES),
                  pl.BlockSpec((tm, d), lambda i: (i, 0)), pl.BlockSpec((tm, d), lambda i: (i, 1)),
                  const(wa.shape), const(wb.shape), const(wo.shape), const(expand.shape)],
        out_specs=row(d),
        out_shape=jax.ShapeDtypeStruct((t, d), F32),
        scratch_shapes=[pltpu.VMEM((2 * (LANE_CHUNKS + 1), tm, LANES), F32)],
        compiler_params=_cparams(1),
        name="mix_out",
    )(x2d, oa, o1, l1, o4, l4, o16, l16, gates, gates, wa, wb, wo, expand)


def _mem_kv_kernel(mem_ref, g_ref, wk_ref, wv_ref, k_ref, v_ref):
    hn = _rms(mem_ref[0], g_ref[...]).astype(BF16)
    k_ref[0] = jnp.dot(hn, wk_ref[...], preferred_element_type=F32).astype(BF16)
    v_ref[0] = jnp.dot(hn, wv_ref[...], preferred_element_type=F32).astype(BF16)


def _mem_kv(mem, g, wk, wv):
    b, ml, d = mem.shape
    const = lambda shape: pl.BlockSpec(shape, lambda bi: (0,) * len(shape))
    out = jax.ShapeDtypeStruct((b, ml, X_WIDTH), BF16)
    blk = pl.BlockSpec((1, ml, X_WIDTH), lambda bi: (bi, 0, 0))
    return pl.pallas_call(
        _mem_kv_kernel,
        grid=(b,),
        in_specs=[pl.BlockSpec((1, ml, d), lambda bi: (bi, 0, 0)), const((1, d)),
                  const(wk.shape), const(wv.shape)],
        out_specs=[blk, blk],
        out_shape=[out, out],
        compiler_params=_cparams(1),
        name="mem_kv",
    )(mem, g.reshape(1, d), wk, wv)


def _cross_kernel(x_ref, g_ref, wq_ref, k_ref, v_ref, wo_ref, out_ref):
    x = x_ref[0]
    h = _rms(x, g_ref[...]).astype(BF16)
    q = (jnp.dot(h, wq_ref[...], preferred_element_type=F32) * (X_HEAD_DIM ** -0.5)).astype(BF16)
    k, v = k_ref[0], v_ref[0]
    outs = []
    for hd in range(X_HEADS):
        sl = slice(hd * X_HEAD_DIM, (hd + 1) * X_HEAD_DIM)
        s = _dot_nt(q[:, sl], k[:, sl])
        m = jnp.max(s, axis=-1, keepdims=True)
        p = jnp.exp(s - m)
        l = jnp.sum(p, axis=-1, keepdims=True)
        o = jnp.dot(p.astype(BF16), v[:, sl], preferred_element_type=F32)
        outs.append((o * (1.0 / l)).astype(BF16))
    o = jnp.concatenate(outs, axis=1)
    out_ref[0] = x + jnp.dot(o, wo_ref[...], preferred_element_type=F32)


def _cross_attention(x3, g, wq, k, v, wo, *, tm):
    b, s, d = x3.shape
    ml = k.shape[1]
    const = lambda shape: pl.BlockSpec(shape, lambda bi, i: (0,) * len(shape))
    return pl.pallas_call(
        _cross_kernel,
        grid=(b, s // tm),
        in_specs=[pl.BlockSpec((1, tm, d), lambda bi, i: (bi, i, 0)), const((1, d)), const(wq.shape),
                  pl.BlockSpec((1, ml, X_WIDTH), lambda bi, i: (bi, 0, 0)),
                  pl.BlockSpec((1, ml, X_WIDTH), lambda bi, i: (bi, 0, 0)),
                  const(wo.shape)],
        out_specs=pl.BlockSpec((1, tm, d), lambda bi, i: (bi, i, 0)),
        out_shape=jax.ShapeDtypeStruct((b, s, d), F32),
        compiler_params=_cparams(2),
        name="cross_attention",
    )(x3, g.reshape(1, d), wq, k, v, wo)


def _pack_bf16_pairs(x):
    n = x.shape[1] // 2
    bits = lax.bitcast_convert_type(x.astype(BF16).astype(F32), U32)
    return (bits[:, :n] >> 16) | (bits[:, n:] & jnp.uint32(0xFFFF0000))


def _unpack_bf16_pairs(p):
    return (lax.bitcast_convert_type(p << 16, F32),
            lax.bitcast_convert_type(p & jnp.uint32(0xFFFF0000), F32))


def _first_argmax(vals, rows, n):
    m = jnp.max(vals, axis=0, keepdims=True)
    idx = jnp.min(jnp.where(vals == m, rows, n), axis=0, keepdims=True)
    return m, idx


def _router_kernel(x_ref, g_ref, wr_ref, hp_ref, e_ref, gate_ref, cnt_ref):
    h = _rms(x_ref[...], g_ref[...])
    tm, d = h.shape
    logits = lax.dot_general(wr_ref[...], h, (((1,), (1,)), ((), ())),
                             precision=lax.Precision.HIGHEST, preferred_element_type=F32)
    rows8 = lax.broadcasted_iota(I32, (EXP_PER_GROUP, tm), 0)
    gl = jnp.where(rows8 < N_GROUPS, logits[N_EXPERTS:N_EXPERTS + 8], -jnp.inf)
    gmax, gidx = _first_argmax(gl, rows8, 8)
    g_gate = 1.0 / jnp.sum(jnp.exp(gl - gmax), axis=0, keepdims=True)
    sel = jnp.zeros((EXP_PER_GROUP, tm), F32)
    for grp in range(N_GROUPS):
        sel = jnp.where(gidx == grp, logits[grp * EXP_PER_GROUP:(grp + 1) * EXP_PER_GROUP], sel)
    v1, i1 = _first_argmax(sel, rows8, 8)
    sel2 = jnp.where(rows8 == i1, -jnp.inf, sel)
    v2, i2 = _first_argmax(sel2, rows8, 8)
    e2x = jnp.exp(v2 - v1)
    den = 1.0 + e2x
    e_ref[...] = jnp.concatenate([gidx * EXP_PER_GROUP + i1, gidx * EXP_PER_GROUP + i2], axis=0)
    gate_ref[...] = jnp.concatenate([(1.0 / den) * g_gate, (e2x / den) * g_gate], axis=0)

    packed = _pack_bf16_pairs(h)
    for c in range(PACKED_TILE_ROWS):
        hp_ref[pl.ds(c, tm, stride=PACKED_TILE_ROWS), :] = packed[:, c * LANES:(c + 1) * LANES]

    @pl.when(pl.program_id(0) == 0)
    def _():
        cnt_ref[...] = jnp.zeros_like(cnt_ref)

    rows32 = lax.broadcasted_iota(I32, (N_EXPERTS, tm), 0)
    e = e_ref[...]
    hits = (rows32 == e[0:1]).astype(F32) + (rows32 == e[1:2]).astype(F32)
    cnt_ref[...] += jnp.sum(hits, axis=1, keepdims=True)


def _router(x2d, g, wr_t, *, tm):
    t, d = x2d.shape
    return pl.pallas_call(
        _router_kernel,
        grid=(t // tm,),
        in_specs=[pl.BlockSpec((tm, d), lambda i: (i, 0)),
                  pl.BlockSpec((1, d), lambda i: (0, 0)),
                  pl.BlockSpec((ROUTER_ROWS, d), lambda i: (0, 0))],
        out_specs=[pl.BlockSpec((tm * PACKED_TILE_ROWS, LANES), lambda i: (i, 0)),
                   pl.BlockSpec((TOP_K, tm), lambda i: (0, i)),
                   pl.BlockSpec((TOP_K, tm), lambda i: (0, i)),
                   pl.BlockSpec((N_EXPERTS, 128), lambda i: (0, 0))],
        out_shape=[jax.ShapeDtypeStruct((t * PACKED_TILE_ROWS, LANES), U32),
                   jax.ShapeDtypeStruct((TOP_K, t), I32),
                   jax.ShapeDtypeStruct((TOP_K, t), F32),
                   jax.ShapeDtypeStruct((N_EXPERTS, 128), F32)],
        compiler_params=_cparams(1),
        name="moe_router",
    )(x2d, g.reshape(1, d), wr_t)


def _dest_kernel(e_ref, start_ref, d_ref, carry_ref):
    @pl.when(pl.program_id(0) == 0)
    def _():
        carry_ref[...] = jnp.zeros_like(carry_ref)

    e = e_ref[...]
    tm = e.shape[1]
    rows32 = lax.broadcasted_iota(I32, (N_EXPERTS, tm), 0)
    oh0 = (rows32 == e[0:1]).astype(F32)
    oh1 = (rows32 == e[1:2]).astype(F32)
    hits = oh0 + oh1
    earlier = (lax.broadcasted_iota(I32, (tm, tm), 0) < lax.broadcasted_iota(I32, (tm, tm), 1))
    prefix = jnp.dot(hits.astype(BF16), earlier.astype(BF16), preferred_element_type=F32)
    base = prefix + carry_ref[:, 0:1] + start_ref[:, 0:1]
    d0 = jnp.sum(oh0 * base, axis=0, keepdims=True)
    d1 = jnp.sum(oh1 * base, axis=0, keepdims=True)
    d_ref[...] = jnp.concatenate([d0, d1], axis=0).astype(I32)
    carry_ref[...] += jnp.sum(hits, axis=1, keepdims=True)


def _assignment_rows(e_t, start, *, tm):
    t = e_t.shape[1]
    return pl.pallas_call(
        _dest_kernel,
        grid=(t // tm,),
        in_specs=[pl.BlockSpec((TOP_K, tm), lambda i: (0, i)),
                  pl.BlockSpec((N_EXPERTS, 128), lambda i: (0, 0))],
        out_specs=pl.BlockSpec((TOP_K, tm), lambda i: (0, i)),
        out_shape=jax.ShapeDtypeStruct((TOP_K, t), I32),
        scratch_shapes=[pltpu.VMEM((N_EXPERTS, 128), F32)],
        compiler_params=_cparams(1),
        name="moe_assignment_rows",
    )(e_t, start)


def _token_rows(ref, token, rows):
    return ref.at[pl.ds(pl.multiple_of(token * rows, rows), rows)]


def _dispatch_kernel(d_ref, hp_ref, xg_in_ref, xg_ref, sem):
    del xg_in_ref
    rows = PACKED_TILE_ROWS
    tm = hp_ref.shape[0] // rows
    for t in range(tm):
        for k in range(TOP_K):
            pltpu.make_async_copy(_token_rows(hp_ref, t, rows), _token_rows(xg_ref, d_ref[k, t], rows),
                                  sem).start(priority=k)
    for k in range(TOP_K):
        pltpu.make_async_copy(hp_ref, xg_ref.at[pl.ds(0, tm * rows)], sem).wait()


def _dispatch(dest, hp, n_rows, *, tm):
    rows = PACKED_TILE_ROWS
    t = hp.shape[0] // rows
    return pl.pallas_call(
        _dispatch_kernel,
        grid=(t // tm,),
        in_specs=[pl.BlockSpec((TOP_K, tm), lambda i: (0, i), memory_space=pltpu.SMEM),
                  pl.BlockSpec((tm * rows, LANES), lambda i: (i, 0)),
                  pl.BlockSpec(memory_space=pl.ANY)],
        out_specs=pl.BlockSpec(memory_space=pl.ANY),
        out_shape=jax.ShapeDtypeStruct((n_rows * rows, LANES), U32),
        scratch_shapes=[pltpu.SemaphoreType.DMA(())],
        input_output_aliases={2: 0},
        compiler_params=_cparams(1),
        name="moe_dispatch",
    )(dest, hp, jnp.zeros((n_rows * rows, LANES), U32))


def _expert_kernel(be_ref, nu_ref, nxt_ref, par_ref, xg_ref, w1_ref, w3_ref, w2_ref, y_ref,
                   f1_ref, f3_ref, f2_ref, w1b_ref, w3b_ref, w2b_ref, sem):
    i = pl.program_id(0)

    def weight_copies(e, slot):
        return [pltpu.make_async_copy(w_ref.at[e], f_ref.at[slot], sem.at[slot, n])
                for n, (w_ref, f_ref) in enumerate(((w1_ref, f1_ref), (w3_ref, f3_ref), (w2_ref, f2_ref)))]

    @pl.when(i < nu_ref[0])
    def _():
        e = be_ref[i]
        slot = par_ref[i]

        @pl.when((i == 0) | (e != be_ref[jnp.maximum(i - 1, 0)]))
        def _():
            @pl.when(i == 0)
            def _():
                for copy in weight_copies(e, slot):
                    copy.start()

            for copy in weight_copies(e, slot):
                copy.wait()
            w1b_ref[...] = f1_ref[slot].astype(BF16)
            w3b_ref[...] = f3_ref[slot].astype(BF16)
            w2b_ref[...] = f2_ref[slot].astype(BF16)

            @pl.when(nxt_ref[i] != e)
            def _():
                for copy in weight_copies(nxt_ref[i], 1 - slot):
                    copy.start()

        halves = [_unpack_bf16_pairs(xg_ref[pl.ds(c, MOE_ROWS, stride=PACKED_TILE_ROWS), :])
                  for c in range(PACKED_TILE_ROWS)]
        x = jnp.concatenate([lo for lo, _ in halves] + [hi for _, hi in halves], axis=1).astype(BF16)
        h1 = jnp.dot(x, w1b_ref[...], preferred_element_type=F32)
        h3 = jnp.dot(x, w3b_ref[...], preferred_element_type=F32)
        a = (jax.nn.silu(h1) * h3).astype(BF16)
        y = _pack_bf16_pairs(jnp.dot(a, w2b_ref[...], preferred_element_type=F32))
        for c in range(PACKED_TILE_ROWS):
            y_ref[pl.ds(c, MOE_ROWS, stride=PACKED_TILE_ROWS), :] = y[:, c * LANES:(c + 1) * LANES]

    @pl.when(i >= nu_ref[0])
    def _():
        y_ref[...] = jnp.zeros_like(y_ref)


def _experts(block_e, n_used, next_e, parity, xg, w1, w3, w2):
    rows = PACKED_TILE_ROWS
    n_rows = xg.shape[0] // rows
    _, d, dff = w1.shape
    nblk = n_rows // MOE_ROWS
    live = lambda i, nu: jnp.maximum(jnp.minimum(i, nu[0] - 1), 0)
    any_space = pl.BlockSpec(memory_space=pl.ANY)
    return pl.pallas_call(
        _expert_kernel,
        grid_spec=pltpu.PrefetchScalarGridSpec(
            num_scalar_prefetch=4,
            grid=(nblk,),
            in_specs=[pl.BlockSpec((MOE_ROWS * rows, LANES), lambda i, be, nu, nx, par: (live(i, nu), 0)),
                      any_space, any_space, any_space],
            out_specs=pl.BlockSpec((MOE_ROWS * rows, LANES), lambda i, be, nu, nx, par: (i, 0)),
            scratch_shapes=[pltpu.VMEM((2, d, dff), F32), pltpu.VMEM((2, d, dff), F32),
                            pltpu.VMEM((2, dff, d), F32), pltpu.VMEM((d, dff), BF16),
                            pltpu.VMEM((d, dff), BF16), pltpu.VMEM((dff, d), BF16),
                            pltpu.SemaphoreType.DMA((2, 3))],
        ),
        out_shape=jax.ShapeDtypeStruct((n_rows * rows, LANES), U32),
        compiler_params=_cparams(1),
        name="moe_experts",
    )(block_e, n_used, next_e, parity, xg, w1, w3, w2)


def _combine_kernel(dc_ref, dn_ref, x_ref, gate_ref, gf_ref, y_ref, out_ref, buf, sem, *, final):
    i = pl.program_id(0)
    n = pl.num_programs(0)
    tm = x_ref.shape[0]
    rows = PACKED_TILE_ROWS

    def issue(d_ref, slot):
        for t in range(tm):
            for k in range(TOP_K):
                pltpu.make_async_copy(_token_rows(y_ref, d_ref[k, t], rows),
                                      _token_rows(buf.at[slot, k], t, rows),
                                      sem.at[slot]).start(priority=k)

    @pl.when(i == 0)
    def _():
        issue(dc_ref, 0)

    @pl.when(i + 1 < n)
    def _():
        issue(dn_ref, (i + 1) % 2)

    slot = i % 2
    for k in range(TOP_K):
        pltpu.make_async_copy(y_ref.at[pl.ds(0, tm * rows)], buf.at[slot, k], sem.at[slot]).wait()
    g = gate_ref[...]
    g0, g1 = g[:, 0:1], g[:, 1:2]
    low, high = [], []
    for c in range(rows):
        lo0, hi0 = _unpack_bf16_pairs(buf[slot, 0, pl.ds(c, tm, stride=rows), :])
        lo1, hi1 = _unpack_bf16_pairs(buf[slot, 1, pl.ds(c, tm, stride=rows), :])
        low.append(x_ref[:, c * LANES:(c + 1) * LANES] + (g0 * lo0 + g1 * lo1))
        high.append(x_ref[:, (rows + c) * LANES:(rows + c + 1) * LANES] + (g0 * hi0 + g1 * hi1))
    y = jnp.concatenate(low + high, axis=1)
    out_ref[...] = _rms(y, gf_ref[...]) if final else y


def _combine(dest, x2d, gates_tok, g_final, y, *, final, tm):
    t, d = x2d.shape
    nt = t // tm
    return pl.pallas_call(
        functools.partial(_combine_kernel, final=final),
        grid=(nt,),
        in_specs=[pl.BlockSpec((TOP_K, tm), lambda i: (0, i), memory_space=pltpu.SMEM),
                  pl.BlockSpec((TOP_K, tm), lambda i: (0, jnp.minimum(i + 1, nt - 1)),
                               memory_space=pltpu.SMEM),
                  pl.BlockSpec((tm, d), lambda i: (i, 0)),
                  pl.BlockSpec((tm, TOP_K), lambda i: (i, 0)),
                  pl.BlockSpec((1, d), lambda i: (0, 0)),
                  pl.BlockSpec(memory_space=pl.ANY)],
        out_specs=pl.BlockSpec((tm, d), lambda i: (i, 0)),
        out_shape=jax.ShapeDtypeStruct((t, d), F32),
        scratch_shapes=[pltpu.VMEM((2, TOP_K, tm * PACKED_TILE_ROWS, LANES), U32),
                        pltpu.SemaphoreType.DMA((2,))],
        compiler_params=_cparams(1),
        name="moe_combine",
    )(dest, dest, x2d, gates_tok, g_final.reshape(1, d), y)


def _swa_q_order():
    heads = []
    for kv_pair in range(A_KV_HEADS // 2):
        for g in range(A_GROUP):
            heads += [(2 * kv_pair) * A_GROUP + g, (2 * kv_pair + 1) * A_GROUP + g]
    return jnp.asarray([h * HEAD_DIM + c for h in heads for c in range(HEAD_DIM)], I32)


def kernel(x, mem, rel_bias, g_mix, w_in, sinks_a, w_a_out, w_b_out, w_gate, b_gate, w_o,
           g_x, g_mem, w_xq, w_xk, w_xv, w_xo, g_moe, w_rg, w_re, w1, w3, w2, g_final):
    b, s, d = x.shape
    t = b * s
    depth = g_mix.shape[0]
    perm = _swa_q_order()
    bias_a = _band_bias(rel_bias[:, :A_Q_HEADS], 1, A_WINDOW - 1)
    bias_b = [_band_bias(rel_bias[:, A_Q_HEADS:], r, w // r) for w, r in B_PATTERNS]
    n_assign = t * TOP_K
    n_rows = -(-(n_assign + N_EXPERTS * (MOE_ROWS - 1)) // MOE_ROWS) * MOE_ROWS
    n_blocks = n_rows // MOE_ROWS

    x2d = x.reshape(t, d)
    for l in range(depth):
        w_in_l = jnp.concatenate([w_in[l][:, :A_WIDTH][:, perm], w_in[l][:, A_WIDTH:]], axis=1).astype(BF16)
        qkv_a, qkv_b, q4, k4, v4, q16, k16, v16 = _in_proj(x2d, g_mix[l], w_in_l, b, s, tm=512)
        gates = _gate_proj(x2d, g_mix[l], w_gate[l].astype(BF16), b_gate[l], tm=1024, tn=1024)
        oa = _swa_attention(qkv_a.reshape(b, s, 2 * B_WIDTH), bias_a, sinks_a[l]).reshape(t, A_WIDTH)
        qkv_b4 = qkv_b.reshape(b, 1, s, 3 * B_WIDTH)
        o1, l1 = _dilated_attention(qkv_b4, qkv_b4, qkv_b4, (0, 1, 2), bias_b[0])
        o4, l4 = _dilated_attention(q4, k4, v4, (0, 0, 0), bias_b[1])
        o16, l16 = _dilated_attention(q16, k16, v16, (0, 0, 0), bias_b[2])
        x2d = _mix_out(x2d, oa, o1.reshape(t, B_WIDTH), l1.reshape(t, LANES), o4, l4, o16, l16, gates,
                       w_a_out[l][perm].astype(BF16), w_b_out[l].astype(BF16), w_o[l].astype(BF16),
                       s, tm=256)
        k_mem, v_mem = _mem_kv(mem, g_mem[l], w_xk[l].astype(BF16), w_xv[l].astype(BF16))
        x2d = _cross_attention(x2d.reshape(b, s, d), g_x[l], w_xq[l].astype(BF16), k_mem, v_mem,
                               w_xo[l].astype(BF16), tm=512).reshape(t, d)
        wr_t = jnp.concatenate([w_re[l].T, w_rg[l].T,
                                jnp.zeros((ROUTER_ROWS - N_EXPERTS - N_GROUPS, d), F32)], axis=0)
        hp, e_t, gate_t, cnt = _router(x2d, g_moe[l], wr_t, tm=512)
        counts = cnt[:, 0].astype(I32)
        padded = (counts + MOE_ROWS - 1) // MOE_ROWS * MOE_ROWS
        seg_end = jnp.cumsum(padded)
        seg_start = seg_end - padded
        n_used = (seg_end[-1] // MOE_ROWS).astype(I32).reshape(1)
        block_row = jnp.arange(n_blocks, dtype=I32) * MOE_ROWS
        block_e = jnp.minimum(jnp.sum(seg_end[None, :] <= block_row[:, None], axis=1),
                              N_EXPERTS - 1).astype(I32)
        start = jnp.broadcast_to(seg_start.astype(F32)[:, None], (N_EXPERTS, 128))
        dest = _assignment_rows(e_t, start, tm=512)
        xg = _dispatch(dest, hp, n_rows, tm=128)
        experts = jnp.arange(N_EXPERTS, dtype=I32)
        owns = padded > 0
        later = (experts[None, :] > experts[:, None]) & owns[None, :]
        next_owner = jnp.min(jnp.where(later, experts[None, :], N_EXPERTS), axis=1)
        next_owner = jnp.where(next_owner == N_EXPERTS, experts, next_owner).astype(I32)
        run_parity = ((jnp.cumsum(owns.astype(I32)) - 1) % 2).astype(I32)
        y = _experts(block_e, n_used, next_owner[block_e], run_parity[block_e], xg, w1[l], w3[l], w2[l])
        x2d = _combine(dest, x2d, gate_t.T, g_final, y, final=(l + 1 == depth), tm=128)
    return x2d.reshape(b, s, d)
```

```python
import functools
import math

import jax
import jax.numpy as jnp
from jax import lax
from jax.experimental import pallas as pl
from jax.experimental.pallas import tpu as pltpu

F32 = jnp.float32
BF16 = jnp.bfloat16
I32 = jnp.int32
U32 = jnp.uint32

HEAD_DIM = 64
PAIR = 2 * HEAD_DIM
A_Q_HEADS = 16
A_KV_HEADS = 4
A_GROUP = A_Q_HEADS // A_KV_HEADS
A_WIDTH = A_Q_HEADS * HEAD_DIM
A_KV_WIDTH = A_KV_HEADS * HEAD_DIM
A_WINDOW = 128
B_HEADS = 12
B_WIDTH = B_HEADS * HEAD_DIM
LANES = 128
LANE_CHUNKS = B_WIDTH // LANES
D_MODEL = 2048
PACKED_TILE_ROWS = D_MODEL // 2 // LANES
B_PATTERNS = ((128, 1), (512, 4), (2048, 16))
BLOCK = 128
Q_BLOCKS = 4
IN_COLS = A_WIDTH + 2 * A_KV_WIDTH + 3 * B_WIDTH
REL_BUCKETS = 32
REL_MAX_DIST = 2048
X_HEADS = 4
X_HEAD_DIM = 128
X_WIDTH = X_HEADS * X_HEAD_DIM
N_GROUPS = 4
EXP_PER_GROUP = 8
N_EXPERTS = N_GROUPS * EXP_PER_GROUP
TOP_K = 2
D_FF = 512
EPS = 1e-6
NEG = -1e30
MOE_ROWS = 256
VMEM_LIMIT = 56 * 1024 * 1024


def _cparams(n_axes):
    return pltpu.CompilerParams(dimension_semantics=("arbitrary",) * n_axes,
                                vmem_limit_bytes=VMEM_LIMIT)


def _rms(xf, g):
    return xf * lax.rsqrt(jnp.mean(xf * xf, axis=-1, keepdims=True) + EPS) * g


def _dot_nt(a, b):
    return lax.dot_general(a, b, (((1,), (1,)), ((), ())), preferred_element_type=F32)


def _load_resident(w_hbm_ref, w_ref, sem):
    @pl.when(pl.program_id(0) == 0)
    def _():
        copy = pltpu.make_async_copy(w_hbm_ref, w_ref, sem)
        copy.start()
        copy.wait()


def _in_proj_kernel(x_ref, g_ref, w_hbm_ref, oa_ref, ob_ref, q4_ref, k4_ref, v4_ref, q16_ref, k16_ref,
                    v16_ref, w_ref, acc_ref, mod4_ref, sem):
    _load_resident(w_hbm_ref, w_ref, sem)
    tm = x_ref.shape[0]
    tn = B_WIDTH
    n4 = tm // 4
    h = _rms(x_ref[...], g_ref[...]).astype(BF16)
    for j in range(2):
        oa_ref[:, j * tn:(j + 1) * tn] = jnp.dot(
            h, w_ref[:, j * tn:(j + 1) * tn], preferred_element_type=F32).astype(BF16)
    for part, (r4_ref, r16_ref) in enumerate(((q4_ref, q16_ref), (k4_ref, k16_ref), (v4_ref, v16_ref))):
        acc = jnp.dot(h, w_ref[:, (2 + part) * tn:(3 + part) * tn], preferred_element_type=F32)
        ob_ref[:, part * tn:(part + 1) * tn] = acc.astype(BF16)
        for cc in range(LANE_CHUNKS):
            acc_ref[part, cc] = acc[:, cc * LANES:(cc + 1) * LANES]
        for c in range(4):
            rows = [acc_ref[part, cc, pl.ds(c, n4, stride=4), :] for cc in range(LANE_CHUNKS)]
            r4_ref[0, c] = jnp.concatenate(rows, axis=1).astype(BF16)
            for cc in range(LANE_CHUNKS):
                mod4_ref[part, cc, c * n4:(c + 1) * n4, :] = rows[cc]
        for c in range(16):
            rows = [mod4_ref[part, cc, pl.ds((c % 4) * n4 + c // 4, tm // 16, stride=4), :]
                    for cc in range(LANE_CHUNKS)]
            r16_ref[0, c] = jnp.concatenate(rows, axis=1).astype(BF16)


def _in_proj(x2d, g, w, b, s, *, tm):
    t, d = x2d.shape
    tn = B_WIDTH
    tiles = s // tm
    res_shape = lambda r: jax.ShapeDtypeStruct((b, r, s // r, B_WIDTH), BF16)
    res_spec = lambda r: pl.BlockSpec((1, r, tm // r, B_WIDTH), lambda i: (i // tiles, 0, i % tiles, 0))
    return pl.pallas_call(
        _in_proj_kernel,
        grid=(t // tm,),
        in_specs=[
            pl.BlockSpec((tm, d), lambda i: (i, 0)),
            pl.BlockSpec((1, d), lambda i: (0, 0)),
            pl.BlockSpec(memory_space=pl.ANY),
        ],
        out_specs=[pl.BlockSpec((tm, 2 * tn), lambda i: (i, 0)), pl.BlockSpec((tm, 3 * tn), lambda i: (i, 0))]
                  + [res_spec(4)] * 3 + [res_spec(16)] * 3,
        out_shape=[jax.ShapeDtypeStruct((t, 2 * tn), BF16), jax.ShapeDtypeStruct((t, 3 * tn), BF16)]
                  + [res_shape(4)] * 3 + [res_shape(16)] * 3,
        scratch_shapes=[pltpu.VMEM(w.shape, BF16), pltpu.VMEM((3, LANE_CHUNKS, tm, LANES), F32),
                        pltpu.VMEM((3, LANE_CHUNKS, tm, LANES), F32), pltpu.SemaphoreType.DMA(())],
        compiler_params=_cparams(1),
        name="in_proj",
    )(x2d, g.reshape(1, d), w)


def _gate_proj_kernel(x_ref, g_ref, w_hbm_ref, b_ref, o_ref, w_ref, sem, *, tn):
    _load_resident(w_hbm_ref, w_ref, sem)
    h = _rms(x_ref[...], g_ref[...]).astype(BF16)
    for j in range(w_ref.shape[1] // tn):
        cols = slice(j * tn, (j + 1) * tn)
        acc = jnp.dot(h, w_ref[:, cols], preferred_element_type=F32)
        o_ref[:, cols] = jax.nn.sigmoid(acc + b_ref[:, cols]).astype(o_ref.dtype)


def _gate_proj(x2d, g, w, b, *, tm, tn):
    t, d = x2d.shape
    n = w.shape[1]
    return pl.pallas_call(
        functools.partial(_gate_proj_kernel, tn=tn),
        grid=(t // tm,),
        in_specs=[
            pl.BlockSpec((tm, d), lambda i: (i, 0)),
            pl.BlockSpec((1, d), lambda i: (0, 0)),
            pl.BlockSpec(memory_space=pl.ANY),
            pl.BlockSpec((1, n), lambda i: (0, 0)),
        ],
        out_specs=pl.BlockSpec((tm, n), lambda i: (i, 0)),
        out_shape=jax.ShapeDtypeStruct((t, n), BF16),
        scratch_shapes=[pltpu.VMEM(w.shape, BF16), pltpu.SemaphoreType.DMA(())],
        compiler_params=_cparams(1),
        name="gate_proj",
    )(x2d, g.reshape(1, d), w, b.reshape(1, n))


def t5_bucket(dist):
    max_exact = REL_BUCKETS // 2
    d = jnp.maximum(dist, 0)
    df = jnp.maximum(d, 1).astype(jnp.float32)
    large = max_exact + (jnp.log(df / max_exact) / math.log(REL_MAX_DIST / max_exact)
                         * (REL_BUCKETS - max_exact)).astype(jnp.int32)
    large = jnp.minimum(large, REL_BUCKETS - 1)
    return jnp.where(d < max_exact, d, large)


def _band_bias(table, step, max_off):
    heads = table.shape[1]
    span = 3 * BLOCK
    dist = 2 * BLOCK - 1 - jnp.arange(span)
    z = jnp.where(((dist >= 0) & (dist <= max_off))[:, None], table[t5_bucket(dist * step)], NEG)
    z = z.T.astype(F32)
    rows = jnp.tile(z, (1, BLOCK))[:, :BLOCK * (span - 1)].reshape(heads, BLOCK, span - 1)
    rest = rows[:, :, BLOCK - 1:3 * BLOCK - 1]
    first = jnp.where(jnp.arange(2 * BLOCK)[None, None, :] >= BLOCK, rest, NEG)
    return jnp.stack([first, rest])


def _softmax_pv(s, v_pair, sink):
    m = jnp.max(s, axis=-1, keepdims=True)
    if sink is not None:
        m = jnp.maximum(m, sink)
    p = jnp.exp(s - m)
    l = jnp.sum(p, axis=-1, keepdims=True)
    if sink is not None:
        l = l + jnp.exp(sink - m)
    o = jnp.dot(p.astype(BF16), v_pair, preferred_element_type=F32)
    return o, m, l


def _swa_kernel(sink_ref, q_ref, kp_ref, kc_ref, vp_ref, vc_ref, bias_ref, o_ref):
    first = jnp.minimum(pl.program_id(1), 1)
    q_all = q_ref[0] * jnp.asarray(HEAD_DIM ** -0.5, BF16)
    k_all = jnp.concatenate([kp_ref[0], kc_ref[0]], axis=0)
    v_all = jnp.concatenate([vp_ref[0], vc_ref[0]], axis=0)
    lane = lax.broadcasted_iota(I32, (1, PAIR), 1)
    low = lane < HEAD_DIM
    for blk in range(Q_BLOCKS):
        rows = slice(blk * BLOCK, (blk + 1) * BLOCK)
        q = q_all[rows]
        k = k_all[blk * BLOCK:(blk + 2) * BLOCK]
        v = v_all[blk * BLOCK:(blk + 2) * BLOCK]
        variant = first if blk == 0 else 1
        for kv_pair in range(A_KV_HEADS // 2):
            k_pair = k[:, kv_pair * PAIR:(kv_pair + 1) * PAIR]
            v_pair = v[:, kv_pair * PAIR:(kv_pair + 1) * PAIR]
            k_half = (jnp.where(low, k_pair, jnp.zeros_like(k_pair)),
                      jnp.where(low, jnp.zeros_like(k_pair), k_pair))
            for g in range(A_GROUP):
                col = (kv_pair * A_GROUP + g) * PAIR
                q_pair = q[:, col:col + PAIR]
                outs = []
                for half in range(2):
                    head = (2 * kv_pair + half) * A_GROUP + g
                    s = _dot_nt(q_pair, k_half[half]) + bias_ref[variant, head]
                    o, _, l = _softmax_pv(s, v_pair, sink_ref[head])
                    outs.append(o * (1.0 / l))
                o_ref[0, rows, col:col + PAIR] = jnp.where(low, outs[0], outs[1]).astype(o_ref.dtype)


def _swa_attention(qkv3, bias, sinks):
    b, s, _ = qkv3.shape
    step = Q_BLOCKS * BLOCK
    kcol = A_WIDTH // A_KV_WIDTH
    prev = lambda i: jnp.maximum(i * Q_BLOCKS - 1, 0)
    return pl.pallas_call(
        _swa_kernel,
        grid=(b, s // step),
        in_specs=[
            pl.BlockSpec(memory_space=pltpu.SMEM),
            pl.BlockSpec((1, step, A_WIDTH), lambda bi, i: (bi, i, 0)),
            pl.BlockSpec((1, BLOCK, A_KV_WIDTH), lambda bi, i: (bi, prev(i), kcol)),
            pl.BlockSpec((1, step, A_KV_WIDTH), lambda bi, i: (bi, i, kcol)),
            pl.BlockSpec((1, BLOCK, A_KV_WIDTH), lambda bi, i: (bi, prev(i), kcol + 1)),
            pl.BlockSpec((1, step, A_KV_WIDTH), lambda bi, i: (bi, i, kcol + 1)),
            pl.BlockSpec((2, A_Q_HEADS, BLOCK, 2 * BLOCK), lambda bi, i: (0, 0, 0, 0)),
        ],
        out_specs=pl.BlockSpec((1, step, A_WIDTH), lambda bi, i: (bi, i, 0)),
        out_shape=jax.ShapeDtypeStruct((b, s, A_WIDTH), BF16),
        compiler_params=_cparams(2),
        name="swa_attention",
    )(sinks, qkv3, qkv3, qkv3, qkv3, qkv3, bias)


def _dilated_kernel(q_ref, kp_ref, kc_ref, vp_ref, vc_ref, bias_ref, o_ref, lse_ref):
    first = jnp.minimum(pl.program_id(2), 1)
    q_all = q_ref[...] * jnp.asarray(HEAD_DIM ** -0.5, BF16)
    k_all = jnp.concatenate([kp_ref[...], kc_ref[...]], axis=0)
    v_all = jnp.concatenate([vp_ref[...], vc_ref[...]], axis=0)
    lane = lax.broadcasted_iota(I32, (1, PAIR), 1)
    low = lane < HEAD_DIM
    for blk in range(Q_BLOCKS):
        rows = slice(blk * BLOCK, (blk + 1) * BLOCK)
        variant = first if blk == 0 else 1
        lse_tile = jnp.zeros((BLOCK, LANES), F32)
        for pair in range(B_HEADS // 2):
            col = pair * PAIR
            q_pair = q_all[rows, col:col + PAIR]
            k_pair = k_all[blk * BLOCK:(blk + 2) * BLOCK, col:col + PAIR]
            v_pair = v_all[blk * BLOCK:(blk + 2) * BLOCK, col:col + PAIR]
            k_half = (jnp.where(low, k_pair, jnp.zeros_like(k_pair)),
                      jnp.where(low, jnp.zeros_like(k_pair), k_pair))
            outs = []
            for half in range(2):
                head = 2 * pair + half
                s = _dot_nt(q_pair, k_half[half]) + bias_ref[variant, head]
                o, m, l = _softmax_pv(s, v_pair, None)
                outs.append(o * (1.0 / l))
                lse_tile = jnp.where(lane == head, m + jnp.log(l), lse_tile)
            o_ref[rows, col:col + PAIR] = jnp.where(low, outs[0], outs[1]).astype(o_ref.dtype)
        lse_ref[rows, :] = lse_tile


def _dilated_attention(q, k, v, cols, bias):
    b, r, n, _ = q.shape
    step = Q_BLOCKS * BLOCK
    assert n % step == 0, (n, step)
    qc, kc, vc = cols
    prev = lambda j: jnp.maximum(j * Q_BLOCKS - 1, 0)
    cur = lambda col, w=B_WIDTH: pl.BlockSpec((None, None, step, w), lambda bi, c, j: (bi, c, j, col))
    old = lambda col: pl.BlockSpec((None, None, BLOCK, B_WIDTH), lambda bi, c, j: (bi, c, prev(j), col))
    return pl.pallas_call(
        _dilated_kernel,
        grid=(b, r, n // step),
        in_specs=[cur(qc), old(kc), cur(kc), old(vc), cur(vc),
                  pl.BlockSpec((2, B_HEADS, BLOCK, 2 * BLOCK), lambda bi, c, j: (0, 0, 0, 0))],
        out_specs=[cur(0), cur(0, LANES)],
        out_shape=[jax.ShapeDtypeStruct((b, r, n, B_WIDTH), BF16),
                   jax.ShapeDtypeStruct((b, r, n, LANES), F32)],
        compiler_params=_cparams(3),
        name=f"dilated_attention_r{r}",
    )(q, k, k, v, v, bias)


def _mix_out_kernel(x_ref, oa_ref, o1_ref, l1_ref, o4_ref, l4_ref, o16_ref, l16_ref, ga_ref, gb_ref,
                    wa_ref, wb_ref, wo_ref, ex_ref, out_ref, seq_ref):
    tm = x_ref.shape[0]
    seq = []
    base = 0
    for r, ref in ((4, o4_ref), (4, l4_ref), (16, o16_ref), (16, l16_ref)):
        chunks = ref.shape[-1] // LANES
        for c in range(r):
            rows = ref[0, c].astype(F32)
            for cc in range(chunks):
                seq_ref[base + cc, pl.ds(c, tm // r, stride=r), :] = rows[:, cc * LANES:(cc + 1) * LANES]
        seq.append(jnp.concatenate([seq_ref[base + cc] for cc in range(chunks)], axis=1))
        base += chunks
    o2, l2, o3, l3 = seq
    l1 = l1_ref[...]
    m = jnp.maximum(jnp.maximum(l1, l2), l3)
    e1, e2, e3 = jnp.exp(l1 - m), jnp.exp(l2 - m), jnp.exp(l3 - m)
    inv = 1.0 / (e1 + e2 + e3)

    def widen(w):
        hi = w.astype(BF16)
        lo = (w - hi.astype(F32)).astype(BF16)
        return (jnp.dot(hi, ex_ref[...], preferred_element_type=F32)
                + jnp.dot(lo, ex_ref[...], preferred_element_type=F32))

    ob = (widen(e1 * inv) * o1_ref[...].astype(F32) + widen(e2 * inv) * o2 + widen(e3 * inv) * o3)
    ya = jnp.dot(oa_ref[...], wa_ref[...], preferred_element_type=F32)
    yb = jnp.dot(ob.astype(BF16), wb_ref[...], preferred_element_type=F32)
    mixed = ga_ref[...].astype(F32) * ya + gb_ref[...].astype(F32) * yb
    out_ref[...] = x_ref[...] + jnp.dot(mixed.astype(BF16), wo_ref[...], preferred_element_type=F32)


def _mix_out(x2d, oa, o1, l1, o4, l4, o16, l16, gates, wa, wb, wo, s, *, tm):
    t, d = x2d.shape
    tiles = s // tm
    row = lambda w: pl.BlockSpec((tm, w), lambda i: (i, 0))
    res = lambda r, w: pl.BlockSpec((1, r, tm // r, w), lambda i: (i // tiles, 0, i % tiles, 0))
    const = lambda shape: pl.BlockSpec(shape, lambda i: (0, 0))
    head_of_col = jnp.arange(B_WIDTH, dtype=I32)[None, :] // HEAD_DIM
    expand = (jnp.arange(LANES, dtype=I32)[:, None] == head_of_col).astype(BF16)
    return pl.pallas_call(
        _mix_out_kernel,
        grid=(t // tm,),
        in_specs=[row(d), row(A_WIDTH), row(B_WIDTH), row(LANES), res(4, B_WIDTH), res(4, LANES),
                  res(16, B_WIDTH), res(16, LANES),
                  pl.BlockSpec((tm, d), lambda i: (i, 0)), pl.BlockSpec((tm, d), lambda i: (i, 1)),
                  const(wa.shape), const(wb.shape), const(wo.shape), const(expand.shape)],
        out_specs=row(d),
        out_shape=jax.ShapeDtypeStruct((t, d), F32),
        scratch_shapes=[pltpu.VMEM((2 * (LANE_CHUNKS + 1), tm, LANES), F32)],
        compiler_params=_cparams(1),
        name="mix_out",
    )(x2d, oa, o1, l1, o4, l4, o16, l16, gates, gates, wa, wb, wo, expand)


def _mem_kv_kernel(mem_ref, g_ref, wk_ref, wv_ref, k_ref, v_ref):
    hn = _rms(mem_ref[0], g_ref[...]).astype(BF16)
    k_ref[0] = jnp.dot(hn, wk_ref[...], preferred_element_type=F32).astype(BF16)
    v_ref[0] = jnp.dot(hn, wv_ref[...], preferred_element_type=F32).astype(BF16)


def _mem_kv(mem, g, wk, wv):
    b, ml, d = mem.shape
    const = lambda shape: pl.BlockSpec(shape, lambda bi: (0,) * len(shape))
    out = jax.ShapeDtypeStruct((b, ml, X_WIDTH), BF16)
    blk = pl.BlockSpec((1, ml, X_WIDTH), lambda bi: (bi, 0, 0))
    return pl.pallas_call(
        _mem_kv_kernel,
        grid=(b,),
        in_specs=[pl.BlockSpec((1, ml, d), lambda bi: (bi, 0, 0)), const((1, d)),
                  const(wk.shape), const(wv.shape)],
        out_specs=[blk, blk],
        out_shape=[out, out],
        compiler_params=_cparams(1),
        name="mem_kv",
    )(mem, g.reshape(1, d), wk, wv)


def _cross_kernel(x_ref, g_ref, wq_ref, k_ref, v_ref, wo_ref, out_ref):
    x = x_ref[0]
    h = _rms(x, g_ref[...]).astype(BF16)
    q = (jnp.dot(h, wq_ref[...], preferred_element_type=F32) * (X_HEAD_DIM ** -0.5)).astype(BF16)
    k, v = k_ref[0], v_ref[0]
    outs = []
    for hd in range(X_HEADS):
        sl = slice(hd * X_HEAD_DIM, (hd + 1) * X_HEAD_DIM)
        s = _dot_nt(q[:, sl], k[:, sl])
        m = jnp.max(s, axis=-1, keepdims=True)
        p = jnp.exp(s - m)
        l = jnp.sum(p, axis=-1, keepdims=True)
        o = jnp.dot(p.astype(BF16), v[:, sl], preferred_element_type=F32)
        outs.append((o * (1.0 / l)).astype(BF16))
    o = jnp.concatenate(outs, axis=1)
    out_ref[0] = x + jnp.dot(o, wo_ref[...], preferred_element_type=F32)


def _cross_attention(x3, g, wq, k, v, wo, *, tm):
    b, s, d = x3.shape
    ml = k.shape[1]
    const = lambda shape: pl.BlockSpec(shape, lambda bi, i: (0,) * len(shape))
    return pl.pallas_call(
        _cross_kernel,
        grid=(b, s // tm),
        in_specs=[pl.BlockSpec((1, tm, d), lambda bi, i: (bi, i, 0)), const((1, d)), const(wq.shape),
                  pl.BlockSpec((1, ml, X_WIDTH), lambda bi, i: (bi, 0, 0)),
                  pl.BlockSpec((1, ml, X_WIDTH), lambda bi, i: (bi, 0, 0)),
                  const(wo.shape)],
        out_specs=pl.BlockSpec((1, tm, d), lambda bi, i: (bi, i, 0)),
        out_shape=jax.ShapeDtypeStruct((b, s, d), F32),
        compiler_params=_cparams(2),
        name="cross_attention",
    )(x3, g.reshape(1, d), wq, k, v, wo)


def _pack_bf16_pairs(x):
    n = x.shape[1] // 2
    bits = lax.bitcast_convert_type(x.astype(BF16).astype(F32), U32)
    return (bits[:, :n] >> 16) | (bits[:, n:] & jnp.uint32(0xFFFF0000))


def _unpack_bf16_pairs(p):
    return (lax.bitcast_convert_type(p << 16, F32),
            lax.bitcast_convert_type(p & jnp.uint32(0xFFFF0000), F32))


def _first_argmax(vals, rows, n):
    m = jnp.max(vals, axis=0, keepdims=True)
    idx = jnp.min(jnp.where(vals == m, rows, n), axis=0, keepdims=True)
    return m, idx


def _router_kernel(x_ref, g_ref, whi_ref, wlo_ref, hp_ref, e_ref, gate_ref, cnt_ref):
    h = _rms(x_ref[...], g_ref[...])
    tm, d = h.shape
    h_hi = h.astype(BF16)
    h_hi32 = h_hi.astype(F32)
    h_lo = (h - h_hi32).astype(BF16)
    logits = (jnp.dot(h_hi, whi_ref[...], preferred_element_type=F32)
              + jnp.dot(h_lo, whi_ref[...], preferred_element_type=F32)
              + jnp.dot(h_hi, wlo_ref[...], preferred_element_type=F32)).T
    rows8 = lax.broadcasted_iota(I32, (EXP_PER_GROUP, tm), 0)
    gl = jnp.where(rows8 < N_GROUPS, logits[N_EXPERTS:N_EXPERTS + 8], -jnp.inf)
    gmax, gidx = _first_argmax(gl, rows8, 8)
    g_gate = 1.0 / jnp.sum(jnp.exp(gl - gmax), axis=0, keepdims=True)
    sel = jnp.zeros((EXP_PER_GROUP, tm), F32)
    for grp in range(N_GROUPS):
        sel = jnp.where(gidx == grp, logits[grp * EXP_PER_GROUP:(grp + 1) * EXP_PER_GROUP], sel)
    v1, i1 = _first_argmax(sel, rows8, 8)
    sel2 = jnp.where(rows8 == i1, -jnp.inf, sel)
    v2, i2 = _first_argmax(sel2, rows8, 8)
    e2x = jnp.exp(v2 - v1)
    den = 1.0 + e2x
    e_ref[...] = jnp.concatenate([gidx * EXP_PER_GROUP + i1, gidx * EXP_PER_GROUP + i2], axis=0)
    gate_ref[...] = jnp.concatenate([(1.0 / den) * g_gate, (e2x / den) * g_gate], axis=0)

    bits = lax.bitcast_convert_type(h_hi32, U32)
    packed = (bits[:, :d // 2] >> 16) | (bits[:, d // 2:] & jnp.uint32(0xFFFF0000))
    for c in range(PACKED_TILE_ROWS):
        hp_ref[pl.ds(c, tm, stride=PACKED_TILE_ROWS), :] = packed[:, c * LANES:(c + 1) * LANES]

    @pl.when(pl.program_id(0) == 0)
    def _():
        cnt_ref[...] = jnp.zeros_like(cnt_ref)

    rows32 = lax.broadcasted_iota(I32, (N_EXPERTS, tm), 0)
    e = e_ref[...]
    hits = (rows32 == e[0:1]).astype(F32) + (rows32 == e[1:2]).astype(F32)
    cnt_ref[...] += jnp.sum(hits, axis=1, keepdims=True)


def _router(x2d, g, w_router, *, tm):
    t, d = x2d.shape
    w_hi = w_router.astype(BF16)
    w_lo = (w_router - w_hi.astype(F32)).astype(BF16)
    return pl.pallas_call(
        _router_kernel,
        grid=(t // tm,),
        in_specs=[pl.BlockSpec((tm, d), lambda i: (i, 0)),
                  pl.BlockSpec((1, d), lambda i: (0, 0)),
                  pl.BlockSpec((d, LANES), lambda i: (0, 0)),
                  pl.BlockSpec((d, LANES), lambda i: (0, 0))],
        out_specs=[pl.BlockSpec((tm * PACKED_TILE_ROWS, LANES), lambda i: (i, 0)),
                   pl.BlockSpec((TOP_K, tm), lambda i: (0, i)),
                   pl.BlockSpec((TOP_K, tm), lambda i: (0, i)),
                   pl.BlockSpec((N_EXPERTS, 128), lambda i: (0, 0))],
        out_shape=[jax.ShapeDtypeStruct((t * PACKED_TILE_ROWS, LANES), U32),
                   jax.ShapeDtypeStruct((TOP_K, t), I32),
                   jax.ShapeDtypeStruct((TOP_K, t), F32),
                   jax.ShapeDtypeStruct((N_EXPERTS, 128), F32)],
        compiler_params=_cparams(1),
        name="moe_router",
    )(x2d, g.reshape(1, d), w_hi, w_lo)


def _dest_kernel(e_ref, start_ref, d_ref, carry_ref):
    @pl.when(pl.program_id(0) == 0)
    def _():
        carry_ref[...] = jnp.zeros_like(carry_ref)

    e = e_ref[...]
    tm = e.shape[1]
    rows32 = lax.broadcasted_iota(I32, (N_EXPERTS, tm), 0)
    oh0 = (rows32 == e[0:1]).astype(F32)
    oh1 = (rows32 == e[1:2]).astype(F32)
    hits = oh0 + oh1
    earlier = (lax.broadcasted_iota(I32, (tm, tm), 0) < lax.broadcasted_iota(I32, (tm, tm), 1))
    prefix = jnp.dot(hits.astype(BF16), earlier.astype(BF16), preferred_element_type=F32)
    base = prefix + carry_ref[:, 0:1] + start_ref[:, 0:1]
    d0 = jnp.sum(oh0 * base, axis=0, keepdims=True)
    d1 = jnp.sum(oh1 * base, axis=0, keepdims=True)
    d_ref[...] = jnp.concatenate([d0, d1], axis=0).astype(I32)
    carry_ref[...] += jnp.sum(hits, axis=1, keepdims=True)


def _assignment_rows(e_t, start, *, tm):
    t = e_t.shape[1]
    return pl.pallas_call(
        _dest_kernel,
        grid=(t // tm,),
        in_specs=[pl.BlockSpec((TOP_K, tm), lambda i: (0, i)),
                  pl.BlockSpec((N_EXPERTS, 128), lambda i: (0, 0))],
        out_specs=pl.BlockSpec((TOP_K, tm), lambda i: (0, i)),
        out_shape=jax.ShapeDtypeStruct((TOP_K, t), I32),
        scratch_shapes=[pltpu.VMEM((N_EXPERTS, 128), F32)],
        compiler_params=_cparams(1),
        name="moe_assignment_rows",
    )(e_t, start)


def _token_rows(ref, token, rows):
    return ref.at[pl.ds(pl.multiple_of(token * rows, rows), rows)]


def _dispatch_kernel(d_ref, hp_ref, xg_in_ref, xg_ref, sem):
    del xg_in_ref
    rows = PACKED_TILE_ROWS
    tm = hp_ref.shape[0] // rows
    for t in range(tm):
        for k in range(TOP_K):
            pltpu.make_async_copy(_token_rows(hp_ref, t, rows), _token_rows(xg_ref, d_ref[k, t], rows),
                                  sem).start(priority=k)
    for k in range(TOP_K):
        pltpu.make_async_copy(hp_ref, xg_ref.at[pl.ds(0, tm * rows)], sem).wait()


def _dispatch(dest, hp, n_rows, *, tm):
    rows = PACKED_TILE_ROWS
    t = hp.shape[0] // rows
    return pl.pallas_call(
        _dispatch_kernel,
        grid=(t // tm,),
        in_specs=[pl.BlockSpec((TOP_K, tm), lambda i: (0, i), memory_space=pltpu.SMEM),
                  pl.BlockSpec((tm * rows, LANES), lambda i: (i, 0)),
                  pl.BlockSpec(memory_space=pl.ANY)],
        out_specs=pl.BlockSpec(memory_space=pl.ANY),
        out_shape=jax.ShapeDtypeStruct((n_rows * rows, LANES), U32),
        scratch_shapes=[pltpu.SemaphoreType.DMA(())],
        input_output_aliases={2: 0},
        compiler_params=_cparams(1),
        name="moe_dispatch",
    )(dest, hp, jnp.zeros((n_rows * rows, LANES), U32))


def _expert_kernel(be_ref, nu_ref, nxt_ref, par_ref, xg_ref, w1_ref, w3_ref, w2_ref, y_ref,
                   f1_ref, f3_ref, f2_ref, w1b_ref, w3b_ref, w2b_ref, sem):
    i = pl.program_id(0)

    def weight_copies(e, slot):
        return [pltpu.make_async_copy(w_ref.at[e], f_ref.at[slot], sem.at[slot, n])
                for n, (w_ref, f_ref) in enumerate(((w1_ref, f1_ref), (w3_ref, f3_ref), (w2_ref, f2_ref)))]

    @pl.when(i < nu_ref[0])
    def _():
        e = be_ref[i]
        slot = par_ref[i]

        @pl.when((i == 0) | (e != be_ref[jnp.maximum(i - 1, 0)]))
        def _():
            @pl.when(i == 0)
            def _():
                for copy in weight_copies(e, slot):
                    copy.start()

            for copy in weight_copies(e, slot):
                copy.wait()
            w1b_ref[...] = f1_ref[slot].astype(BF16)
            w3b_ref[...] = f3_ref[slot].astype(BF16)
            w2b_ref[...] = f2_ref[slot].astype(BF16)

            @pl.when(nxt_ref[i] != e)
            def _():
                for copy in weight_copies(nxt_ref[i], 1 - slot):
                    copy.start()

        halves = [_unpack_bf16_pairs(xg_ref[pl.ds(c, MOE_ROWS, stride=PACKED_TILE_ROWS), :])
                  for c in range(PACKED_TILE_ROWS)]
        x = jnp.concatenate([lo for lo, _ in halves] + [hi for _, hi in halves], axis=1).astype(BF16)
        h1 = jnp.dot(x, w1b_ref[...], preferred_element_type=F32)
        h3 = jnp.dot(x, w3b_ref[...], preferred_element_type=F32)
        a = (jax.nn.silu(h1) * h3).astype(BF16)
        y = _pack_bf16_pairs(jnp.dot(a, w2b_ref[...], preferred_element_type=F32))
        for c in range(PACKED_TILE_ROWS):
            y_ref[pl.ds(c, MOE_ROWS, stride=PACKED_TILE_ROWS), :] = y[:, c * LANES:(c + 1) * LANES]

    @pl.when(i >= nu_ref[0])
    def _():
        y_ref[...] = jnp.zeros_like(y_ref)


def _experts(block_e, n_used, next_e, parity, xg, w1, w3, w2):
    rows = PACKED_TILE_ROWS
    n_rows = xg.shape[0] // rows
    _, d, dff = w1.shape
    nblk = n_rows // MOE_ROWS
    live = lambda i, nu: jnp.maximum(jnp.minimum(i, nu[0] - 1), 0)
    any_space = pl.BlockSpec(memory_space=pl.ANY)
    return pl.pallas_call(
        _expert_kernel,
        grid_spec=pltpu.PrefetchScalarGridSpec(
            num_scalar_prefetch=4,
            grid=(nblk,),
            in_specs=[pl.BlockSpec((MOE_ROWS * rows, LANES), lambda i, be, nu, nx, par: (live(i, nu), 0)),
                      any_space, any_space, any_space],
            out_specs=pl.BlockSpec((MOE_ROWS * rows, LANES), lambda i, be, nu, nx, par: (i, 0)),
            scratch_shapes=[pltpu.VMEM((2, d, dff), F32), pltpu.VMEM((2, d, dff), F32),
                            pltpu.VMEM((2, dff, d), F32), pltpu.VMEM((d, dff), BF16),
                            pltpu.VMEM((d, dff), BF16), pltpu.VMEM((dff, d), BF16),
                            pltpu.SemaphoreType.DMA((2, 3))],
        ),
        out_shape=jax.ShapeDtypeStruct((n_rows * rows, LANES), U32),
        compiler_params=_cparams(1),
        name="moe_experts",
    )(block_e, n_used, next_e, parity, xg, w1, w3, w2)


def _combine_kernel(dc_ref, dn_ref, x_ref, gate_ref, gf_ref, y_ref, out_ref, buf, sem, *, final):
    i = pl.program_id(0)
    n = pl.num_programs(0)
    tm = x_ref.shape[0]
    rows = PACKED_TILE_ROWS

    def issue(d_ref, slot):
        for t in range(tm):
            for k in range(TOP_K):
                pltpu.make_async_copy(_token_rows(y_ref, d_ref[k, t], rows),
                                      _token_rows(buf.at[slot, k], t, rows),
                                      sem.at[slot]).start(priority=k)

    @pl.when(i == 0)
    def _():
        issue(dc_ref, 0)

    @pl.when(i + 1 < n)
    def _():
        issue(dn_ref, (i + 1) % 2)

    slot = i % 2
    for k in range(TOP_K):
        pltpu.make_async_copy(y_ref.at[pl.ds(0, tm * rows)], buf.at[slot, k], sem.at[slot]).wait()
    g = gate_ref[...]
    g0, g1 = g[:, 0:1], g[:, 1:2]
    low, high = [], []
    for c in range(rows):
        lo0, hi0 = _unpack_bf16_pairs(buf[slot, 0, pl.ds(c, tm, stride=rows), :])
        lo1, hi1 = _unpack_bf16_pairs(buf[slot, 1, pl.ds(c, tm, stride=rows), :])
        low.append(x_ref[:, c * LANES:(c + 1) * LANES] + (g0 * lo0 + g1 * lo1))
        high.append(x_ref[:, (rows + c) * LANES:(rows + c + 1) * LANES] + (g0 * hi0 + g1 * hi1))
    y = jnp.concatenate(low + high, axis=1)
    out_ref[...] = _rms(y, gf_ref[...]) if final else y


def _combine(dest, x2d, gates_tok, g_final, y, *, final, tm):
    t, d = x2d.shape
    nt = t // tm
    return pl.pallas_call(
        functools.partial(_combine_kernel, final=final),
        grid=(nt,),
        in_specs=[pl.BlockSpec((TOP_K, tm), lambda i: (0, i), memory_space=pltpu.SMEM),
                  pl.BlockSpec((TOP_K, tm), lambda i: (0, jnp.minimum(i + 1, nt - 1)),
                               memory_space=pltpu.SMEM),
                  pl.BlockSpec((tm, d), lambda i: (i, 0)),
                  pl.BlockSpec((tm, TOP_K), lambda i: (i, 0)),
                  pl.BlockSpec((1, d), lambda i: (0, 0)),
                  pl.BlockSpec(memory_space=pl.ANY)],
        out_specs=pl.BlockSpec((tm, d), lambda i: (i, 0)),
        out_shape=jax.ShapeDtypeStruct((t, d), F32),
        scratch_shapes=[pltpu.VMEM((2, TOP_K, tm * PACKED_TILE_ROWS, LANES), U32),
                        pltpu.SemaphoreType.DMA((2,))],
        compiler_params=_cparams(1),
        name="moe_combine",
    )(dest, dest, x2d, gates_tok, g_final.reshape(1, d), y)


def _swa_q_order():
    heads = []
    for kv_pair in range(A_KV_HEADS // 2):
        for g in range(A_GROUP):
            heads += [(2 * kv_pair) * A_GROUP + g, (2 * kv_pair + 1) * A_GROUP + g]
    return jnp.asarray([h * HEAD_DIM + c for h in heads for c in range(HEAD_DIM)], I32)


def kernel(x, mem, rel_bias, g_mix, w_in, sinks_a, w_a_out, w_b_out, w_gate, b_gate, w_o,
           g_x, g_mem, w_xq, w_xk, w_xv, w_xo, g_moe, w_rg, w_re, w1, w3, w2, g_final):
    b, s, d = x.shape
    t = b * s
    depth = g_mix.shape[0]
    perm = _swa_q_order()
    bias_a = _band_bias(rel_bias[:, :A_Q_HEADS], 1, A_WINDOW - 1)
    bias_b = [_band_bias(rel_bias[:, A_Q_HEADS:], r, w // r) for w, r in B_PATTERNS]
    n_assign = t * TOP_K
    n_rows = -(-(n_assign + N_EXPERTS * (MOE_ROWS - 1)) // MOE_ROWS) * MOE_ROWS
    n_blocks = n_rows // MOE_ROWS

    x2d = x.reshape(t, d)
    for l in range(depth):
        w_in_l = jnp.concatenate([w_in[l][:, :A_WIDTH][:, perm], w_in[l][:, A_WIDTH:]], axis=1).astype(BF16)
        qkv_a, qkv_b, q4, k4, v4, q16, k16, v16 = _in_proj(x2d, g_mix[l], w_in_l, b, s, tm=512)
        gates = _gate_proj(x2d, g_mix[l], w_gate[l].astype(BF16), b_gate[l], tm=512, tn=1024)
        oa = _swa_attention(qkv_a.reshape(b, s, 2 * B_WIDTH), bias_a, sinks_a[l]).reshape(t, A_WIDTH)
        qkv_b4 = qkv_b.reshape(b, 1, s, 3 * B_WIDTH)
        o1, l1 = _dilated_attention(qkv_b4, qkv_b4, qkv_b4, (0, 1, 2), bias_b[0])
        o4, l4 = _dilated_attention(q4, k4, v4, (0, 0, 0), bias_b[1])
        o16, l16 = _dilated_attention(q16, k16, v16, (0, 0, 0), bias_b[2])
        x2d = _mix_out(x2d, oa, o1.reshape(t, B_WIDTH), l1.reshape(t, LANES), o4, l4, o16, l16, gates,
                       w_a_out[l][perm].astype(BF16), w_b_out[l].astype(BF16), w_o[l].astype(BF16),
                       s, tm=256)
        k_mem, v_mem = _mem_kv(mem, g_mem[l], w_xk[l].astype(BF16), w_xv[l].astype(BF16))
        x2d = _cross_attention(x2d.reshape(b, s, d), g_x[l], w_xq[l].astype(BF16), k_mem, v_mem,
                               w_xo[l].astype(BF16), tm=512).reshape(t, d)
        w_router = jnp.concatenate([w_re[l], w_rg[l],
                                    jnp.zeros((d, LANES - N_EXPERTS - N_GROUPS), F32)], axis=1)
        hp, e_t, gate_t, cnt = _router(x2d, g_moe[l], w_router, tm=512)
        counts = cnt[:, 0].astype(I32)
        padded = (counts + MOE_ROWS - 1) // MOE_ROWS * MOE_ROWS
        seg_end = jnp.cumsum(padded)
        seg_start = seg_end - padded
        n_used = (seg_end[-1] // MOE_ROWS).astype(I32).reshape(1)
        block_row = jnp.arange(n_blocks, dtype=I32) * MOE_ROWS
        block_e = jnp.minimum(jnp.sum(seg_end[None, :] <= block_row[:, None], axis=1),
                              N_EXPERTS - 1).astype(I32)
        start = jnp.broadcast_to(seg_start.astype(F32)[:, None], (N_EXPERTS, 128))
        dest = _assignment_rows(e_t, start, tm=512)
        xg = _dispatch(dest, hp, n_rows, tm=128)
        experts = jnp.arange(N_EXPERTS, dtype=I32)
        owns = padded > 0
        later = (experts[None, :] > experts[:, None]) & owns[None, :]
        next_owner = jnp.min(jnp.where(later, experts[None, :], N_EXPERTS), axis=1)
        next_owner = jnp.where(next_owner == N_EXPERTS, experts, next_owner).astype(I32)
        run_parity = ((jnp.cumsum(owns.astype(I32)) - 1) % 2).astype(I32)
        y = _experts(block_e, n_used, next_owner[block_e], run_parity[block_e], xg, w1[l], w3[l], w2[l])
        x2d = _combine(dest, x2d, gate_t.T, g_final, y, final=(l + 1 == depth), tm=128)
    return x2d.reshape(b, s, d)
```

```python
import functools
import math

import jax
import jax.numpy as jnp
from jax import lax
from jax.experimental import pallas as pl
from jax.experimental.pallas import tpu as pltpu

F32 = jnp.float32
BF16 = jnp.bfloat16
I32 = jnp.int32
U32 = jnp.uint32

HEAD_DIM = 64
PAIR = 2 * HEAD_DIM
A_Q_HEADS = 16
A_KV_HEADS = 4
A_GROUP = A_Q_HEADS // A_KV_HEADS
A_WIDTH = A_Q_HEADS * HEAD_DIM
A_KV_WIDTH = A_KV_HEADS * HEAD_DIM
A_WINDOW = 128
B_HEADS = 12
B_WIDTH = B_HEADS * HEAD_DIM
LANES = 128
LANE_CHUNKS = B_WIDTH // LANES
D_MODEL = 2048
PACKED_TILE_ROWS = D_MODEL // 2 // LANES
B_PATTERNS = ((128, 1), (512, 4), (2048, 16))
BLOCK = 128
Q_BLOCKS = 4
IN_COLS = A_WIDTH + 2 * A_KV_WIDTH + 3 * B_WIDTH
REL_BUCKETS = 32
REL_MAX_DIST = 2048
X_HEADS = 4
X_HEAD_DIM = 128
X_WIDTH = X_HEADS * X_HEAD_DIM
N_GROUPS = 4
EXP_PER_GROUP = 8
N_EXPERTS = N_GROUPS * EXP_PER_GROUP
TOP_K = 2
D_FF = 512
EPS = 1e-6
NEG = -1e30
MOE_ROWS = 256
VMEM_LIMIT = 56 * 1024 * 1024


def _cparams(n_axes):
    return pltpu.CompilerParams(dimension_semantics=("arbitrary",) * n_axes,
                                vmem_limit_bytes=VMEM_LIMIT)


def _rms(xf, g):
    return xf * lax.rsqrt(jnp.mean(xf * xf, axis=-1, keepdims=True) + EPS) * g


def _dot_nt(a, b):
    return lax.dot_general(a, b, (((1,), (1,)), ((), ())), preferred_element_type=F32)


def _load_resident(w_hbm_ref, w_ref, sem):
    @pl.when(pl.program_id(0) == 0)
    def _():
        copy = pltpu.make_async_copy(w_hbm_ref, w_ref, sem)
        copy.start()
        copy.wait()


def _in_proj_kernel(x_ref, g_ref, w_hbm_ref, oa_ref, ob_ref, q4_ref, k4_ref, v4_ref, q16_ref, k16_ref,
                    v16_ref, w_ref, acc_ref, mod4_ref, sem):
    _load_resident(w_hbm_ref, w_ref, sem)
    tm = x_ref.shape[0]
    tn = B_WIDTH
    n4 = tm // 4
    h = _rms(x_ref[...], g_ref[...]).astype(BF16)
    for j in range(2):
        oa_ref[:, j * tn:(j + 1) * tn] = jnp.dot(
            h, w_ref[:, j * tn:(j + 1) * tn], preferred_element_type=F32).astype(BF16)
    for part, (r4_ref, r16_ref) in enumerate(((q4_ref, q16_ref), (k4_ref, k16_ref), (v4_ref, v16_ref))):
        acc = jnp.dot(h, w_ref[:, (2 + part) * tn:(3 + part) * tn], preferred_element_type=F32)
        ob_ref[:, part * tn:(part + 1) * tn] = acc.astype(BF16)
        for cc in range(LANE_CHUNKS):
            acc_ref[part, cc] = acc[:, cc * LANES:(cc + 1) * LANES]
        for c in range(4):
            rows = [acc_ref[part, cc, pl.ds(c, n4, stride=4), :] for cc in range(LANE_CHUNKS)]
            r4_ref[0, c] = jnp.concatenate(rows, axis=1).astype(BF16)
            for cc in range(LANE_CHUNKS):
                mod4_ref[part, cc, c * n4:(c + 1) * n4, :] = rows[cc]
        for c in range(16):
            rows = [mod4_ref[part, cc, pl.ds((c % 4) * n4 + c // 4, tm // 16, stride=4), :]
                    for cc in range(LANE_CHUNKS)]
            r16_ref[0, c] = jnp.concatenate(rows, axis=1).astype(BF16)


def _in_proj(x2d, g, w, b, s, *, tm):
    t, d = x2d.shape
    tn = B_WIDTH
    tiles = s // tm
    res_shape = lambda r: jax.ShapeDtypeStruct((b, r, s // r, B_WIDTH), BF16)
    res_spec = lambda r: pl.BlockSpec((1, r, tm // r, B_WIDTH), lambda i: (i // tiles, 0, i % tiles, 0))
    return pl.pallas_call(
        _in_proj_kernel,
        grid=(t // tm,),
        in_specs=[
            pl.BlockSpec((tm, d), lambda i: (i, 0)),
            pl.BlockSpec((1, d), lambda i: (0, 0)),
            pl.BlockSpec(memory_space=pl.ANY),
        ],
        out_specs=[pl.BlockSpec((tm, 2 * tn), lambda i: (i, 0)), pl.BlockSpec((tm, 3 * tn), lambda i: (i, 0))]
                  + [res_spec(4)] * 3 + [res_spec(16)] * 3,
        out_shape=[jax.ShapeDtypeStruct((t, 2 * tn), BF16), jax.ShapeDtypeStruct((t, 3 * tn), BF16)]
                  + [res_shape(4)] * 3 + [res_shape(16)] * 3,
        scratch_shapes=[pltpu.VMEM(w.shape, BF16), pltpu.VMEM((3, LANE_CHUNKS, tm, LANES), F32),
                        pltpu.VMEM((3, LANE_CHUNKS, tm, LANES), F32), pltpu.SemaphoreType.DMA(())],
        compiler_params=_cparams(1),
        name="in_proj",
    )(x2d, g.reshape(1, d), w)


def _gate_proj_kernel(x_ref, g_ref, w_hbm_ref, b_ref, o_ref, w_ref, sem, *, tn):
    _load_resident(w_hbm_ref, w_ref, sem)
    h = _rms(x_ref[...], g_ref[...]).astype(BF16)
    for j in range(w_ref.shape[1] // tn):
        cols = slice(j * tn, (j + 1) * tn)
        acc = jnp.dot(h, w_ref[:, cols], preferred_element_type=F32)
        o_ref[:, cols] = jax.nn.sigmoid(acc + b_ref[:, cols]).astype(o_ref.dtype)


def _gate_proj(x2d, g, w, b, *, tm, tn):
    t, d = x2d.shape
    n = w.shape[1]
    return pl.pallas_call(
        functools.partial(_gate_proj_kernel, tn=tn),
        grid=(t // tm,),
        in_specs=[
            pl.BlockSpec((tm, d), lambda i: (i, 0)),
            pl.BlockSpec((1, d), lambda i: (0, 0)),
            pl.BlockSpec(memory_space=pl.ANY),
            pl.BlockSpec((1, n), lambda i: (0, 0)),
        ],
        out_specs=pl.BlockSpec((tm, n), lambda i: (i, 0)),
        out_shape=jax.ShapeDtypeStruct((t, n), BF16),
        scratch_shapes=[pltpu.VMEM(w.shape, BF16), pltpu.SemaphoreType.DMA(())],
        compiler_params=_cparams(1),
        name="gate_proj",
    )(x2d, g.reshape(1, d), w, b.reshape(1, n))


def t5_bucket(dist):
    max_exact = REL_BUCKETS // 2
    d = jnp.maximum(dist, 0)
    df = jnp.maximum(d, 1).astype(jnp.float32)
    large = max_exact + (jnp.log(df / max_exact) / math.log(REL_MAX_DIST / max_exact)
                         * (REL_BUCKETS - max_exact)).astype(jnp.int32)
    large = jnp.minimum(large, REL_BUCKETS - 1)
    return jnp.where(d < max_exact, d, large)


BIAS_HEADS = 64
BIAS_A_BLOCK = 3


def _band_bias_table(rel_bias):
    span = 3 * BLOCK
    dist = 2 * BLOCK - 1 - jnp.arange(span)

    def per_distance(table, step, max_off):
        z = jnp.where(((dist >= 0) & (dist <= max_off))[:, None], table[t5_bucket(dist * step)], NEG)
        return z.T.astype(F32)

    zs = [per_distance(rel_bias[:, A_Q_HEADS:], r, w // r) for w, r in B_PATTERNS]
    zs.append(jnp.full((BIAS_HEADS - len(B_PATTERNS) * B_HEADS - A_Q_HEADS, span), NEG, F32))
    zs.append(per_distance(rel_bias[:, :A_Q_HEADS], 1, A_WINDOW - 1))
    z = jnp.concatenate(zs, axis=0)
    rows = jnp.tile(z, (1, BLOCK))[:, :BLOCK * (span - 1)].reshape(BIAS_HEADS, BLOCK, span - 1)
    rest = rows[:, :, BLOCK - 1:3 * BLOCK - 1]
    first = jnp.where(jnp.arange(2 * BLOCK)[None, None, :] >= BLOCK, rest, NEG)
    return jnp.stack([first, rest])


def _softmax_pv(s, v_pair, sink):
    m = jnp.max(s, axis=-1, keepdims=True)
    if sink is not None:
        m = jnp.maximum(m, sink)
    p = jnp.exp(s - m)
    l = jnp.sum(p, axis=-1, keepdims=True)
    if sink is not None:
        l = l + jnp.exp(sink - m)
    o = jnp.dot(p.astype(BF16), v_pair, preferred_element_type=F32)
    return o, m, l


def _swa_kernel(sink_ref, q_ref, kp_ref, kc_ref, vp_ref, vc_ref, bias_ref, o_ref):
    first = jnp.minimum(pl.program_id(1), 1)
    q_all = q_ref[0] * jnp.asarray(HEAD_DIM ** -0.5, BF16)
    k_all = jnp.concatenate([kp_ref[0], kc_ref[0]], axis=0)
    v_all = jnp.concatenate([vp_ref[0], vc_ref[0]], axis=0)
    lane = lax.broadcasted_iota(I32, (1, PAIR), 1)
    low = lane < HEAD_DIM
    for blk in range(Q_BLOCKS):
        rows = slice(blk * BLOCK, (blk + 1) * BLOCK)
        q = q_all[rows]
        k = k_all[blk * BLOCK:(blk + 2) * BLOCK]
        v = v_all[blk * BLOCK:(blk + 2) * BLOCK]
        variant = first if blk == 0 else 1
        for kv_pair in range(A_KV_HEADS // 2):
            k_pair = k[:, kv_pair * PAIR:(kv_pair + 1) * PAIR]
            v_pair = v[:, kv_pair * PAIR:(kv_pair + 1) * PAIR]
            k_half = (jnp.where(low, k_pair, jnp.zeros_like(k_pair)),
                      jnp.where(low, jnp.zeros_like(k_pair), k_pair))
            for g in range(A_GROUP):
                col = (kv_pair * A_GROUP + g) * PAIR
                q_pair = q[:, col:col + PAIR]
                outs = []
                for half in range(2):
                    head = (2 * kv_pair + half) * A_GROUP + g
                    s = _dot_nt(q_pair, k_half[half]) + bias_ref[variant, head]
                    o, _, l = _softmax_pv(s, v_pair, sink_ref[head])
                    outs.append(o * (1.0 / l))
                o_ref[0, rows, col:col + PAIR] = jnp.where(low, outs[0], outs[1]).astype(o_ref.dtype)


def _swa_attention(qkv3, bias, sinks):
    b, s, _ = qkv3.shape
    step = Q_BLOCKS * BLOCK
    kcol = A_WIDTH // A_KV_WIDTH
    prev = lambda i: jnp.maximum(i * Q_BLOCKS - 1, 0)
    return pl.pallas_call(
        _swa_kernel,
        grid=(b, s // step),
        in_specs=[
            pl.BlockSpec(memory_space=pltpu.SMEM),
            pl.BlockSpec((1, step, A_WIDTH), lambda bi, i: (bi, i, 0)),
            pl.BlockSpec((1, BLOCK, A_KV_WIDTH), lambda bi, i: (bi, prev(i), kcol)),
            pl.BlockSpec((1, step, A_KV_WIDTH), lambda bi, i: (bi, i, kcol)),
            pl.BlockSpec((1, BLOCK, A_KV_WIDTH), lambda bi, i: (bi, prev(i), kcol + 1)),
            pl.BlockSpec((1, step, A_KV_WIDTH), lambda bi, i: (bi, i, kcol + 1)),
            pl.BlockSpec((2, A_Q_HEADS, BLOCK, 2 * BLOCK), lambda bi, i: (0, BIAS_A_BLOCK, 0, 0)),
        ],
        out_specs=pl.BlockSpec((1, step, A_WIDTH), lambda bi, i: (bi, i, 0)),
        out_shape=jax.ShapeDtypeStruct((b, s, A_WIDTH), BF16),
        compiler_params=_cparams(2),
        name="swa_attention",
    )(sinks, qkv3, qkv3, qkv3, qkv3, qkv3, bias)


def _dilated_kernel(q_ref, kp_ref, kc_ref, vp_ref, vc_ref, bias_ref, o_ref, lse_ref):
    first = jnp.minimum(pl.program_id(2), 1)
    q_all = q_ref[...] * jnp.asarray(HEAD_DIM ** -0.5, BF16)
    k_all = jnp.concatenate([kp_ref[...], kc_ref[...]], axis=0)
    v_all = jnp.concatenate([vp_ref[...], vc_ref[...]], axis=0)
    lane = lax.broadcasted_iota(I32, (1, PAIR), 1)
    low = lane < HEAD_DIM
    for blk in range(Q_BLOCKS):
        rows = slice(blk * BLOCK, (blk + 1) * BLOCK)
        variant = first if blk == 0 else 1
        lse_tile = jnp.zeros((BLOCK, LANES), F32)
        for pair in range(B_HEADS // 2):
            col = pair * PAIR
            q_pair = q_all[rows, col:col + PAIR]
            k_pair = k_all[blk * BLOCK:(blk + 2) * BLOCK, col:col + PAIR]
            v_pair = v_all[blk * BLOCK:(blk + 2) * BLOCK, col:col + PAIR]
            k_half = (jnp.where(low, k_pair, jnp.zeros_like(k_pair)),
                      jnp.where(low, jnp.zeros_like(k_pair), k_pair))
            outs = []
            for half in range(2):
                head = 2 * pair + half
                s = _dot_nt(q_pair, k_half[half]) + bias_ref[variant, head]
                o, m, l = _softmax_pv(s, v_pair, None)
                outs.append(o * (1.0 / l))
                lse_tile = jnp.where(lane == head, m + jnp.log(l), lse_tile)
            o_ref[rows, col:col + PAIR] = jnp.where(low, outs[0], outs[1]).astype(o_ref.dtype)
        lse_ref[rows, :] = lse_tile


def _dilated_attention(q, k, v, cols, bias, pattern):
    b, r, n, _ = q.shape
    step = Q_BLOCKS * BLOCK
    assert n % step == 0, (n, step)
    qc, kc, vc = cols
    prev = lambda j: jnp.maximum(j * Q_BLOCKS - 1, 0)
    cur = lambda col, w=B_WIDTH: pl.BlockSpec((None, None, step, w), lambda bi, c, j: (bi, c, j, col))
    old = lambda col: pl.BlockSpec((None, None, BLOCK, B_WIDTH), lambda bi, c, j: (bi, c, prev(j), col))
    return pl.pallas_call(
        _dilated_kernel,
        grid=(b, r, n // step),
        in_specs=[cur(qc), old(kc), cur(kc), old(vc), cur(vc),
                  pl.BlockSpec((2, B_HEADS, BLOCK, 2 * BLOCK), lambda bi, c, j: (0, pattern, 0, 0))],
        out_specs=[cur(0), cur(0, LANES)],
        out_shape=[jax.ShapeDtypeStruct((b, r, n, B_WIDTH), BF16),
                   jax.ShapeDtypeStruct((b, r, n, LANES), F32)],
        compiler_params=_cparams(3),
        name=f"dilated_attention_r{r}",
    )(q, k, k, v, v, bias)


def _mix_out_kernel(x_ref, oa_ref, o1_ref, l1_ref, o4_ref, l4_ref, o16_ref, l16_ref, ga_ref, gb_ref,
                    wa_ref, wb_ref, wo_ref, ex_ref, out_ref, seq_ref):
    tm = x_ref.shape[0]
    seq = []
    base = 0
    for r, ref in ((4, o4_ref), (4, l4_ref), (16, o16_ref), (16, l16_ref)):
        chunks = ref.shape[-1] // LANES
        for c in range(r):
            rows = ref[0, c].astype(F32)
            for cc in range(chunks):
                seq_ref[base + cc, pl.ds(c, tm // r, stride=r), :] = rows[:, cc * LANES:(cc + 1) * LANES]
        seq.append(jnp.concatenate([seq_ref[base + cc] for cc in range(chunks)], axis=1))
        base += chunks
    o2, l2, o3, l3 = seq
    l1 = l1_ref[...]
    m = jnp.maximum(jnp.maximum(l1, l2), l3)
    e1, e2, e3 = jnp.exp(l1 - m), jnp.exp(l2 - m), jnp.exp(l3 - m)
    inv = 1.0 / (e1 + e2 + e3)

    def widen(w):
        hi = w.astype(BF16)
        lo = (w - hi.astype(F32)).astype(BF16)
        return (jnp.dot(hi, ex_ref[...], preferred_element_type=F32)
                + jnp.dot(lo, ex_ref[...], preferred_element_type=F32))

    ob = (widen(e1 * inv) * o1_ref[...].astype(F32) + widen(e2 * inv) * o2 + widen(e3 * inv) * o3)
    ya = jnp.dot(oa_ref[...], wa_ref[...], preferred_element_type=F32)
    yb = jnp.dot(ob.astype(BF16), wb_ref[...], preferred_element_type=F32)
    mixed = ga_ref[...].astype(F32) * ya + gb_ref[...].astype(F32) * yb
    out_ref[...] = x_ref[...] + jnp.dot(mixed.astype(BF16), wo_ref[...], preferred_element_type=F32)


def _mix_out(x2d, oa, o1, l1, o4, l4, o16, l16, gates, wa, wb, wo, s, *, tm):
    t, d = x2d.shape
    tiles = s // tm
    row = lambda w: pl.BlockSpec((tm, w), lambda i: (i, 0))
    res = lambda r, w: pl.BlockSpec((1, r, tm // r, w), lambda i: (i // tiles, 0, i % tiles, 0))
    const = lambda shape: pl.BlockSpec(shape, lambda i: (0, 0))
    head_of_col = jnp.arange(B_WIDTH, dtype=I32)[None, :] // HEAD_DIM
    expand = (jnp.arange(LANES, dtype=I32)[:, None] == head_of_col).astype(BF16)
    return pl.pallas_call(
        _mix_out_kernel,
        grid=(t // tm,),
        in_specs=[row(d), row(A_WIDTH), row(B_WIDTH), row(LANES), res(4, B_WIDTH), res(4, LANES),
                  res(16, B_WIDTH), res(16, LANES),
                  pl.BlockSpec((tm, d), lambda i: (i, 0)), pl.BlockSpec((tm, d), lambda i: (i, 1)),
                  const(wa.shape), const(wb.shape), const(wo.shape), const(expand.shape)],
        out_specs=row(d),
        out_shape=jax.ShapeDtypeStruct((t, d), F32),
        scratch_shapes=[pltpu.VMEM((2 * (LANE_CHUNKS + 1), tm, LANES), F32)],
        compiler_params=_cparams(1),
        name="mix_out",
    )(x2d, oa, o1, l1, o4, l4, o16, l16, gates, gates, wa, wb, wo, expand)


def _mem_kv_kernel(mem_ref, g_ref, wk_ref, wv_ref, k_ref, v_ref):
    hn = _rms(mem_ref[0], g_ref[...]).astype(BF16)
    k_ref[0] = jnp.dot(hn, wk_ref[...], preferred_element_type=F32).astype(BF16)
    v_ref[0] = jnp.dot(hn, wv_ref[...], preferred_element_type=F32).astype(BF16)


def _mem_kv(mem, g, wk, wv):
    b, ml, d = mem.shape
    const = lambda shape: pl.BlockSpec(shape, lambda bi: (0,) * len(shape))
    out = jax.ShapeDtypeStruct((b, ml, X_WIDTH), BF16)
    blk = pl.BlockSpec((1, ml, X_WIDTH), lambda bi: (bi, 0, 0))
    return pl.pallas_call(
        _mem_kv_kernel,
        grid=(b,),
        in_specs=[pl.BlockSpec((1, ml, d), lambda bi: (bi, 0, 0)), const((1, d)),
                  const(wk.shape), const(wv.shape)],
        out_specs=[blk, blk],
        out_shape=[out, out],
        compiler_params=_cparams(1),
        name="mem_kv",
    )(mem, g.reshape(1, d), wk, wv)


def _cross_kernel(x_ref, g_ref, wq_ref, k_ref, v_ref, wo_ref, out_ref):
    x = x_ref[0]
    h = _rms(x, g_ref[...]).astype(BF16)
    q = (jnp.dot(h, wq_ref[...], preferred_element_type=F32) * (X_HEAD_DIM ** -0.5)).astype(BF16)
    k, v = k_ref[0], v_ref[0]
    outs = []
    for hd in range(X_HEADS):
        sl = slice(hd * X_HEAD_DIM, (hd + 1) * X_HEAD_DIM)
        s = _dot_nt(q[:, sl], k[:, sl])
        m = jnp.max(s, axis=-1, keepdims=True)
        p = jnp.exp(s - m)
        l = jnp.sum(p, axis=-1, keepdims=True)
        o = jnp.dot(p.astype(BF16), v[:, sl], preferred_element_type=F32)
        outs.append((o * (1.0 / l)).astype(BF16))
    o = jnp.concatenate(outs, axis=1)
    out_ref[0] = x + jnp.dot(o, wo_ref[...], preferred_element_type=F32)


def _cross_attention(x3, g, wq, k, v, wo, *, tm):
    b, s, d = x3.shape
    ml = k.shape[1]
    const = lambda shape: pl.BlockSpec(shape, lambda bi, i: (0,) * len(shape))
    return pl.pallas_call(
        _cross_kernel,
        grid=(b, s // tm),
        in_specs=[pl.BlockSpec((1, tm, d), lambda bi, i: (bi, i, 0)), const((1, d)), const(wq.shape),
                  pl.BlockSpec((1, ml, X_WIDTH), lambda bi, i: (bi, 0, 0)),
                  pl.BlockSpec((1, ml, X_WIDTH), lambda bi, i: (bi, 0, 0)),
                  const(wo.shape)],
        out_specs=pl.BlockSpec((1, tm, d), lambda bi, i: (bi, i, 0)),
        out_shape=jax.ShapeDtypeStruct((b, s, d), F32),
        compiler_params=_cparams(2),
        name="cross_attention",
    )(x3, g.reshape(1, d), wq, k, v, wo)


def _pack_bf16_pairs(x):
    n = x.shape[1] // 2
    bits = lax.bitcast_convert_type(x.astype(BF16).astype(F32), U32)
    return (bits[:, :n] >> 16) | (bits[:, n:] & jnp.uint32(0xFFFF0000))


def _unpack_bf16_pairs(p):
    return (lax.bitcast_convert_type(p << 16, F32),
            lax.bitcast_convert_type(p & jnp.uint32(0xFFFF0000), F32))


def _first_argmax(vals, rows, n):
    m = jnp.max(vals, axis=0, keepdims=True)
    idx = jnp.min(jnp.where(vals == m, rows, n), axis=0, keepdims=True)
    return m, idx


def _router_kernel(x_ref, g_ref, whi_ref, wlo_ref, hp_ref, e_ref, gate_ref, cnt_ref):
    h = _rms(x_ref[...], g_ref[...])
    tm, d = h.shape
    h_hi = h.astype(BF16)
    h_hi32 = h_hi.astype(F32)
    h_lo = (h - h_hi32).astype(BF16)
    logits = (jnp.dot(h_hi, whi_ref[...], preferred_element_type=F32)
              + jnp.dot(h_lo, whi_ref[...], preferred_element_type=F32)
              + jnp.dot(h_hi, wlo_ref[...], preferred_element_type=F32)).T
    rows8 = lax.broadcasted_iota(I32, (EXP_PER_GROUP, tm), 0)
    gl = jnp.where(rows8 < N_GROUPS, logits[N_EXPERTS:N_EXPERTS + 8], -jnp.inf)
    gmax, gidx = _first_argmax(gl, rows8, 8)
    g_gate = 1.0 / jnp.sum(jnp.exp(gl - gmax), axis=0, keepdims=True)
    sel = jnp.zeros((EXP_PER_GROUP, tm), F32)
    for grp in range(N_GROUPS):
        sel = jnp.where(gidx == grp, logits[grp * EXP_PER_GROUP:(grp + 1) * EXP_PER_GROUP], sel)
    v1, i1 = _first_argmax(sel, rows8, 8)
    sel2 = jnp.where(rows8 == i1, -jnp.inf, sel)
    v2, i2 = _first_argmax(sel2, rows8, 8)
    e2x = jnp.exp(v2 - v1)
    den = 1.0 + e2x
    e_ref[...] = jnp.concatenate([gidx * EXP_PER_GROUP + i1, gidx * EXP_PER_GROUP + i2], axis=0)
    gate_ref[...] = jnp.concatenate([(1.0 / den) * g_gate, (e2x / den) * g_gate], axis=0)

    bits = lax.bitcast_convert_type(h_hi32, U32)
    packed = (bits[:, :d // 2] >> 16) | (bits[:, d // 2:] & jnp.uint32(0xFFFF0000))
    for c in range(PACKED_TILE_ROWS):
        hp_ref[pl.ds(c, tm, stride=PACKED_TILE_ROWS), :] = packed[:, c * LANES:(c + 1) * LANES]

    @pl.when(pl.program_id(0) == 0)
    def _():
        cnt_ref[...] = jnp.zeros_like(cnt_ref)

    rows32 = lax.broadcasted_iota(I32, (N_EXPERTS, tm), 0)
    e = e_ref[...]
    hits = (rows32 == e[0:1]).astype(F32) + (rows32 == e[1:2]).astype(F32)
    cnt_ref[...] += jnp.sum(hits, axis=1, keepdims=True)


def _router(x2d, g, w_router, *, tm):
    t, d = x2d.shape
    w_hi = w_router.astype(BF16)
    w_lo = (w_router - w_hi.astype(F32)).astype(BF16)
    return pl.pallas_call(
        _router_kernel,
        grid=(t // tm,),
        in_specs=[pl.BlockSpec((tm, d), lambda i: (i, 0)),
                  pl.BlockSpec((1, d), lambda i: (0, 0)),
                  pl.BlockSpec((d, LANES), lambda i: (0, 0)),
                  pl.BlockSpec((d, LANES), lambda i: (0, 0))],
        out_specs=[pl.BlockSpec((tm * PACKED_TILE_ROWS, LANES), lambda i: (i, 0)),
                   pl.BlockSpec((TOP_K, tm), lambda i: (0, i)),
                   pl.BlockSpec((TOP_K, tm), lambda i: (0, i)),
                   pl.BlockSpec((N_EXPERTS, 128), lambda i: (0, 0))],
        out_shape=[jax.ShapeDtypeStruct((t * PACKED_TILE_ROWS, LANES), U32),
                   jax.ShapeDtypeStruct((TOP_K, t), I32),
                   jax.ShapeDtypeStruct((TOP_K, t), F32),
                   jax.ShapeDtypeStruct((N_EXPERTS, 128), F32)],
        compiler_params=_cparams(1),
        name="moe_router",
    )(x2d, g.reshape(1, d), w_hi, w_lo)


def _dest_kernel(e_ref, start_ref, d_ref, carry_ref):
    @pl.when(pl.program_id(0) == 0)
    def _():
        carry_ref[...] = jnp.zeros_like(carry_ref)

    e = e_ref[...]
    tm = e.shape[1]
    rows32 = lax.broadcasted_iota(I32, (N_EXPERTS, tm), 0)
    oh0 = (rows32 == e[0:1]).astype(F32)
    oh1 = (rows32 == e[1:2]).astype(F32)
    hits = oh0 + oh1
    earlier = (lax.broadcasted_iota(I32, (tm, tm), 0) < lax.broadcasted_iota(I32, (tm, tm), 1))
    prefix = jnp.dot(hits.astype(BF16), earlier.astype(BF16), preferred_element_type=F32)
    base = prefix + carry_ref[:, 0:1] + start_ref[:, 0:1]
    d0 = jnp.sum(oh0 * base, axis=0, keepdims=True)
    d1 = jnp.sum(oh1 * base, axis=0, keepdims=True)
    d_ref[...] = jnp.concatenate([d0, d1], axis=0).astype(I32)
    carry_ref[...] += jnp.sum(hits, axis=1, keepdims=True)


def _assignment_rows(e_t, start, *, tm):
    t = e_t.shape[1]
    return pl.pallas_call(
        _dest_kernel,
        grid=(t // tm,),
        in_specs=[pl.BlockSpec((TOP_K, tm), lambda i: (0, i)),
                  pl.BlockSpec((N_EXPERTS, 128), lambda i: (0, 0))],
        out_specs=pl.BlockSpec((TOP_K, tm), lambda i: (0, i)),
        out_shape=jax.ShapeDtypeStruct((TOP_K, t), I32),
        scratch_shapes=[pltpu.VMEM((N_EXPERTS, 128), F32)],
        compiler_params=_cparams(1),
        name="moe_assignment_rows",
    )(e_t, start)


def _token_rows(ref, token, rows):
    return ref.at[pl.ds(pl.multiple_of(token * rows, rows), rows)]


def _dispatch_kernel(end_ref, pad_ref, nu_ref, d_ref, hp_ref, xg_ref, zero_ref, sem, zero_sem):
    rows = PACKED_TILE_ROWS
    tm = hp_ref.shape[0] // rows
    block_rows = MOE_ROWS * rows

    @pl.when(pl.program_id(0) == 0)
    def _():
        zero_ref[...] = jnp.zeros_like(zero_ref)

        def zero_block(block):
            start = pl.multiple_of(block * block_rows, block_rows)
            return pltpu.make_async_copy(zero_ref, xg_ref.at[pl.ds(start, block_rows)], zero_sem)

        for phase in ("start", "wait"):
            for e in range(N_EXPERTS):
                @pl.when(pad_ref[e] > 0)
                def _():
                    copy = zero_block(end_ref[e] // MOE_ROWS - 1)
                    copy.start() if phase == "start" else copy.wait()

        def start_tail(block, carry):
            zero_block(block).start()
            return carry

        def wait_tail(block, carry):
            zero_block(block).wait()
            return carry

        n_blocks = xg_ref.shape[0] // block_rows
        lax.fori_loop(nu_ref[0], n_blocks, start_tail, 0)
        lax.fori_loop(nu_ref[0], n_blocks, wait_tail, 0)

    for t in range(tm):
        for k in range(TOP_K):
            pltpu.make_async_copy(_token_rows(hp_ref, t, rows), _token_rows(xg_ref, d_ref[k, t], rows),
                                  sem).start(priority=k)
    for k in range(TOP_K):
        pltpu.make_async_copy(hp_ref, xg_ref.at[pl.ds(0, tm * rows)], sem).wait()


def _dispatch(seg_end, padded, n_used, dest, hp, n_rows, *, tm):
    rows = PACKED_TILE_ROWS
    t = hp.shape[0] // rows
    return pl.pallas_call(
        _dispatch_kernel,
        grid_spec=pltpu.PrefetchScalarGridSpec(
            num_scalar_prefetch=3,
            grid=(t // tm,),
            in_specs=[pl.BlockSpec((TOP_K, tm), lambda i, *_: (0, i), memory_space=pltpu.SMEM),
                      pl.BlockSpec((tm * rows, LANES), lambda i, *_: (i, 0))],
            out_specs=pl.BlockSpec(memory_space=pl.ANY),
            scratch_shapes=[pltpu.VMEM((MOE_ROWS * rows, LANES), U32), pltpu.SemaphoreType.DMA(()),
                            pltpu.SemaphoreType.DMA(())],
        ),
        out_shape=jax.ShapeDtypeStruct((n_rows * rows, LANES), U32),
        compiler_params=_cparams(1),
        name="moe_dispatch",
    )(seg_end, padded, n_used, dest, hp)


def _expert_kernel(be_ref, nu_ref, nxt_ref, par_ref, xg_ref, w1_ref, w3_ref, w2_ref, y_ref,
                   f1_ref, f3_ref, f2_ref, w1b_ref, w3b_ref, w2b_ref, sem):
    i = pl.program_id(0)

    def weight_copies(e, slot):
        return [pltpu.make_async_copy(w_ref.at[e], f_ref.at[slot], sem.at[slot, n])
                for n, (w_ref, f_ref) in enumerate(((w1_ref, f1_ref), (w3_ref, f3_ref), (w2_ref, f2_ref)))]

    @pl.when(i < nu_ref[0])
    def _():
        e = be_ref[i]
        slot = par_ref[i]

        @pl.when((i == 0) | (e != be_ref[jnp.maximum(i - 1, 0)]))
        def _():
            @pl.when(i == 0)
            def _():
                for copy in weight_copies(e, slot):
                    copy.start()

            for copy in weight_copies(e, slot):
                copy.wait()
            w1b_ref[...] = f1_ref[slot].astype(BF16)
            w3b_ref[...] = f3_ref[slot].astype(BF16)
            w2b_ref[...] = f2_ref[slot].astype(BF16)

            @pl.when(nxt_ref[i] != e)
            def _():
                for copy in weight_copies(nxt_ref[i], 1 - slot):
                    copy.start()

        halves = [_unpack_bf16_pairs(xg_ref[pl.ds(c, MOE_ROWS, stride=PACKED_TILE_ROWS), :])
                  for c in range(PACKED_TILE_ROWS)]
        x = jnp.concatenate([lo for lo, _ in halves] + [hi for _, hi in halves], axis=1).astype(BF16)
        h1 = jnp.dot(x, w1b_ref[...], preferred_element_type=F32)
        h3 = jnp.dot(x, w3b_ref[...], preferred_element_type=F32)
        a = (jax.nn.silu(h1) * h3).astype(BF16)
        y = _pack_bf16_pairs(jnp.dot(a, w2b_ref[...], preferred_element_type=F32))
        for c in range(PACKED_TILE_ROWS):
            y_ref[pl.ds(c, MOE_ROWS, stride=PACKED_TILE_ROWS), :] = y[:, c * LANES:(c + 1) * LANES]

    @pl.when(i >= nu_ref[0])
    def _():
        y_ref[...] = jnp.zeros_like(y_ref)


def _experts(block_e, n_used, next_e, parity, xg, w1, w3, w2):
    rows = PACKED_TILE_ROWS
    n_rows = xg.shape[0] // rows
    _, d, dff = w1.shape
    nblk = n_rows // MOE_ROWS
    live = lambda i, nu: jnp.maximum(jnp.minimum(i, nu[0] - 1), 0)
    any_space = pl.BlockSpec(memory_space=pl.ANY)
    return pl.pallas_call(
        _expert_kernel,
        grid_spec=pltpu.PrefetchScalarGridSpec(
            num_scalar_prefetch=4,
            grid=(nblk,),
            in_specs=[pl.BlockSpec((MOE_ROWS * rows, LANES), lambda i, be, nu, nx, par: (live(i, nu), 0)),
                      any_space, any_space, any_space],
            out_specs=pl.BlockSpec((MOE_ROWS * rows, LANES), lambda i, be, nu, nx, par: (i, 0)),
            scratch_shapes=[pltpu.VMEM((2, d, dff), F32), pltpu.VMEM((2, d, dff), F32),
                            pltpu.VMEM((2, dff, d), F32), pltpu.VMEM((d, dff), BF16),
                            pltpu.VMEM((d, dff), BF16), pltpu.VMEM((dff, d), BF16),
                            pltpu.SemaphoreType.DMA((2, 3))],
        ),
        out_shape=jax.ShapeDtypeStruct((n_rows * rows, LANES), U32),
        compiler_params=_cparams(1),
        name="moe_experts",
    )(block_e, n_used, next_e, parity, xg, w1, w3, w2)


def _combine_kernel(dc_ref, dn_ref, x_ref, gate_ref, gf_ref, y_ref, out_ref, buf, sem, *, final):
    i = pl.program_id(0)
    n = pl.num_programs(0)
    tm = x_ref.shape[0]
    rows = PACKED_TILE_ROWS

    def issue(d_ref, slot):
        for t in range(tm):
            for k in range(TOP_K):
                pltpu.make_async_copy(_token_rows(y_ref, d_ref[k, t], rows),
                                      _token_rows(buf.at[slot, k], t, rows),
                                      sem.at[slot]).start(priority=k)

    @pl.when(i == 0)
    def _():
        issue(dc_ref, 0)

    @pl.when(i + 1 < n)
    def _():
        issue(dn_ref, (i + 1) % 2)

    slot = i % 2
    for k in range(TOP_K):
        pltpu.make_async_copy(y_ref.at[pl.ds(0, tm * rows)], buf.at[slot, k], sem.at[slot]).wait()
    g = gate_ref[...]
    g0, g1 = g[:, 0:1], g[:, 1:2]
    low, high = [], []
    for c in range(rows):
        lo0, hi0 = _unpack_bf16_pairs(buf[slot, 0, pl.ds(c, tm, stride=rows), :])
        lo1, hi1 = _unpack_bf16_pairs(buf[slot, 1, pl.ds(c, tm, stride=rows), :])
        low.append(x_ref[:, c * LANES:(c + 1) * LANES] + (g0 * lo0 + g1 * lo1))
        high.append(x_ref[:, (rows + c) * LANES:(rows + c + 1) * LANES] + (g0 * hi0 + g1 * hi1))
    y = jnp.concatenate(low + high, axis=1)
    out_ref[...] = _rms(y, gf_ref[...]) if final else y


def _combine(dest, x2d, gates_tok, g_final, y, *, final, tm):
    t, d = x2d.shape
    nt = t // tm
    return pl.pallas_call(
        functools.partial(_combine_kernel, final=final),
        grid=(nt,),
        in_specs=[pl.BlockSpec((TOP_K, tm), lambda i: (0, i), memory_space=pltpu.SMEM),
                  pl.BlockSpec((TOP_K, tm), lambda i: (0, jnp.minimum(i + 1, nt - 1)),
                               memory_space=pltpu.SMEM),
                  pl.BlockSpec((tm, d), lambda i: (i, 0)),
                  pl.BlockSpec((tm, TOP_K), lambda i: (i, 0)),
                  pl.BlockSpec((1, d), lambda i: (0, 0)),
                  pl.BlockSpec(memory_space=pl.ANY)],
        out_specs=pl.BlockSpec((tm, d), lambda i: (i, 0)),
        out_shape=jax.ShapeDtypeStruct((t, d), F32),
        scratch_shapes=[pltpu.VMEM((2, TOP_K, tm * PACKED_TILE_ROWS, LANES), U32),
                        pltpu.SemaphoreType.DMA((2,))],
        compiler_params=_cparams(1),
        name="moe_combine",
    )(dest, dest, x2d, gates_tok, g_final.reshape(1, d), y)


def _swa_q_order():
    heads = []
    for kv_pair in range(A_KV_HEADS // 2):
        for g in range(A_GROUP):
            heads += [(2 * kv_pair) * A_GROUP + g, (2 * kv_pair + 1) * A_GROUP + g]
    return jnp.asarray([h * HEAD_DIM + c for h in heads for c in range(HEAD_DIM)], I32)


def kernel(x, mem, rel_bias, g_mix, w_in, sinks_a, w_a_out, w_b_out, w_gate, b_gate, w_o,
           g_x, g_mem, w_xq, w_xk, w_xv, w_xo, g_moe, w_rg, w_re, w1, w3, w2, g_final):
    b, s, d = x.shape
    t = b * s
    depth = g_mix.shape[0]
    perm = _swa_q_order()
    bias = _band_bias_table(rel_bias)
    n_assign = t * TOP_K
    n_rows = -(-(n_assign + N_EXPERTS * (MOE_ROWS - 1)) // MOE_ROWS) * MOE_ROWS
    n_blocks = n_rows // MOE_ROWS

    x2d = x.reshape(t, d)
    for l in range(depth):
        w_in_l = jnp.concatenate([w_in[l][:, :A_WIDTH][:, perm], w_in[l][:, A_WIDTH:]], axis=1).astype(BF16)
        qkv_a, qkv_b, q4, k4, v4, q16, k16, v16 = _in_proj(x2d, g_mix[l], w_in_l, b, s, tm=512)
        gates = _gate_proj(x2d, g_mix[l], w_gate[l].astype(BF16), b_gate[l], tm=512, tn=1024)
        oa = _swa_attention(qkv_a.reshape(b, s, 2 * B_WIDTH), bias, sinks_a[l]).reshape(t, A_WIDTH)
        qkv_b4 = qkv_b.reshape(b, 1, s, 3 * B_WIDTH)
        o1, l1 = _dilated_attention(qkv_b4, qkv_b4, qkv_b4, (0, 1, 2), bias, 0)
        o4, l4 = _dilated_attention(q4, k4, v4, (0, 0, 0), bias, 1)
        o16, l16 = _dilated_attention(q16, k16, v16, (0, 0, 0), bias, 2)
        x2d = _mix_out(x2d, oa, o1.reshape(t, B_WIDTH), l1.reshape(t, LANES), o4, l4, o16, l16, gates,
                       w_a_out[l][perm].astype(BF16), w_b_out[l].astype(BF16), w_o[l].astype(BF16),
                       s, tm=256)
        k_mem, v_mem = _mem_kv(mem, g_mem[l], w_xk[l].astype(BF16), w_xv[l].astype(BF16))
        x2d = _cross_attention(x2d.reshape(b, s, d), g_x[l], w_xq[l].astype(BF16), k_mem, v_mem,
                               w_xo[l].astype(BF16), tm=512).reshape(t, d)
        w_router = jnp.concatenate([w_re[l], w_rg[l],
                                    jnp.zeros((d, LANES - N_EXPERTS - N_GROUPS), F32)], axis=1)
        hp, e_t, gate_t, cnt = _router(x2d, g_moe[l], w_router, tm=512)
        counts = cnt[:, 0].astype(I32)
        padded = (counts + MOE_ROWS - 1) // MOE_ROWS * MOE_ROWS
        seg_end = jnp.cumsum(padded)
        seg_start = seg_end - padded
        n_used = (seg_end[-1] // MOE_ROWS).astype(I32).reshape(1)
        block_row = jnp.arange(n_blocks, dtype=I32) * MOE_ROWS
        block_e = jnp.minimum(jnp.sum(seg_end[None, :] <= block_row[:, None], axis=1),
                              N_EXPERTS - 1).astype(I32)
        start = jnp.broadcast_to(seg_start.astype(F32)[:, None], (N_EXPERTS, 128))
        dest = _assignment_rows(e_t, start, tm=512)
        xg = _dispatch(seg_end.astype(I32), padded.astype(I32), n_used, dest, hp, n_rows, tm=128)
        experts = jnp.arange(N_EXPERTS, dtype=I32)
        owns = padded > 0
        later = (experts[None, :] > experts[:, None]) & owns[None, :]
        next_owner = jnp.min(jnp.where(later, experts[None, :], N_EXPERTS), axis=1)
        next_owner = jnp.where(next_owner == N_EXPERTS, experts, next_owner).astype(I32)
        run_parity = ((jnp.cumsum(owns.astype(I32)) - 1) % 2).astype(I32)
        y = _experts(block_e, n_used, next_owner[block_e], run_parity[block_e], xg, w1[l], w3[l], w2[l])
        x2d = _combine(dest, x2d, gate_t.T, g_final, y, final=(l + 1 == depth), tm=256)
    return x2d.reshape(b, s, d)
```

```python
import functools
import math

import jax
import jax.numpy as jnp
import numpy as np
from jax import lax
from jax.experimental import pallas as pl
from jax.experimental.pallas import tpu as pltpu

F32 = jnp.float32
BF16 = jnp.bfloat16
I32 = jnp.int32
U32 = jnp.uint32

HEAD_DIM = 64
PAIR = 2 * HEAD_DIM
A_Q_HEADS = 16
A_KV_HEADS = 4
A_GROUP = A_Q_HEADS // A_KV_HEADS
A_WIDTH = A_Q_HEADS * HEAD_DIM
A_KV_WIDTH = A_KV_HEADS * HEAD_DIM
A_WINDOW = 128
B_HEADS = 12
B_WIDTH = B_HEADS * HEAD_DIM
LANES = 128
LANE_CHUNKS = B_WIDTH // LANES
D_MODEL = 2048
PACKED_TILE_ROWS = D_MODEL // 2 // LANES
B_PATTERNS = ((128, 1), (512, 4), (2048, 16))
BLOCK = 128
Q_BLOCKS = 4
IN_COLS = A_WIDTH + 2 * A_KV_WIDTH + 3 * B_WIDTH
REL_BUCKETS = 32
REL_MAX_DIST = 2048
X_HEADS = 4
X_HEAD_DIM = 128
X_WIDTH = X_HEADS * X_HEAD_DIM
N_GROUPS = 4
EXP_PER_GROUP = 8
N_EXPERTS = N_GROUPS * EXP_PER_GROUP
TOP_K = 2
D_FF = 512
EPS = 1e-6
NEG = -1e30
MOE_ROWS = 256
VMEM_LIMIT = 56 * 1024 * 1024


def _cparams(n_axes):
    return pltpu.CompilerParams(dimension_semantics=("arbitrary",) * n_axes,
                                vmem_limit_bytes=VMEM_LIMIT)


def _rms(xf, g):
    return xf * lax.rsqrt(jnp.mean(xf * xf, axis=-1, keepdims=True) + EPS) * g


def _dot_nt(a, b):
    return lax.dot_general(a, b, (((1,), (1,)), ((), ())), preferred_element_type=F32)


def _load_resident(w_hbm_ref, w_ref, sem):
    @pl.when(pl.program_id(0) == 0)
    def _():
        copy = pltpu.make_async_copy(w_hbm_ref, w_ref, sem)
        copy.start()
        copy.wait()


def _in_proj_kernel(x_ref, g_ref, w_hbm_ref, oa_ref, ob_ref, q4_ref, k4_ref, v4_ref, q16_ref, k16_ref,
                    v16_ref, w_ref, acc_ref, mod4_ref, sem):
    _load_resident(w_hbm_ref, w_ref, sem)
    tm = x_ref.shape[0]
    tn = B_WIDTH
    n4 = tm // 4
    h = _rms(x_ref[...], g_ref[...]).astype(BF16)
    for j in range(2):
        oa_ref[:, j * tn:(j + 1) * tn] = jnp.dot(
            h, w_ref[:, j * tn:(j + 1) * tn], preferred_element_type=F32).astype(BF16)
    for part, (r4_ref, r16_ref) in enumerate(((q4_ref, q16_ref), (k4_ref, k16_ref), (v4_ref, v16_ref))):
        acc = jnp.dot(h, w_ref[:, (2 + part) * tn:(3 + part) * tn], preferred_element_type=F32)
        ob_ref[:, part * tn:(part + 1) * tn] = acc.astype(BF16)
        for cc in range(LANE_CHUNKS):
            acc_ref[part, cc] = acc[:, cc * LANES:(cc + 1) * LANES]
        for c in range(4):
            rows = [acc_ref[part, cc, pl.ds(c, n4, stride=4), :] for cc in range(LANE_CHUNKS)]
            r4_ref[0, c] = jnp.concatenate(rows, axis=1).astype(BF16)
            for cc in range(LANE_CHUNKS):
                mod4_ref[part, cc, c * n4:(c + 1) * n4, :] = rows[cc]
        for c in range(16):
            rows = [mod4_ref[part, cc, pl.ds((c % 4) * n4 + c // 4, tm // 16, stride=4), :]
                    for cc in range(LANE_CHUNKS)]
            r16_ref[0, c] = jnp.concatenate(rows, axis=1).astype(BF16)


def _in_proj(x2d, g, w, b, s, *, tm):
    t, d = x2d.shape
    tn = B_WIDTH
    tiles = s // tm
    res_shape = lambda r: jax.ShapeDtypeStruct((b, r, s // r, B_WIDTH), BF16)
    res_spec = lambda r: pl.BlockSpec((1, r, tm // r, B_WIDTH), lambda i: (i // tiles, 0, i % tiles, 0))
    return pl.pallas_call(
        _in_proj_kernel,
        grid=(t // tm,),
        in_specs=[
            pl.BlockSpec((tm, d), lambda i: (i, 0)),
            pl.BlockSpec((1, d), lambda i: (0, 0)),
            pl.BlockSpec(memory_space=pl.ANY),
        ],
        out_specs=[pl.BlockSpec((tm, 2 * tn), lambda i: (i, 0)), pl.BlockSpec((tm, 3 * tn), lambda i: (i, 0))]
                  + [res_spec(4)] * 3 + [res_spec(16)] * 3,
        out_shape=[jax.ShapeDtypeStruct((t, 2 * tn), BF16), jax.ShapeDtypeStruct((t, 3 * tn), BF16)]
                  + [res_shape(4)] * 3 + [res_shape(16)] * 3,
        scratch_shapes=[pltpu.VMEM(w.shape, BF16), pltpu.VMEM((3, LANE_CHUNKS, tm, LANES), F32),
                        pltpu.VMEM((3, LANE_CHUNKS, tm, LANES), F32), pltpu.SemaphoreType.DMA(())],
        compiler_params=_cparams(1),
        name="in_proj",
    )(x2d, g.reshape(1, d), w)


def _gate_proj_kernel(x_ref, g_ref, w_hbm_ref, b_ref, o_ref, w_ref, sem, *, tn):
    _load_resident(w_hbm_ref, w_ref, sem)
    h = _rms(x_ref[...], g_ref[...]).astype(BF16)
    for j in range(w_ref.shape[1] // tn):
        cols = slice(j * tn, (j + 1) * tn)
        acc = jnp.dot(h, w_ref[:, cols], preferred_element_type=F32)
        o_ref[:, cols] = jax.nn.sigmoid(acc + b_ref[:, cols]).astype(o_ref.dtype)


def _gate_proj(x2d, g, w, b, *, tm, tn):
    t, d = x2d.shape
    n = w.shape[1]
    return pl.pallas_call(
        functools.partial(_gate_proj_kernel, tn=tn),
        grid=(t // tm,),
        in_specs=[
            pl.BlockSpec((tm, d), lambda i: (i, 0)),
            pl.BlockSpec((1, d), lambda i: (0, 0)),
            pl.BlockSpec(memory_space=pl.ANY),
            pl.BlockSpec((1, n), lambda i: (0, 0)),
        ],
        out_specs=pl.BlockSpec((tm, n), lambda i: (i, 0)),
        out_shape=jax.ShapeDtypeStruct((t, n), BF16),
        scratch_shapes=[pltpu.VMEM(w.shape, BF16), pltpu.SemaphoreType.DMA(())],
        compiler_params=_cparams(1),
        name="gate_proj",
    )(x2d, g.reshape(1, d), w, b.reshape(1, n))


def _bucket_runs(step, max_off):
    max_exact = REL_BUCKETS // 2
    dist = np.arange(max_off + 1) * step
    buckets = []
    for ft in (np.float32, np.float64):
        df = np.maximum(dist, 1).astype(ft)
        large = max_exact + (np.log(df / ft(max_exact)) / ft(math.log(REL_MAX_DIST / max_exact))
                             * ft(REL_BUCKETS - max_exact)).astype(np.int32)
        buckets.append(np.where(dist < max_exact, dist, np.minimum(large, REL_BUCKETS - 1)))
    assert (buckets[0] == buckets[1]).all()
    runs = []
    for off, bucket in enumerate(buckets[0].tolist()):
        if not runs or runs[-1][1] != bucket:
            runs.append((off, bucket))
    return runs


def _fill_band_bias(bias_ref, table_ref, head0, n_heads, step, max_off):
    qi = lax.broadcasted_iota(I32, (BLOCK, 2 * BLOCK), 0)
    ki = lax.broadcasted_iota(I32, (BLOCK, 2 * BLOCK), 1)
    off = qi + BLOCK - ki
    runs = _bucket_runs(step, max_off)
    for h in range(n_heads):
        cur = jnp.full((BLOCK, 2 * BLOCK), NEG, F32)
        for first_off, bucket in runs:
            cur = jnp.where(off >= first_off, table_ref[bucket, head0 + h], cur)
        rest = jnp.where(off > max_off, NEG, cur)
        bias_ref[1, h] = rest
        bias_ref[0, h] = jnp.where(ki >= BLOCK, rest, NEG)


def _softmax_pv(s, v_pair, sink):
    m = jnp.max(s, axis=-1, keepdims=True)
    if sink is not None:
        m = jnp.maximum(m, sink)
    p = jnp.exp(s - m)
    l = jnp.sum(p, axis=-1, keepdims=True)
    if sink is not None:
        l = l + jnp.exp(sink - m)
    o = jnp.dot(p.astype(BF16), v_pair, preferred_element_type=F32)
    return o, m, l


def _swa_kernel(sink_ref, table_ref, q_ref, kp_ref, kc_ref, vp_ref, vc_ref, o_ref, bias_ref):
    @pl.when((pl.program_id(0) == 0) & (pl.program_id(1) == 0))
    def _():
        _fill_band_bias(bias_ref, table_ref, 0, A_Q_HEADS, 1, A_WINDOW - 1)

    first = jnp.minimum(pl.program_id(1), 1)
    q_all = q_ref[0] * jnp.asarray(HEAD_DIM ** -0.5, BF16)
    k_all = jnp.concatenate([kp_ref[0], kc_ref[0]], axis=0)
    v_all = jnp.concatenate([vp_ref[0], vc_ref[0]], axis=0)
    lane = lax.broadcasted_iota(I32, (1, PAIR), 1)
    low = lane < HEAD_DIM
    for blk in range(Q_BLOCKS):
        rows = slice(blk * BLOCK, (blk + 1) * BLOCK)
        q = q_all[rows]
        k = k_all[blk * BLOCK:(blk + 2) * BLOCK]
        v = v_all[blk * BLOCK:(blk + 2) * BLOCK]
        variant = first if blk == 0 else 1
        for kv_pair in range(A_KV_HEADS // 2):
            k_pair = k[:, kv_pair * PAIR:(kv_pair + 1) * PAIR]
            v_pair = v[:, kv_pair * PAIR:(kv_pair + 1) * PAIR]
            k_half = (jnp.where(low, k_pair, jnp.zeros_like(k_pair)),
                      jnp.where(low, jnp.zeros_like(k_pair), k_pair))
            for g in range(A_GROUP):
                col = (kv_pair * A_GROUP + g) * PAIR
                q_pair = q[:, col:col + PAIR]
                outs = []
                for half in range(2):
                    head = (2 * kv_pair + half) * A_GROUP + g
                    s = _dot_nt(q_pair, k_half[half]) + bias_ref[variant, head]
                    o, _, l = _softmax_pv(s, v_pair, sink_ref[head])
                    outs.append(o * (1.0 / l))
                o_ref[0, rows, col:col + PAIR] = jnp.where(low, outs[0], outs[1]).astype(o_ref.dtype)


def _swa_attention(qkv3, rel_bias, sinks):
    b, s, _ = qkv3.shape
    step = Q_BLOCKS * BLOCK
    kcol = A_WIDTH // A_KV_WIDTH
    prev = lambda i: jnp.maximum(i * Q_BLOCKS - 1, 0)
    return pl.pallas_call(
        _swa_kernel,
        grid=(b, s // step),
        in_specs=[
            pl.BlockSpec(memory_space=pltpu.SMEM),
            pl.BlockSpec(memory_space=pltpu.SMEM),
            pl.BlockSpec((1, step, A_WIDTH), lambda bi, i: (bi, i, 0)),
            pl.BlockSpec((1, BLOCK, A_KV_WIDTH), lambda bi, i: (bi, prev(i), kcol)),
            pl.BlockSpec((1, step, A_KV_WIDTH), lambda bi, i: (bi, i, kcol)),
            pl.BlockSpec((1, BLOCK, A_KV_WIDTH), lambda bi, i: (bi, prev(i), kcol + 1)),
            pl.BlockSpec((1, step, A_KV_WIDTH), lambda bi, i: (bi, i, kcol + 1)),
        ],
        out_specs=pl.BlockSpec((1, step, A_WIDTH), lambda bi, i: (bi, i, 0)),
        out_shape=jax.ShapeDtypeStruct((b, s, A_WIDTH), BF16),
        scratch_shapes=[pltpu.VMEM((2, A_Q_HEADS, BLOCK, 2 * BLOCK), F32)],
        compiler_params=_cparams(2),
        name="swa_attention",
    )(sinks, rel_bias, qkv3, qkv3, qkv3, qkv3, qkv3)


def _dilated_kernel(table_ref, q_ref, kp_ref, kc_ref, vp_ref, vc_ref, o_ref, lse_ref, bias_ref, *,
                    dilation, max_off):
    @pl.when((pl.program_id(0) == 0) & (pl.program_id(1) == 0) & (pl.program_id(2) == 0))
    def _():
        _fill_band_bias(bias_ref, table_ref, A_Q_HEADS, B_HEADS, dilation, max_off)

    first = jnp.minimum(pl.program_id(2), 1)
    q_all = q_ref[...] * jnp.asarray(HEAD_DIM ** -0.5, BF16)
    k_all = jnp.concatenate([kp_ref[...], kc_ref[...]], axis=0)
    v_all = jnp.concatenate([vp_ref[...], vc_ref[...]], axis=0)
    lane = lax.broadcasted_iota(I32, (1, PAIR), 1)
    low = lane < HEAD_DIM
    for blk in range(Q_BLOCKS):
        rows = slice(blk * BLOCK, (blk + 1) * BLOCK)
        variant = first if blk == 0 else 1
        lse_tile = jnp.zeros((BLOCK, LANES), F32)
        for pair in range(B_HEADS // 2):
            col = pair * PAIR
            q_pair = q_all[rows, col:col + PAIR]
            k_pair = k_all[blk * BLOCK:(blk + 2) * BLOCK, col:col + PAIR]
            v_pair = v_all[blk * BLOCK:(blk + 2) * BLOCK, col:col + PAIR]
            k_half = (jnp.where(low, k_pair, jnp.zeros_like(k_pair)),
                      jnp.where(low, jnp.zeros_like(k_pair), k_pair))
            outs = []
            for half in range(2):
                head = 2 * pair + half
                s = _dot_nt(q_pair, k_half[half]) + bias_ref[variant, head]
                o, m, l = _softmax_pv(s, v_pair, None)
                outs.append(o * (1.0 / l))
                lse_tile = jnp.where(lane == head, m + jnp.log(l), lse_tile)
            o_ref[rows, col:col + PAIR] = jnp.where(low, outs[0], outs[1]).astype(o_ref.dtype)
        lse_ref[rows, :] = lse_tile


def _dilated_attention(q, k, v, cols, rel_bias, window):
    b, r, n, _ = q.shape
    step = Q_BLOCKS * BLOCK
    assert n % step == 0, (n, step)
    qc, kc, vc = cols
    prev = lambda j: jnp.maximum(j * Q_BLOCKS - 1, 0)
    cur = lambda col, w=B_WIDTH: pl.BlockSpec((None, None, step, w), lambda bi, c, j: (bi, c, j, col))
    old = lambda col: pl.BlockSpec((None, None, BLOCK, B_WIDTH), lambda bi, c, j: (bi, c, prev(j), col))
    return pl.pallas_call(
        functools.partial(_dilated_kernel, dilation=r, max_off=window // r),
        grid=(b, r, n // step),
        in_specs=[pl.BlockSpec(memory_space=pltpu.SMEM), cur(qc), old(kc), cur(kc), old(vc), cur(vc)],
        out_specs=[cur(0), cur(0, LANES)],
        out_shape=[jax.ShapeDtypeStruct((b, r, n, B_WIDTH), BF16),
                   jax.ShapeDtypeStruct((b, r, n, LANES), F32)],
        scratch_shapes=[pltpu.VMEM((2, B_HEADS, BLOCK, 2 * BLOCK), F32)],
        compiler_params=_cparams(3),
        name=f"dilated_attention_r{r}",
    )(rel_bias, q, k, k, v, v)


def _mix_out_kernel(x_ref, oa_ref, o1_ref, l1_ref, o4_ref, l4_ref, o16_ref, l16_ref, ga_ref, gb_ref,
                    wa_ref, wb_ref, wo_ref, ex_ref, out_ref, seq_ref):
    tm = x_ref.shape[0]
    seq = []
    base = 0
    for r, ref in ((4, o4_ref), (4, l4_ref), (16, o16_ref), (16, l16_ref)):
        chunks = ref.shape[-1] // LANES
        for c in range(r):
            rows = ref[0, c].astype(F32)
            for cc in range(chunks):
                seq_ref[base + cc, pl.ds(c, tm // r, stride=r), :] = rows[:, cc * LANES:(cc + 1) * LANES]
        seq.append(jnp.concatenate([seq_ref[base + cc] for cc in range(chunks)], axis=1))
        base += chunks
    o2, l2, o3, l3 = seq
    l1 = l1_ref[...]
    m = jnp.maximum(jnp.maximum(l1, l2), l3)
    e1, e2, e3 = jnp.exp(l1 - m), jnp.exp(l2 - m), jnp.exp(l3 - m)
    inv = 1.0 / (e1 + e2 + e3)

    def widen(w):
        hi = w.astype(BF16)
        lo = (w - hi.astype(F32)).astype(BF16)
        return (jnp.dot(hi, ex_ref[...], preferred_element_type=F32)
                + jnp.dot(lo, ex_ref[...], preferred_element_type=F32))

    ob = (widen(e1 * inv) * o1_ref[...].astype(F32) + widen(e2 * inv) * o2 + widen(e3 * inv) * o3)
    ya = jnp.dot(oa_ref[...], wa_ref[...], preferred_element_type=F32)
    yb = jnp.dot(ob.astype(BF16), wb_ref[...], preferred_element_type=F32)
    mixed = ga_ref[...].astype(F32) * ya + gb_ref[...].astype(F32) * yb
    out_ref[...] = x_ref[...] + jnp.dot(mixed.astype(BF16), wo_ref[...], preferred_element_type=F32)


def _mix_out(x2d, oa, o1, l1, o4, l4, o16, l16, gates, wa, wb, wo, s, *, tm):
    t, d = x2d.shape
    tiles = s // tm
    row = lambda w: pl.BlockSpec((tm, w), lambda i: (i, 0))
    res = lambda r, w: pl.BlockSpec((1, r, tm // r, w), lambda i: (i // tiles, 0, i % tiles, 0))
    const = lambda shape: pl.BlockSpec(shape, lambda i: (0, 0))
    head_of_col = jnp.arange(B_WIDTH, dtype=I32)[None, :] // HEAD_DIM
    expand = (jnp.arange(LANES, dtype=I32)[:, None] == head_of_col).astype(BF16)
    return pl.pallas_call(
        _mix_out_kernel,
        grid=(t // tm,),
        in_specs=[row(d), row(A_WIDTH), row(B_WIDTH), row(LANES), res(4, B_WIDTH), res(4, LANES),
                  res(16, B_WIDTH), res(16, LANES),
                  pl.BlockSpec((tm, d), lambda i: (i, 0)), pl.BlockSpec((tm, d), lambda i: (i, 1)),
                  const(wa.shape), const(wb.shape), const(wo.shape), const(expand.shape)],
        out_specs=row(d),
        out_shape=jax.ShapeDtypeStruct((t, d), F32),
        scratch_shapes=[pltpu.VMEM((2 * (LANE_CHUNKS + 1), tm, LANES), F32)],
        compiler_params=_cparams(1),
        name="mix_out",
    )(x2d, oa, o1, l1, o4, l4, o16, l16, gates, gates, wa, wb, wo, expand)


def _mem_kv_kernel(mem_ref, g_ref, wk_ref, wv_ref, k_ref, v_ref):
    hn = _rms(mem_ref[0], g_ref[...]).astype(BF16)
    k_ref[0] = jnp.dot(hn, wk_ref[...], preferred_element_type=F32).astype(BF16)
    v_ref[0] = jnp.dot(hn, wv_ref[...], preferred_element_type=F32).astype(BF16)


def _mem_kv(mem, g, wk, wv):
    b, ml, d = mem.shape
    const = lambda shape: pl.BlockSpec(shape, lambda bi: (0,) * len(shape))
    out = jax.ShapeDtypeStruct((b, ml, X_WIDTH), BF16)
    blk = pl.BlockSpec((1, ml, X_WIDTH), lambda bi: (bi, 0, 0))
    return pl.pallas_call(
        _mem_kv_kernel,
        grid=(b,),
        in_specs=[pl.BlockSpec((1, ml, d), lambda bi: (bi, 0, 0)), const((1, d)),
                  const(wk.shape), const(wv.shape)],
        out_specs=[blk, blk],
        out_shape=[out, out],
        compiler_params=_cparams(1),
        name="mem_kv",
    )(mem, g.reshape(1, d), wk, wv)


def _cross_kernel(x_ref, g_ref, wq_ref, k_ref, v_ref, wo_ref, out_ref):
    x = x_ref[0]
    h = _rms(x, g_ref[...]).astype(BF16)
    q = (jnp.dot(h, wq_ref[...], preferred_element_type=F32) * (X_HEAD_DIM ** -0.5)).astype(BF16)
    k, v = k_ref[0], v_ref[0]
    outs = []
    for hd in range(X_HEADS):
        sl = slice(hd * X_HEAD_DIM, (hd + 1) * X_HEAD_DIM)
        s = _dot_nt(q[:, sl], k[:, sl])
        m = jnp.max(s, axis=-1, keepdims=True)
        p = jnp.exp(s - m)
        l = jnp.sum(p, axis=-1, keepdims=True)
        o = jnp.dot(p.astype(BF16), v[:, sl], preferred_element_type=F32)
        outs.append((o * (1.0 / l)).astype(BF16))
    o = jnp.concatenate(outs, axis=1)
    out_ref[0] = x + jnp.dot(o, wo_ref[...], preferred_element_type=F32)


def _cross_attention(x3, g, wq, k, v, wo, *, tm):
    b, s, d = x3.shape
    ml = k.shape[1]
    const = lambda shape: pl.BlockSpec(shape, lambda bi, i: (0,) * len(shape))
    return pl.pallas_call(
        _cross_kernel,
        grid=(b, s // tm),
        in_specs=[pl.BlockSpec((1, tm, d), lambda bi, i: (bi, i, 0)), const((1, d)), const(wq.shape),
                  pl.BlockSpec((1, ml, X_WIDTH), lambda bi, i: (bi, 0, 0)),
                  pl.BlockSpec((1, ml, X_WIDTH), lambda bi, i: (bi, 0, 0)),
                  const(wo.shape)],
        out_specs=pl.BlockSpec((1, tm, d), lambda bi, i: (bi, i, 0)),
        out_shape=jax.ShapeDtypeStruct((b, s, d), F32),
        compiler_params=_cparams(2),
        name="cross_attention",
    )(x3, g.reshape(1, d), wq, k, v, wo)


def _pack_bf16_pairs(x):
    n = x.shape[1] // 2
    bits = lax.bitcast_convert_type(x.astype(BF16).astype(F32), U32)
    return (bits[:, :n] >> 16) | (bits[:, n:] & jnp.uint32(0xFFFF0000))


def _unpack_bf16_pairs(p):
    return (lax.bitcast_convert_type(p << 16, F32),
            lax.bitcast_convert_type(p & jnp.uint32(0xFFFF0000), F32))


def _first_argmax(vals, rows, n):
    m = jnp.max(vals, axis=0, keepdims=True)
    idx = jnp.min(jnp.where(vals == m, rows, n), axis=0, keepdims=True)
    return m, idx


def _router_kernel(x_ref, g_ref, whi_ref, wlo_ref, hp_ref, e_ref, gate_ref, cnt_ref):
    h = _rms(x_ref[...], g_ref[...])
    tm, d = h.shape
    h_hi = h.astype(BF16)
    h_hi32 = h_hi.astype(F32)
    h_lo = (h - h_hi32).astype(BF16)
    logits = (jnp.dot(h_hi, whi_ref[...], preferred_element_type=F32)
              + jnp.dot(h_lo, whi_ref[...], preferred_element_type=F32)
              + jnp.dot(h_hi, wlo_ref[...], preferred_element_type=F32)).T
    rows8 = lax.broadcasted_iota(I32, (EXP_PER_GROUP, tm), 0)
    gl = jnp.where(rows8 < N_GROUPS, logits[N_EXPERTS:N_EXPERTS + 8], -jnp.inf)
    gmax, gidx = _first_argmax(gl, rows8, 8)
    g_gate = 1.0 / jnp.sum(jnp.exp(gl - gmax), axis=0, keepdims=True)
    sel = jnp.zeros((EXP_PER_GROUP, tm), F32)
    for grp in range(N_GROUPS):
        sel = jnp.where(gidx == grp, logits[grp * EXP_PER_GROUP:(grp + 1) * EXP_PER_GROUP], sel)
    v1, i1 = _first_argmax(sel, rows8, 8)
    sel2 = jnp.where(rows8 == i1, -jnp.inf, sel)
    v2, i2 = _first_argmax(sel2, rows8, 8)
    e2x = jnp.exp(v2 - v1)
    den = 1.0 + e2x
    e_ref[...] = jnp.concatenate([gidx * EXP_PER_GROUP + i1, gidx * EXP_PER_GROUP + i2], axis=0)
    gate_ref[...] = jnp.concatenate([(1.0 / den) * g_gate, (e2x / den) * g_gate], axis=0)

    bits = lax.bitcast_convert_type(h_hi32, U32)
    packed = (bits[:, :d // 2] >> 16) | (bits[:, d // 2:] & jnp.uint32(0xFFFF0000))
    for c in range(PACKED_TILE_ROWS):
        hp_ref[pl.ds(c, tm, stride=PACKED_TILE_ROWS), :] = packed[:, c * LANES:(c + 1) * LANES]

    @pl.when(pl.program_id(0) == 0)
    def _():
        cnt_ref[...] = jnp.zeros_like(cnt_ref)

    rows32 = lax.broadcasted_iota(I32, (N_EXPERTS, tm), 0)
    e = e_ref[...]
    hits = (rows32 == e[0:1]).astype(F32) + (rows32 == e[1:2]).astype(F32)
    cnt_ref[...] += jnp.sum(hits, axis=1, keepdims=True)


def _router(x2d, g, w_router, *, tm):
    t, d = x2d.shape
    w_hi = w_router.astype(BF16)
    w_lo = (w_router - w_hi.astype(F32)).astype(BF16)
    return pl.pallas_call(
        _router_kernel,
        grid=(t // tm,),
        in_specs=[pl.BlockSpec((tm, d), lambda i: (i, 0)),
                  pl.BlockSpec((1, d), lambda i: (0, 0)),
                  pl.BlockSpec((d, LANES), lambda i: (0, 0)),
                  pl.BlockSpec((d, LANES), lambda i: (0, 0))],
        out_specs=[pl.BlockSpec((tm * PACKED_TILE_ROWS, LANES), lambda i: (i, 0)),
                   pl.BlockSpec((TOP_K, tm), lambda i: (0, i)),
                   pl.BlockSpec((TOP_K, tm), lambda i: (0, i)),
                   pl.BlockSpec((N_EXPERTS, 128), lambda i: (0, 0))],
        out_shape=[jax.ShapeDtypeStruct((t * PACKED_TILE_ROWS, LANES), U32),
                   jax.ShapeDtypeStruct((TOP_K, t), I32),
                   jax.ShapeDtypeStruct((TOP_K, t), F32),
                   jax.ShapeDtypeStruct((N_EXPERTS, 128), F32)],
        compiler_params=_cparams(1),
        name="moe_router",
    )(x2d, g.reshape(1, d), w_hi, w_lo)


def _dest_kernel(e_ref, start_ref, d_ref, carry_ref):
    @pl.when(pl.program_id(0) == 0)
    def _():
        carry_ref[...] = jnp.zeros_like(carry_ref)

    e = e_ref[...]
    tm = e.shape[1]
    rows32 = lax.broadcasted_iota(I32, (N_EXPERTS, tm), 0)
    oh0 = (rows32 == e[0:1]).astype(F32)
    oh1 = (rows32 == e[1:2]).astype(F32)
    hits = oh0 + oh1
    earlier = (lax.broadcasted_iota(I32, (tm, tm), 0) < lax.broadcasted_iota(I32, (tm, tm), 1))
    prefix = jnp.dot(hits.astype(BF16), earlier.astype(BF16), preferred_element_type=F32)
    base = prefix + carry_ref[:, 0:1] + start_ref[:, 0:1]
    d0 = jnp.sum(oh0 * base, axis=0, keepdims=True)
    d1 = jnp.sum(oh1 * base, axis=0, keepdims=True)
    d_ref[...] = jnp.concatenate([d0, d1], axis=0).astype(I32)
    carry_ref[...] += jnp.sum(hits, axis=1, keepdims=True)


def _assignment_rows(e_t, start, *, tm):
    t = e_t.shape[1]
    return pl.pallas_call(
        _dest_kernel,
        grid=(t // tm,),
        in_specs=[pl.BlockSpec((TOP_K, tm), lambda i: (0, i)),
                  pl.BlockSpec((N_EXPERTS, 128), lambda i: (0, 0))],
        out_specs=pl.BlockSpec((TOP_K, tm), lambda i: (0, i)),
        out_shape=jax.ShapeDtypeStruct((TOP_K, t), I32),
        scratch_shapes=[pltpu.VMEM((N_EXPERTS, 128), F32)],
        compiler_params=_cparams(1),
        name="moe_assignment_rows",
    )(e_t, start)


def _token_rows(ref, token, rows):
    return ref.at[pl.ds(pl.multiple_of(token * rows, rows), rows)]


def _dispatch_kernel(end_ref, pad_ref, nu_ref, d_ref, hp_ref, xg_ref, zero_ref, sem, zero_sem):
    rows = PACKED_TILE_ROWS
    tm = hp_ref.shape[0] // rows
    block_rows = MOE_ROWS * rows

    @pl.when(pl.program_id(0) == 0)
    def _():
        zero_ref[...] = jnp.zeros_like(zero_ref)

        def zero_block(block):
            start = pl.multiple_of(block * block_rows, block_rows)
            return pltpu.make_async_copy(zero_ref, xg_ref.at[pl.ds(start, block_rows)], zero_sem)

        for phase in ("start", "wait"):
            for e in range(N_EXPERTS):
                @pl.when(pad_ref[e] > 0)
                def _():
                    copy = zero_block(end_ref[e] // MOE_ROWS - 1)
                    copy.start() if phase == "start" else copy.wait()

        def start_tail(block, carry):
            zero_block(block).start()
            return carry

        def wait_tail(block, carry):
            zero_block(block).wait()
            return carry

        n_blocks = xg_ref.shape[0] // block_rows
        lax.fori_loop(nu_ref[0], n_blocks, start_tail, 0)
        lax.fori_loop(nu_ref[0], n_blocks, wait_tail, 0)

    for t in range(tm):
        for k in range(TOP_K):
            pltpu.make_async_copy(_token_rows(hp_ref, t, rows), _token_rows(xg_ref, d_ref[k, t], rows),
                                  sem).start(priority=k)
    for k in range(TOP_K):
        pltpu.make_async_copy(hp_ref, xg_ref.at[pl.ds(0, tm * rows)], sem).wait()


def _dispatch(seg_end, padded, n_used, dest, hp, n_rows, *, tm):
    rows = PACKED_TILE_ROWS
    t = hp.shape[0] // rows
    return pl.pallas_call(
        _dispatch_kernel,
        grid_spec=pltpu.PrefetchScalarGridSpec(
            num_scalar_prefetch=3,
            grid=(t // tm,),
            in_specs=[pl.BlockSpec((TOP_K, tm), lambda i, *_: (0, i), memory_space=pltpu.SMEM),
                      pl.BlockSpec((tm * rows, LANES), lambda i, *_: (i, 0))],
            out_specs=pl.BlockSpec(memory_space=pl.ANY),
            scratch_shapes=[pltpu.VMEM((MOE_ROWS * rows, LANES), U32), pltpu.SemaphoreType.DMA(()),
                            pltpu.SemaphoreType.DMA(())],
        ),
        out_shape=jax.ShapeDtypeStruct((n_rows * rows, LANES), U32),
        compiler_params=_cparams(1),
        name="moe_dispatch",
    )(seg_end, padded, n_used, dest, hp)


def _expert_kernel(be_ref, nu_ref, nxt_ref, par_ref, xg_ref, w1_ref, w3_ref, w2_ref, y_ref,
                   f1_ref, f3_ref, f2_ref, w1b_ref, w3b_ref, w2b_ref, sem):
    i = pl.program_id(0)

    def weight_copies(e, slot):
        return [pltpu.make_async_copy(w_ref.at[e], f_ref.at[slot], sem.at[slot, n])
                for n, (w_ref, f_ref) in enumerate(((w1_ref, f1_ref), (w3_ref, f3_ref), (w2_ref, f2_ref)))]

    @pl.when(i < nu_ref[0])
    def _():
        e = be_ref[i]
        slot = par_ref[i]

        @pl.when((i == 0) | (e != be_ref[jnp.maximum(i - 1, 0)]))
        def _():
            @pl.when(i == 0)
            def _():
                for copy in weight_copies(e, slot):
                    copy.start()

            for copy in weight_copies(e, slot):
                copy.wait()
            w1b_ref[...] = f1_ref[slot].astype(BF16)
            w3b_ref[...] = f3_ref[slot].astype(BF16)
            w2b_ref[...] = f2_ref[slot].astype(BF16)

            @pl.when(nxt_ref[i] != e)
            def _():
                for copy in weight_copies(nxt_ref[i], 1 - slot):
                    copy.start()

        halves = [_unpack_bf16_pairs(xg_ref[pl.ds(c, MOE_ROWS, stride=PACKED_TILE_ROWS), :])
                  for c in range(PACKED_TILE_ROWS)]
        x = jnp.concatenate([lo for lo, _ in halves] + [hi for _, hi in halves], axis=1).astype(BF16)
        h1 = jnp.dot(x, w1b_ref[...], preferred_element_type=F32)
        h3 = jnp.dot(x, w3b_ref[...], preferred_element_type=F32)
        a = (jax.nn.silu(h1) * h3).astype(BF16)
        y = _pack_bf16_pairs(jnp.dot(a, w2b_ref[...], preferred_element_type=F32))
        for c in range(PACKED_TILE_ROWS):
            y_ref[pl.ds(c, MOE_ROWS, stride=PACKED_TILE_ROWS), :] = y[:, c * LANES:(c + 1) * LANES]

    @pl.when(i >= nu_ref[0])
    def _():
        y_ref[...] = jnp.zeros_like(y_ref)


def _experts(block_e, n_used, next_e, parity, xg, w1, w3, w2):
    rows = PACKED_TILE_ROWS
    n_rows = xg.shape[0] // rows
    _, d, dff = w1.shape
    nblk = n_rows // MOE_ROWS
    live = lambda i, nu: jnp.maximum(jnp.minimum(i, nu[0] - 1), 0)
    any_space = pl.BlockSpec(memory_space=pl.ANY)
    return pl.pallas_call(
        _expert_kernel,
        grid_spec=pltpu.PrefetchScalarGridSpec(
            num_scalar_prefetch=4,
            grid=(nblk,),
            in_specs=[pl.BlockSpec((MOE_ROWS * rows, LANES), lambda i, be, nu, nx, par: (live(i, nu), 0)),
                      any_space, any_space, any_space],
            out_specs=pl.BlockSpec((MOE_ROWS * rows, LANES), lambda i, be, nu, nx, par: (i, 0)),
            scratch_shapes=[pltpu.VMEM((2, d, dff), F32), pltpu.VMEM((2, d, dff), F32),
                            pltpu.VMEM((2, dff, d), F32), pltpu.VMEM((d, dff), BF16),
                            pltpu.VMEM((d, dff), BF16), pltpu.VMEM((dff, d), BF16),
                            pltpu.SemaphoreType.DMA((2, 3))],
        ),
        out_shape=jax.ShapeDtypeStruct((n_rows * rows, LANES), U32),
        compiler_params=_cparams(1),
        name="moe_experts",
    )(block_e, n_used, next_e, parity, xg, w1, w3, w2)


def _combine_kernel(dc_ref, dn_ref, x_ref, gate_ref, gf_ref, y_ref, out_ref, buf, sem, *, final):
    i = pl.program_id(0)
    n = pl.num_programs(0)
    tm = x_ref.shape[0]
    rows = PACKED_TILE_ROWS

    def issue(d_ref, slot):
        for t in range(tm):
            for k in range(TOP_K):
                pltpu.make_async_copy(_token_rows(y_ref, d_ref[k, t], rows),
                                      _token_rows(buf.at[slot, k], t, rows),
                                      sem.at[slot]).start(priority=k)

    @pl.when(i == 0)
    def _():
        issue(dc_ref, 0)

    @pl.when(i + 1 < n)
    def _():
        issue(dn_ref, (i + 1) % 2)

    slot = i % 2
    for k in range(TOP_K):
        pltpu.make_async_copy(y_ref.at[pl.ds(0, tm * rows)], buf.at[slot, k], sem.at[slot]).wait()
    g = gate_ref[...]
    g0, g1 = g[:, 0:1], g[:, 1:2]
    low, high = [], []
    for c in range(rows):
        lo0, hi0 = _unpack_bf16_pairs(buf[slot, 0, pl.ds(c, tm, stride=rows), :])
        lo1, hi1 = _unpack_bf16_pairs(buf[slot, 1, pl.ds(c, tm, stride=rows), :])
        low.append(x_ref[:, c * LANES:(c + 1) * LANES] + (g0 * lo0 + g1 * lo1))
        high.append(x_ref[:, (rows + c) * LANES:(rows + c + 1) * LANES] + (g0 * hi0 + g1 * hi1))
    y = jnp.concatenate(low + high, axis=1)
    out_ref[...] = _rms(y, gf_ref[...]) if final else y


def _combine(dest, x2d, gates_tok, g_final, y, *, final, tm):
    t, d = x2d.shape
    nt = t // tm
    return pl.pallas_call(
        functools.partial(_combine_kernel, final=final),
        grid=(nt,),
        in_specs=[pl.BlockSpec((TOP_K, tm), lambda i: (0, i), memory_space=pltpu.SMEM),
                  pl.BlockSpec((TOP_K, tm), lambda i: (0, jnp.minimum(i + 1, nt - 1)),
                               memory_space=pltpu.SMEM),
                  pl.BlockSpec((tm, d), lambda i: (i, 0)),
                  pl.BlockSpec((tm, TOP_K), lambda i: (i, 0)),
                  pl.BlockSpec((1, d), lambda i: (0, 0)),
                  pl.BlockSpec(memory_space=pl.ANY)],
        out_specs=pl.BlockSpec((tm, d), lambda i: (i, 0)),
        out_shape=jax.ShapeDtypeStruct((t, d), F32),
        scratch_shapes=[pltpu.VMEM((2, TOP_K, tm * PACKED_TILE_ROWS, LANES), U32),
                        pltpu.SemaphoreType.DMA((2,))],
        compiler_params=_cparams(1),
        name="moe_combine",
    )(dest, dest, x2d, gates_tok, g_final.reshape(1, d), y)


def _swa_q_order():
    heads = []
    for kv_pair in range(A_KV_HEADS // 2):
        for g in range(A_GROUP):
            heads += [(2 * kv_pair) * A_GROUP + g, (2 * kv_pair + 1) * A_GROUP + g]
    return jnp.asarray([h * HEAD_DIM + c for h in heads for c in range(HEAD_DIM)], I32)


def kernel(x, mem, rel_bias, g_mix, w_in, sinks_a, w_a_out, w_b_out, w_gate, b_gate, w_o,
           g_x, g_mem, w_xq, w_xk, w_xv, w_xo, g_moe, w_rg, w_re, w1, w3, w2, g_final):
    b, s, d = x.shape
    t = b * s
    depth = g_mix.shape[0]
    perm = _swa_q_order()
    n_assign = t * TOP_K
    n_rows = -(-(n_assign + N_EXPERTS * (MOE_ROWS - 1)) // MOE_ROWS) * MOE_ROWS
    n_blocks = n_rows // MOE_ROWS

    x2d = x.reshape(t, d)
    for l in range(depth):
        w_in_l = jnp.concatenate([w_in[l][:, :A_WIDTH][:, perm], w_in[l][:, A_WIDTH:]], axis=1).astype(BF16)
        qkv_a, qkv_b, q4, k4, v4, q16, k16, v16 = _in_proj(x2d, g_mix[l], w_in_l, b, s, tm=512)
        gates = _gate_proj(x2d, g_mix[l], w_gate[l].astype(BF16), b_gate[l], tm=512, tn=1024)
        oa = _swa_attention(qkv_a.reshape(b, s, 2 * B_WIDTH), rel_bias, sinks_a[l]).reshape(t, A_WIDTH)
        qkv_b4 = qkv_b.reshape(b, 1, s, 3 * B_WIDTH)
        windows = {r: w for w, r in B_PATTERNS}
        o1, l1 = _dilated_attention(qkv_b4, qkv_b4, qkv_b4, (0, 1, 2), rel_bias, windows[1])
        o4, l4 = _dilated_attention(q4, k4, v4, (0, 0, 0), rel_bias, windows[4])
        o16, l16 = _dilated_attention(q16, k16, v16, (0, 0, 0), rel_bias, windows[16])
        x2d = _mix_out(x2d, oa, o1.reshape(t, B_WIDTH), l1.reshape(t, LANES), o4, l4, o16, l16, gates,
                       w_a_out[l][perm].astype(BF16), w_b_out[l].astype(BF16), w_o[l].astype(BF16),
                       s, tm=256)
        k_mem, v_mem = _mem_kv(mem, g_mem[l], w_xk[l].astype(BF16), w_xv[l].astype(BF16))
        x2d = _cross_attention(x2d.reshape(b, s, d), g_x[l], w_xq[l].astype(BF16), k_mem, v_mem,
                               w_xo[l].astype(BF16), tm=512).reshape(t, d)
        w_router = jnp.concatenate([w_re[l], w_rg[l],
                                    jnp.zeros((d, LANES - N_EXPERTS - N_GROUPS), F32)], axis=1)
        hp, e_t, gate_t, cnt = _router(x2d, g_moe[l], w_router, tm=512)
        counts = cnt[:, 0].astype(I32)
        padded = (counts + MOE_ROWS - 1) // MOE_ROWS * MOE_ROWS
        seg_end = jnp.cumsum(padded)
        seg_start = seg_end - padded
        n_used = (seg_end[-1] // MOE_ROWS).astype(I32).reshape(1)
        block_row = jnp.arange(n_blocks, dtype=I32) * MOE_ROWS
        block_e = jnp.minimum(jnp.sum(seg_end[None, :] <= block_row[:, None], axis=1),
                              N_EXPERTS - 1).astype(I32)
        start = jnp.broadcast_to(seg_start.astype(F32)[:, None], (N_EXPERTS, 128))
        dest = _assignment_rows(e_t, start, tm=512)
        xg = _dispatch(seg_end.astype(I32), padded.astype(I32), n_used, dest, hp, n_rows, tm=128)
        experts = jnp.arange(N_EXPERTS, dtype=I32)
        owns = padded > 0
        later = (experts[None, :] > experts[:, None]) & owns[None, :]
        next_owner = jnp.min(jnp.where(later, experts[None, :], N_EXPERTS), axis=1)
        next_owner = jnp.where(next_owner == N_EXPERTS, experts, next_owner).astype(I32)
        run_parity = ((jnp.cumsum(owns.astype(I32)) - 1) % 2).astype(I32)
        y = _experts(block_e, n_used, next_owner[block_e], run_parity[block_e], xg, w1[l], w3[l], w2[l])
        x2d = _combine(dest, x2d, gate_t.T, g_final, y, final=(l + 1 == depth), tm=256)
    return x2d.reshape(b, s, d)
```

```python
import functools
import math

import jax
import jax.numpy as jnp
import numpy as np
from jax import lax
from jax.experimental import pallas as pl
from jax.experimental.pallas import tpu as pltpu

F32 = jnp.float32
BF16 = jnp.bfloat16
I32 = jnp.int32
U32 = jnp.uint32

HEAD_DIM = 64
PAIR = 2 * HEAD_DIM
A_Q_HEADS = 16
A_KV_HEADS = 4
A_GROUP = A_Q_HEADS // A_KV_HEADS
A_WIDTH = A_Q_HEADS * HEAD_DIM
A_KV_WIDTH = A_KV_HEADS * HEAD_DIM
A_WINDOW = 128
B_HEADS = 12
B_WIDTH = B_HEADS * HEAD_DIM
LANES = 128
LANE_CHUNKS = B_WIDTH // LANES
D_MODEL = 2048
PACKED_TILE_ROWS = D_MODEL // 2 // LANES
B_PATTERNS = ((128, 1), (512, 4), (2048, 16))
BLOCK = 128
Q_BLOCKS = 4
IN_COLS = A_WIDTH + 2 * A_KV_WIDTH + 3 * B_WIDTH
REL_BUCKETS = 32
REL_MAX_DIST = 2048
X_HEADS = 4
X_HEAD_DIM = 128
X_WIDTH = X_HEADS * X_HEAD_DIM
N_GROUPS = 4
EXP_PER_GROUP = 8
N_EXPERTS = N_GROUPS * EXP_PER_GROUP
TOP_K = 2
D_FF = 512
EPS = 1e-6
NEG = -1e30
LOG2E = math.log2(math.e)
LN2 = math.log(2.0)
MOE_ROWS = 256
VMEM_LIMIT = 56 * 1024 * 1024


def _cparams(n_axes):
    return pltpu.CompilerParams(dimension_semantics=("arbitrary",) * n_axes,
                                vmem_limit_bytes=VMEM_LIMIT)


def _rms(xf, g):
    return xf * lax.rsqrt(jnp.mean(xf * xf, axis=-1, keepdims=True) + EPS) * g


def _dot_nt(a, b):
    return lax.dot_general(a, b, (((1,), (1,)), ((), ())), preferred_element_type=F32)


def _load_resident(w_hbm_ref, w_ref, sem):
    @pl.when(pl.program_id(0) == 0)
    def _():
        copy = pltpu.make_async_copy(w_hbm_ref, w_ref, sem)
        copy.start()
        copy.wait()


def _in_proj_kernel(x_ref, g_ref, w_hbm_ref, oa_ref, ob_ref, q4_ref, k4_ref, v4_ref, q16_ref, k16_ref,
                    v16_ref, w_ref, acc_ref, mod4_ref, sem):
    _load_resident(w_hbm_ref, w_ref, sem)
    tm = x_ref.shape[0]
    tn = B_WIDTH
    n4 = tm // 4
    h = _rms(x_ref[...], g_ref[...]).astype(BF16)
    for j in range(2):
        oa_ref[:, j * tn:(j + 1) * tn] = jnp.dot(
            h, w_ref[:, j * tn:(j + 1) * tn], preferred_element_type=F32).astype(BF16)
    for part, (r4_ref, r16_ref) in enumerate(((q4_ref, q16_ref), (k4_ref, k16_ref), (v4_ref, v16_ref))):
        acc = jnp.dot(h, w_ref[:, (2 + part) * tn:(3 + part) * tn], preferred_element_type=F32)
        ob_ref[:, part * tn:(part + 1) * tn] = acc.astype(BF16)
        for cc in range(LANE_CHUNKS):
            acc_ref[part, cc] = acc[:, cc * LANES:(cc + 1) * LANES]
        for c in range(4):
            rows = [acc_ref[part, cc, pl.ds(c, n4, stride=4), :] for cc in range(LANE_CHUNKS)]
            r4_ref[0, c] = jnp.concatenate(rows, axis=1).astype(BF16)
            for cc in range(LANE_CHUNKS):
                mod4_ref[part, cc, c * n4:(c + 1) * n4, :] = rows[cc]
        for c in range(16):
            rows = [mod4_ref[part, cc, pl.ds((c % 4) * n4 + c // 4, tm // 16, stride=4), :]
                    for cc in range(LANE_CHUNKS)]
            r16_ref[0, c] = jnp.concatenate(rows, axis=1).astype(BF16)


def _in_proj(x2d, g, w, b, s, *, tm):
    t, d = x2d.shape
    tn = B_WIDTH
    tiles = s // tm
    res_shape = lambda r: jax.ShapeDtypeStruct((b, r, s // r, B_WIDTH), BF16)
    res_spec = lambda r: pl.BlockSpec((1, r, tm // r, B_WIDTH), lambda i: (i // tiles, 0, i % tiles, 0))
    return pl.pallas_call(
        _in_proj_kernel,
        grid=(t // tm,),
        in_specs=[
            pl.BlockSpec((tm, d), lambda i: (i, 0)),
            pl.BlockSpec((1, d), lambda i: (0, 0)),
            pl.BlockSpec(memory_space=pl.ANY),
        ],
        out_specs=[pl.BlockSpec((tm, 2 * tn), lambda i: (i, 0)), pl.BlockSpec((tm, 3 * tn), lambda i: (i, 0))]
                  + [res_spec(4)] * 3 + [res_spec(16)] * 3,
        out_shape=[jax.ShapeDtypeStruct((t, 2 * tn), BF16), jax.ShapeDtypeStruct((t, 3 * tn), BF16)]
                  + [res_shape(4)] * 3 + [res_shape(16)] * 3,
        scratch_shapes=[pltpu.VMEM(w.shape, BF16), pltpu.VMEM((3, LANE_CHUNKS, tm, LANES), F32),
                        pltpu.VMEM((3, LANE_CHUNKS, tm, LANES), F32), pltpu.SemaphoreType.DMA(())],
        compiler_params=_cparams(1),
        name="in_proj",
    )(x2d, g.reshape(1, d), w)


def _gate_proj_kernel(x_ref, g_ref, w_hbm_ref, b_ref, o_ref, w_ref, sem, *, tn):
    _load_resident(w_hbm_ref, w_ref, sem)
    h = _rms(x_ref[...], g_ref[...]).astype(BF16)
    for j in range(w_ref.shape[1] // tn):
        cols = slice(j * tn, (j + 1) * tn)
        acc = jnp.dot(h, w_ref[:, cols], preferred_element_type=F32)
        o_ref[:, cols] = jax.nn.sigmoid(acc + b_ref[:, cols]).astype(o_ref.dtype)


def _gate_proj(x2d, g, w, b, *, tm, tn):
    t, d = x2d.shape
    n = w.shape[1]
    return pl.pallas_call(
        functools.partial(_gate_proj_kernel, tn=tn),
        grid=(t // tm,),
        in_specs=[
            pl.BlockSpec((tm, d), lambda i: (i, 0)),
            pl.BlockSpec((1, d), lambda i: (0, 0)),
            pl.BlockSpec(memory_space=pl.ANY),
            pl.BlockSpec((1, n), lambda i: (0, 0)),
        ],
        out_specs=pl.BlockSpec((tm, n), lambda i: (i, 0)),
        out_shape=jax.ShapeDtypeStruct((t, n), BF16),
        scratch_shapes=[pltpu.VMEM(w.shape, BF16), pltpu.SemaphoreType.DMA(())],
        compiler_params=_cparams(1),
        name="gate_proj",
    )(x2d, g.reshape(1, d), w, b.reshape(1, n))


def _bucket_runs(step, max_off):
    max_exact = REL_BUCKETS // 2
    dist = np.arange(max_off + 1) * step
    buckets = []
    for ft in (np.float32, np.float64):
        df = np.maximum(dist, 1).astype(ft)
        large = max_exact + (np.log(df / ft(max_exact)) / ft(math.log(REL_MAX_DIST / max_exact))
                             * ft(REL_BUCKETS - max_exact)).astype(np.int32)
        buckets.append(np.where(dist < max_exact, dist, np.minimum(large, REL_BUCKETS - 1)))
    assert (buckets[0] == buckets[1]).all()
    runs = []
    for off, bucket in enumerate(buckets[0].tolist()):
        if not runs or runs[-1][1] != bucket:
            runs.append((off, bucket))
    return runs


def _fill_band_bias(bias_ref, table_ref, head0, n_heads, step, max_off):
    qi = lax.broadcasted_iota(I32, (BLOCK, 2 * BLOCK), 0)
    ki = lax.broadcasted_iota(I32, (BLOCK, 2 * BLOCK), 1)
    off = qi + BLOCK - ki
    runs = _bucket_runs(step, max_off)
    for h in range(n_heads):
        cur = jnp.full((BLOCK, 2 * BLOCK), NEG, F32)
        for first_off, bucket in runs:
            cur = jnp.where(off >= first_off, table_ref[bucket, head0 + h] * LOG2E, cur)
        rest = jnp.where(off > max_off, NEG, cur)
        bias_ref[1, h] = rest
        bias_ref[0, h] = jnp.where(ki >= BLOCK, rest, NEG)


def _softmax_pv(s, v_pair, sink):
    m = jnp.max(s, axis=-1, keepdims=True)
    if sink is not None:
        m = jnp.maximum(m, sink)
    p = jnp.exp2(s - m)
    l = jnp.sum(p, axis=-1, keepdims=True)
    if sink is not None:
        l = l + jnp.exp2(sink - m)
    o = jnp.dot(p.astype(BF16), v_pair, preferred_element_type=F32)
    return o, m, l


def _swa_kernel(sink_ref, table_ref, q_ref, kp_ref, kc_ref, vp_ref, vc_ref, o_ref, bias_ref):
    @pl.when((pl.program_id(0) == 0) & (pl.program_id(1) == 0))
    def _():
        _fill_band_bias(bias_ref, table_ref, 0, A_Q_HEADS, 1, A_WINDOW - 1)

    first = jnp.minimum(pl.program_id(1), 1)
    q_all = q_ref[0]
    k_all = jnp.concatenate([kp_ref[0], kc_ref[0]], axis=0)
    v_all = jnp.concatenate([vp_ref[0], vc_ref[0]], axis=0)
    lane = lax.broadcasted_iota(I32, (1, PAIR), 1)
    low = lane < HEAD_DIM
    for blk in range(Q_BLOCKS):
        rows = slice(blk * BLOCK, (blk + 1) * BLOCK)
        q = q_all[rows]
        k = k_all[blk * BLOCK:(blk + 2) * BLOCK]
        v = v_all[blk * BLOCK:(blk + 2) * BLOCK]
        variant = first if blk == 0 else 1
        for kv_pair in range(A_KV_HEADS // 2):
            k_pair = k[:, kv_pair * PAIR:(kv_pair + 1) * PAIR]
            v_pair = v[:, kv_pair * PAIR:(kv_pair + 1) * PAIR]
            k_half = (jnp.where(low, k_pair, jnp.zeros_like(k_pair)),
                      jnp.where(low, jnp.zeros_like(k_pair), k_pair))
            for g in range(A_GROUP):
                col = (kv_pair * A_GROUP + g) * PAIR
                q_pair = q[:, col:col + PAIR]
                outs = []
                for half in range(2):
                    head = (2 * kv_pair + half) * A_GROUP + g
                    s = _dot_nt(q_pair, k_half[half]) + bias_ref[variant, head]
                    o, _, l = _softmax_pv(s, v_pair, sink_ref[head] * LOG2E)
                    outs.append(o * (1.0 / l))
                o_ref[0, rows, col:col + PAIR] = jnp.where(low, outs[0], outs[1]).astype(o_ref.dtype)


def _swa_attention(qkv3, rel_bias, sinks):
    b, s, _ = qkv3.shape
    step = Q_BLOCKS * BLOCK
    kcol = A_WIDTH // A_KV_WIDTH
    prev = lambda i: jnp.maximum(i * Q_BLOCKS - 1, 0)
    return pl.pallas_call(
        _swa_kernel,
        grid=(b, s // step),
        in_specs=[
            pl.BlockSpec(memory_space=pltpu.SMEM),
            pl.BlockSpec(memory_space=pltpu.SMEM),
            pl.BlockSpec((1, step, A_WIDTH), lambda bi, i: (bi, i, 0)),
            pl.BlockSpec((1, BLOCK, A_KV_WIDTH), lambda bi, i: (bi, prev(i), kcol)),
            pl.BlockSpec((1, step, A_KV_WIDTH), lambda bi, i: (bi, i, kcol)),
            pl.BlockSpec((1, BLOCK, A_KV_WIDTH), lambda bi, i: (bi, prev(i), kcol + 1)),
            pl.BlockSpec((1, step, A_KV_WIDTH), lambda bi, i: (bi, i, kcol + 1)),
        ],
        out_specs=pl.BlockSpec((1, step, A_WIDTH), lambda bi, i: (bi, i, 0)),
        out_shape=jax.ShapeDtypeStruct((b, s, A_WIDTH), BF16),
        scratch_shapes=[pltpu.VMEM((2, A_Q_HEADS, BLOCK, 2 * BLOCK), F32)],
        compiler_params=_cparams(2),
        name="swa_attention",
    )(sinks, rel_bias, qkv3, qkv3, qkv3, qkv3, qkv3)


def _dilated_kernel(table_ref, q_ref, kp_ref, kc_ref, vp_ref, vc_ref, o_ref, lse_ref, bias_ref, *,
                    dilation, max_off):
    @pl.when((pl.program_id(0) == 0) & (pl.program_id(1) == 0) & (pl.program_id(2) == 0))
    def _():
        _fill_band_bias(bias_ref, table_ref, A_Q_HEADS, B_HEADS, dilation, max_off)

    first = jnp.minimum(pl.program_id(2), 1)
    q_all = q_ref[...]
    k_all = jnp.concatenate([kp_ref[...], kc_ref[...]], axis=0)
    v_all = jnp.concatenate([vp_ref[...], vc_ref[...]], axis=0)
    lane = lax.broadcasted_iota(I32, (1, PAIR), 1)
    low = lane < HEAD_DIM
    for blk in range(Q_BLOCKS):
        rows = slice(blk * BLOCK, (blk + 1) * BLOCK)
        variant = first if blk == 0 else 1
        lse_tile = jnp.zeros((BLOCK, LANES), F32)
        for pair in range(B_HEADS // 2):
            col = pair * PAIR
            q_pair = q_all[rows, col:col + PAIR]
            k_pair = k_all[blk * BLOCK:(blk + 2) * BLOCK, col:col + PAIR]
            v_pair = v_all[blk * BLOCK:(blk + 2) * BLOCK, col:col + PAIR]
            k_half = (jnp.where(low, k_pair, jnp.zeros_like(k_pair)),
                      jnp.where(low, jnp.zeros_like(k_pair), k_pair))
            outs = []
            for half in range(2):
                head = 2 * pair + half
                s = _dot_nt(q_pair, k_half[half]) + bias_ref[variant, head]
                o, m, l = _softmax_pv(s, v_pair, None)
                outs.append(o * (1.0 / l))
                lse_tile = jnp.where(lane == head, m * LN2 + jnp.log(l), lse_tile)
            o_ref[rows, col:col + PAIR] = jnp.where(low, outs[0], outs[1]).astype(o_ref.dtype)
        lse_ref[rows, :] = lse_tile


def _dilated_attention(q, k, v, cols, rel_bias, window):
    b, r, n, _ = q.shape
    step = Q_BLOCKS * BLOCK
    assert n % step == 0, (n, step)
    qc, kc, vc = cols
    prev = lambda j: jnp.maximum(j * Q_BLOCKS - 1, 0)
    cur = lambda col, w=B_WIDTH: pl.BlockSpec((None, None, step, w), lambda bi, c, j: (bi, c, j, col))
    old = lambda col: pl.BlockSpec((None, None, BLOCK, B_WIDTH), lambda bi, c, j: (bi, c, prev(j), col))
    return pl.pallas_call(
        functools.partial(_dilated_kernel, dilation=r, max_off=window // r),
        grid=(b, r, n // step),
        in_specs=[pl.BlockSpec(memory_space=pltpu.SMEM), cur(qc), old(kc), cur(kc), old(vc), cur(vc)],
        out_specs=[cur(0), cur(0, LANES)],
        out_shape=[jax.ShapeDtypeStruct((b, r, n, B_WIDTH), BF16),
                   jax.ShapeDtypeStruct((b, r, n, LANES), F32)],
        scratch_shapes=[pltpu.VMEM((2, B_HEADS, BLOCK, 2 * BLOCK), F32)],
        compiler_params=_cparams(3),
        name=f"dilated_attention_r{r}",
    )(rel_bias, q, k, k, v, v)


def _mix_out_kernel(x_ref, oa_ref, o1_ref, l1_ref, o4_ref, l4_ref, o16_ref, l16_ref, ga_ref, gb_ref,
                    wa_ref, wb_ref, wo_ref, ex_ref, out_ref, seq_ref):
    tm = x_ref.shape[0]
    seq = []
    base = 0
    for r, ref in ((4, o4_ref), (4, l4_ref), (16, o16_ref), (16, l16_ref)):
        chunks = ref.shape[-1] // LANES
        for c in range(r):
            rows = ref[0, c].astype(F32)
            for cc in range(chunks):
                seq_ref[base + cc, pl.ds(c, tm // r, stride=r), :] = rows[:, cc * LANES:(cc + 1) * LANES]
        seq.append(jnp.concatenate([seq_ref[base + cc] for cc in range(chunks)], axis=1))
        base += chunks
    o2, l2, o3, l3 = seq
    l1 = l1_ref[...]
    m = jnp.maximum(jnp.maximum(l1, l2), l3)
    e1, e2, e3 = jnp.exp(l1 - m), jnp.exp(l2 - m), jnp.exp(l3 - m)
    inv = 1.0 / (e1 + e2 + e3)

    def widen(w):
        hi = w.astype(BF16)
        lo = (w - hi.astype(F32)).astype(BF16)
        return (jnp.dot(hi, ex_ref[...], preferred_element_type=F32)
                + jnp.dot(lo, ex_ref[...], preferred_element_type=F32))

    ob = (widen(e1 * inv) * o1_ref[...].astype(F32) + widen(e2 * inv) * o2 + widen(e3 * inv) * o3)
    ya = jnp.dot(oa_ref[...], wa_ref[...], preferred_element_type=F32)
    yb = jnp.dot(ob.astype(BF16), wb_ref[...], preferred_element_type=F32)
    mixed = ga_ref[...].astype(F32) * ya + gb_ref[...].astype(F32) * yb
    out_ref[...] = x_ref[...] + jnp.dot(mixed.astype(BF16), wo_ref[...], preferred_element_type=F32)


def _mix_out(x2d, oa, o1, l1, o4, l4, o16, l16, gates, wa, wb, wo, s, *, tm):
    t, d = x2d.shape
    tiles = s // tm
    row = lambda w: pl.BlockSpec((tm, w), lambda i: (i, 0))
    res = lambda r, w: pl.BlockSpec((1, r, tm // r, w), lambda i: (i // tiles, 0, i % tiles, 0))
    const = lambda shape: pl.BlockSpec(shape, lambda i: (0, 0))
    head_of_col = jnp.arange(B_WIDTH, dtype=I32)[None, :] // HEAD_DIM
    expand = (jnp.arange(LANES, dtype=I32)[:, None] == head_of_col).astype(BF16)
    return pl.pallas_call(
        _mix_out_kernel,
        grid=(t // tm,),
        in_specs=[row(d), row(A_WIDTH), row(B_WIDTH), row(LANES), res(4, B_WIDTH), res(4, LANES),
                  res(16, B_WIDTH), res(16, LANES),
                  pl.BlockSpec((tm, d), lambda i: (i, 0)), pl.BlockSpec((tm, d), lambda i: (i, 1)),
                  const(wa.shape), const(wb.shape), const(wo.shape), const(expand.shape)],
        out_specs=row(d),
        out_shape=jax.ShapeDtypeStruct((t, d), F32),
        scratch_shapes=[pltpu.VMEM((2 * (LANE_CHUNKS + 1), tm, LANES), F32)],
        compiler_params=_cparams(1),
        name="mix_out",
    )(x2d, oa, o1, l1, o4, l4, o16, l16, gates, gates, wa, wb, wo, expand)


def _mem_kv_kernel(mem_ref, g_ref, wk_ref, wv_ref, k_ref, v_ref):
    hn = _rms(mem_ref[0], g_ref[...]).astype(BF16)
    k_ref[0] = jnp.dot(hn, wk_ref[...], preferred_element_type=F32).astype(BF16)
    v_ref[0] = jnp.dot(hn, wv_ref[...], preferred_element_type=F32).astype(BF16)


def _mem_kv(mem, g, wk, wv):
    b, ml, d = mem.shape
    const = lambda shape: pl.BlockSpec(shape, lambda bi: (0,) * len(shape))
    out = jax.ShapeDtypeStruct((b, ml, X_WIDTH), BF16)
    blk = pl.BlockSpec((1, ml, X_WIDTH), lambda bi: (bi, 0, 0))
    return pl.pallas_call(
        _mem_kv_kernel,
        grid=(b,),
        in_specs=[pl.BlockSpec((1, ml, d), lambda bi: (bi, 0, 0)), const((1, d)),
                  const(wk.shape), const(wv.shape)],
        out_specs=[blk, blk],
        out_shape=[out, out],
        compiler_params=_cparams(1),
        name="mem_kv",
    )(mem, g.reshape(1, d), wk, wv)


def _cross_kernel(x_ref, g_ref, wq_ref, k_ref, v_ref, wo_ref, out_ref):
    x = x_ref[0]
    h = _rms(x, g_ref[...]).astype(BF16)
    q = (jnp.dot(h, wq_ref[...], preferred_element_type=F32) * (X_HEAD_DIM ** -0.5)).astype(BF16)
    k, v = k_ref[0], v_ref[0]
    outs = []
    for hd in range(X_HEADS):
        sl = slice(hd * X_HEAD_DIM, (hd + 1) * X_HEAD_DIM)
        s = _dot_nt(q[:, sl], k[:, sl])
        m = jnp.max(s, axis=-1, keepdims=True)
        p = jnp.exp(s - m)
        l = jnp.sum(p, axis=-1, keepdims=True)
        o = jnp.dot(p.astype(BF16), v[:, sl], preferred_element_type=F32)
        outs.append((o * (1.0 / l)).astype(BF16))
    o = jnp.concatenate(outs, axis=1)
    out_ref[0] = x + jnp.dot(o, wo_ref[...], preferred_element_type=F32)


def _cross_attention(x3, g, wq, k, v, wo, *, tm):
    b, s, d = x3.shape
    ml = k.shape[1]
    const = lambda shape: pl.BlockSpec(shape, lambda bi, i: (0,) * len(shape))
    return pl.pallas_call(
        _cross_kernel,
        grid=(b, s // tm),
        in_specs=[pl.BlockSpec((1, tm, d), lambda bi, i: (bi, i, 0)), const((1, d)), const(wq.shape),
                  pl.BlockSpec((1, ml, X_WIDTH), lambda bi, i: (bi, 0, 0)),
                  pl.BlockSpec((1, ml, X_WIDTH), lambda bi, i: (bi, 0, 0)),
                  const(wo.shape)],
        out_specs=pl.BlockSpec((1, tm, d), lambda bi, i: (bi, i, 0)),
        out_shape=jax.ShapeDtypeStruct((b, s, d), F32),
        compiler_params=_cparams(2),
        name="cross_attention",
    )(x3, g.reshape(1, d), wq, k, v, wo)


def _pack_bf16_pairs(x):
    n = x.shape[1] // 2
    bits = lax.bitcast_convert_type(x.astype(BF16).astype(F32), U32)
    return (bits[:, :n] >> 16) | (bits[:, n:] & jnp.uint32(0xFFFF0000))


def _unpack_bf16_pairs(p):
    return (lax.bitcast_convert_type(p << 16, F32),
            lax.bitcast_convert_type(p & jnp.uint32(0xFFFF0000), F32))


def _first_argmax(vals, rows, n):
    m = jnp.max(vals, axis=0, keepdims=True)
    idx = jnp.min(jnp.where(vals == m, rows, n), axis=0, keepdims=True)
    return m, idx


def _router_kernel(x_ref, g_ref, whi_ref, wlo_ref, hp_ref, e_ref, gate_ref, cnt_ref):
    h = _rms(x_ref[...], g_ref[...])
    tm, d = h.shape
    h_hi = h.astype(BF16)
    h_hi32 = h_hi.astype(F32)
    h_lo = (h - h_hi32).astype(BF16)
    logits = (jnp.dot(h_hi, whi_ref[...], preferred_element_type=F32)
              + jnp.dot(h_lo, whi_ref[...], preferred_element_type=F32)
              + jnp.dot(h_hi, wlo_ref[...], preferred_element_type=F32)).T
    rows8 = lax.broadcasted_iota(I32, (EXP_PER_GROUP, tm), 0)
    gl = jnp.where(rows8 < N_GROUPS, logits[N_EXPERTS:N_EXPERTS + 8], -jnp.inf)
    gmax, gidx = _first_argmax(gl, rows8, 8)
    g_gate = 1.0 / jnp.sum(jnp.exp(gl - gmax), axis=0, keepdims=True)
    sel = jnp.zeros((EXP_PER_GROUP, tm), F32)
    for grp in range(N_GROUPS):
        sel = jnp.where(gidx == grp, logits[grp * EXP_PER_GROUP:(grp + 1) * EXP_PER_GROUP], sel)
    v1, i1 = _first_argmax(sel, rows8, 8)
    sel2 = jnp.where(rows8 == i1, -jnp.inf, sel)
    v2, i2 = _first_argmax(sel2, rows8, 8)
    e2x = jnp.exp(v2 - v1)
    den = 1.0 + e2x
    e_ref[...] = jnp.concatenate([gidx * EXP_PER_GROUP + i1, gidx * EXP_PER_GROUP + i2], axis=0)
    gate_ref[...] = jnp.concatenate([(1.0 / den) * g_gate, (e2x / den) * g_gate], axis=0)

    bits = lax.bitcast_convert_type(h_hi32, U32)
    packed = (bits[:, :d // 2] >> 16) | (bits[:, d // 2:] & jnp.uint32(0xFFFF0000))
    for c in range(PACKED_TILE_ROWS):
        hp_ref[pl.ds(c, tm, stride=PACKED_TILE_ROWS), :] = packed[:, c * LANES:(c + 1) * LANES]

    @pl.when(pl.program_id(0) == 0)
    def _():
        cnt_ref[...] = jnp.zeros_like(cnt_ref)

    rows32 = lax.broadcasted_iota(I32, (N_EXPERTS, tm), 0)
    e = e_ref[...]
    hits = (rows32 == e[0:1]).astype(F32) + (rows32 == e[1:2]).astype(F32)
    cnt_ref[...] += jnp.sum(hits, axis=1, keepdims=True)


def _router(x2d, g, w_router, *, tm):
    t, d = x2d.shape
    w_hi = w_router.astype(BF16)
    w_lo = (w_router - w_hi.astype(F32)).astype(BF16)
    return pl.pallas_call(
        _router_kernel,
        grid=(t // tm,),
        in_specs=[pl.BlockSpec((tm, d), lambda i: (i, 0)),
                  pl.BlockSpec((1, d), lambda i: (0, 0)),
                  pl.BlockSpec((d, LANES), lambda i: (0, 0)),
                  pl.BlockSpec((d, LANES), lambda i: (0, 0))],
        out_specs=[pl.BlockSpec((tm * PACKED_TILE_ROWS, LANES), lambda i: (i, 0)),
                   pl.BlockSpec((TOP_K, tm), lambda i: (0, i)),
                   pl.BlockSpec((TOP_K, tm), lambda i: (0, i)),
                   pl.BlockSpec((N_EXPERTS, 128), lambda i: (0, 0))],
        out_shape=[jax.ShapeDtypeStruct((t * PACKED_TILE_ROWS, LANES), U32),
                   jax.ShapeDtypeStruct((TOP_K, t), I32),
                   jax.ShapeDtypeStruct((TOP_K, t), F32),
                   jax.ShapeDtypeStruct((N_EXPERTS, 128), F32)],
        compiler_params=_cparams(1),
        name="moe_router",
    )(x2d, g.reshape(1, d), w_hi, w_lo)


def _dest_kernel(e_ref, start_ref, d_ref, carry_ref):
    @pl.when(pl.program_id(0) == 0)
    def _():
        carry_ref[...] = jnp.zeros_like(carry_ref)

    e = e_ref[...]
    tm = e.shape[1]
    rows32 = lax.broadcasted_iota(I32, (N_EXPERTS, tm), 0)
    oh0 = (rows32 == e[0:1]).astype(F32)
    oh1 = (rows32 == e[1:2]).astype(F32)
    hits = oh0 + oh1
    earlier = (lax.broadcasted_iota(I32, (tm, tm), 0) < lax.broadcasted_iota(I32, (tm, tm), 1))
    prefix = jnp.dot(hits.astype(BF16), earlier.astype(BF16), preferred_element_type=F32)
    base = prefix + carry_ref[:, 0:1] + start_ref[:, 0:1]
    d0 = jnp.sum(oh0 * base, axis=0, keepdims=True)
    d1 = jnp.sum(oh1 * base, axis=0, keepdims=True)
    d_ref[...] = jnp.concatenate([d0, d1], axis=0).astype(I32)
    carry_ref[...] += jnp.sum(hits, axis=1, keepdims=True)


def _assignment_rows(e_t, start, *, tm):
    t = e_t.shape[1]
    return pl.pallas_call(
        _dest_kernel,
        grid=(t // tm,),
        in_specs=[pl.BlockSpec((TOP_K, tm), lambda i: (0, i)),
                  pl.BlockSpec((N_EXPERTS, 128), lambda i: (0, 0))],
        out_specs=pl.BlockSpec((TOP_K, tm), lambda i: (0, i)),
        out_shape=jax.ShapeDtypeStruct((TOP_K, t), I32),
        scratch_shapes=[pltpu.VMEM((N_EXPERTS, 128), F32)],
        compiler_params=_cparams(1),
        name="moe_assignment_rows",
    )(e_t, start)


def _token_rows(ref, token, rows):
    return ref.at[pl.ds(pl.multiple_of(token * rows, rows), rows)]


def _dispatch_kernel(end_ref, pad_ref, nu_ref, d_ref, hp_ref, xg_ref, zero_ref, sem, zero_sem):
    rows = PACKED_TILE_ROWS
    tm = hp_ref.shape[0] // rows
    block_rows = MOE_ROWS * rows

    @pl.when(pl.program_id(0) == 0)
    def _():
        zero_ref[...] = jnp.zeros_like(zero_ref)

        def zero_block(block):
            start = pl.multiple_of(block * block_rows, block_rows)
            return pltpu.make_async_copy(zero_ref, xg_ref.at[pl.ds(start, block_rows)], zero_sem)

        for phase in ("start", "wait"):
            for e in range(N_EXPERTS):
                @pl.when(pad_ref[e] > 0)
                def _():
                    copy = zero_block(end_ref[e] // MOE_ROWS - 1)
                    copy.start() if phase == "start" else copy.wait()

        def start_tail(block, carry):
            zero_block(block).start()
            return carry

        def wait_tail(block, carry):
            zero_block(block).wait()
            return carry

        n_blocks = xg_ref.shape[0] // block_rows
        lax.fori_loop(nu_ref[0], n_blocks, start_tail, 0)
        lax.fori_loop(nu_ref[0], n_blocks, wait_tail, 0)

    for t in range(tm):
        for k in range(TOP_K):
            pltpu.make_async_copy(_token_rows(hp_ref, t, rows), _token_rows(xg_ref, d_ref[k, t], rows),
                                  sem).start(priority=k)
    for k in range(TOP_K):
        pltpu.make_async_copy(hp_ref, xg_ref.at[pl.ds(0, tm * rows)], sem).wait()


def _dispatch(seg_end, padded, n_used, dest, hp, n_rows, *, tm):
    rows = PACKED_TILE_ROWS
    t = hp.shape[0] // rows
    return pl.pallas_call(
        _dispatch_kernel,
        grid_spec=pltpu.PrefetchScalarGridSpec(
            num_scalar_prefetch=3,
            grid=(t // tm,),
            in_specs=[pl.BlockSpec((TOP_K, tm), lambda i, *_: (0, i), memory_space=pltpu.SMEM),
                      pl.BlockSpec((tm * rows, LANES), lambda i, *_: (i, 0))],
            out_specs=pl.BlockSpec(memory_space=pl.ANY),
            scratch_shapes=[pltpu.VMEM((MOE_ROWS * rows, LANES), U32), pltpu.SemaphoreType.DMA(()),
                            pltpu.SemaphoreType.DMA(())],
        ),
        out_shape=jax.ShapeDtypeStruct((n_rows * rows, LANES), U32),
        compiler_params=_cparams(1),
        name="moe_dispatch",
    )(seg_end, padded, n_used, dest, hp)


def _expert_kernel(be_ref, nu_ref, nxt_ref, par_ref, xg_ref, w1_ref, w3_ref, w2_ref, y_ref,
                   f1_ref, f3_ref, f2_ref, w1b_ref, w3b_ref, w2b_ref, sem):
    i = pl.program_id(0)

    def weight_copies(e, slot):
        return [pltpu.make_async_copy(w_ref.at[e], f_ref.at[slot], sem.at[slot, n])
                for n, (w_ref, f_ref) in enumerate(((w1_ref, f1_ref), (w3_ref, f3_ref), (w2_ref, f2_ref)))]

    @pl.when(i < nu_ref[0])
    def _():
        e = be_ref[i]
        slot = par_ref[i]

        @pl.when((i == 0) | (e != be_ref[jnp.maximum(i - 1, 0)]))
        def _():
            @pl.when(i == 0)
            def _():
                for copy in weight_copies(e, slot):
                    copy.start()

            for copy in weight_copies(e, slot):
                copy.wait()
            w1b_ref[...] = f1_ref[slot].astype(BF16)
            w3b_ref[...] = f3_ref[slot].astype(BF16)
            w2b_ref[...] = f2_ref[slot].astype(BF16)

            @pl.when(nxt_ref[i] != e)
            def _():
                for copy in weight_copies(nxt_ref[i], 1 - slot):
                    copy.start()

        halves = [_unpack_bf16_pairs(xg_ref[pl.ds(c, MOE_ROWS, stride=PACKED_TILE_ROWS), :])
                  for c in range(PACKED_TILE_ROWS)]
        x = jnp.concatenate([lo for lo, _ in halves] + [hi for _, hi in halves], axis=1).astype(BF16)
        h1 = jnp.dot(x, w1b_ref[...], preferred_element_type=F32)
        h3 = jnp.dot(x, w3b_ref[...], preferred_element_type=F32)
        a = (jax.nn.silu(h1) * h3).astype(BF16)
        y = _pack_bf16_pairs(jnp.dot(a, w2b_ref[...], preferred_element_type=F32))
        for c in range(PACKED_TILE_ROWS):
            y_ref[pl.ds(c, MOE_ROWS, stride=PACKED_TILE_ROWS), :] = y[:, c * LANES:(c + 1) * LANES]

    @pl.when(i >= nu_ref[0])
    def _():
        y_ref[...] = jnp.zeros_like(y_ref)


def _experts(block_e, n_used, next_e, parity, xg, w1, w3, w2):
    rows = PACKED_TILE_ROWS
    n_rows = xg.shape[0] // rows
    _, d, dff = w1.shape
    nblk = n_rows // MOE_ROWS
    live = lambda i, nu: jnp.maximum(jnp.minimum(i, nu[0] - 1), 0)
    any_space = pl.BlockSpec(memory_space=pl.ANY)
    return pl.pallas_call(
        _expert_kernel,
        grid_spec=pltpu.PrefetchScalarGridSpec(
            num_scalar_prefetch=4,
            grid=(nblk,),
            in_specs=[pl.BlockSpec((MOE_ROWS * rows, LANES), lambda i, be, nu, nx, par: (live(i, nu), 0)),
                      any_space, any_space, any_space],
            out_specs=pl.BlockSpec((MOE_ROWS * rows, LANES), lambda i, be, nu, nx, par: (i, 0)),
            scratch_shapes=[pltpu.VMEM((2, d, dff), F32), pltpu.VMEM((2, d, dff), F32),
                            pltpu.VMEM((2, dff, d), F32), pltpu.VMEM((d, dff), BF16),
                            pltpu.VMEM((d, dff), BF16), pltpu.VMEM((dff, d), BF16),
                            pltpu.SemaphoreType.DMA((2, 3))],
        ),
        out_shape=jax.ShapeDtypeStruct((n_rows * rows, LANES), U32),
        compiler_params=_cparams(1),
        name="moe_experts",
    )(block_e, n_used, next_e, parity, xg, w1, w3, w2)


def _combine_kernel(dc_ref, dn_ref, x_ref, gate_ref, gf_ref, y_ref, out_ref, buf, sem, *, final):
    i = pl.program_id(0)
    n = pl.num_programs(0)
    tm = x_ref.shape[0]
    rows = PACKED_TILE_ROWS

    def issue(d_ref, slot):
        for t in range(tm):
            for k in range(TOP_K):
                pltpu.make_async_copy(_token_rows(y_ref, d_ref[k, t], rows),
                                      _token_rows(buf.at[slot, k], t, rows),
                                      sem.at[slot]).start(priority=k)

    @pl.when(i == 0)
    def _():
        issue(dc_ref, 0)

    @pl.when(i + 1 < n)
    def _():
        issue(dn_ref, (i + 1) % 2)

    slot = i % 2
    for k in range(TOP_K):
        pltpu.make_async_copy(y_ref.at[pl.ds(0, tm * rows)], buf.at[slot, k], sem.at[slot]).wait()
    g = gate_ref[...]
    g0, g1 = g[:, 0:1], g[:, 1:2]
    low, high = [], []
    for c in range(rows):
        lo0, hi0 = _unpack_bf16_pairs(buf[slot, 0, pl.ds(c, tm, stride=rows), :])
        lo1, hi1 = _unpack_bf16_pairs(buf[slot, 1, pl.ds(c, tm, stride=rows), :])
        low.append(x_ref[:, c * LANES:(c + 1) * LANES] + (g0 * lo0 + g1 * lo1))
        high.append(x_ref[:, (rows + c) * LANES:(rows + c + 1) * LANES] + (g0 * hi0 + g1 * hi1))
    y = jnp.concatenate(low + high, axis=1)
    out_ref[...] = _rms(y, gf_ref[...]) if final else y


def _combine(dest, x2d, gates_tok, g_final, y, *, final, tm):
    t, d = x2d.shape
    nt = t // tm
    return pl.pallas_call(
        functools.partial(_combine_kernel, final=final),
        grid=(nt,),
        in_specs=[pl.BlockSpec((TOP_K, tm), lambda i: (0, i), memory_space=pltpu.SMEM),
                  pl.BlockSpec((TOP_K, tm), lambda i: (0, jnp.minimum(i + 1, nt - 1)),
                               memory_space=pltpu.SMEM),
                  pl.BlockSpec((tm, d), lambda i: (i, 0)),
                  pl.BlockSpec((tm, TOP_K), lambda i: (i, 0)),
                  pl.BlockSpec((1, d), lambda i: (0, 0)),
                  pl.BlockSpec(memory_space=pl.ANY)],
        out_specs=pl.BlockSpec((tm, d), lambda i: (i, 0)),
        out_shape=jax.ShapeDtypeStruct((t, d), F32),
        scratch_shapes=[pltpu.VMEM((2, TOP_K, tm * PACKED_TILE_ROWS, LANES), U32),
                        pltpu.SemaphoreType.DMA((2,))],
        compiler_params=_cparams(1),
        name="moe_combine",
    )(dest, dest, x2d, gates_tok, g_final.reshape(1, d), y)


def _swa_q_order():
    heads = []
    for kv_pair in range(A_KV_HEADS // 2):
        for g in range(A_GROUP):
            heads += [(2 * kv_pair) * A_GROUP + g, (2 * kv_pair + 1) * A_GROUP + g]
    return jnp.asarray([h * HEAD_DIM + c for h in heads for c in range(HEAD_DIM)], I32)


def kernel(x, mem, rel_bias, g_mix, w_in, sinks_a, w_a_out, w_b_out, w_gate, b_gate, w_o,
           g_x, g_mem, w_xq, w_xk, w_xv, w_xo, g_moe, w_rg, w_re, w1, w3, w2, g_final):
    b, s, d = x.shape
    t = b * s
    depth = g_mix.shape[0]
    perm = _swa_q_order()
    n_assign = t * TOP_K
    n_rows = -(-(n_assign + N_EXPERTS * (MOE_ROWS - 1)) // MOE_ROWS) * MOE_ROWS
    n_blocks = n_rows // MOE_ROWS

    x2d = x.reshape(t, d)
    for l in range(depth):
        q_scale = HEAD_DIM ** -0.5 * LOG2E
        kv_a = A_WIDTH + 2 * A_KV_WIDTH
        w_in_l = jnp.concatenate([w_in[l][:, :A_WIDTH][:, perm] * q_scale, w_in[l][:, A_WIDTH:kv_a],
                                  w_in[l][:, kv_a:kv_a + B_WIDTH] * q_scale,
                                  w_in[l][:, kv_a + B_WIDTH:]], axis=1).astype(BF16)
        qkv_a, qkv_b, q4, k4, v4, q16, k16, v16 = _in_proj(x2d, g_mix[l], w_in_l, b, s, tm=512)
        gates = _gate_proj(x2d, g_mix[l], w_gate[l].astype(BF16), b_gate[l], tm=512, tn=1024)
        oa = _swa_attention(qkv_a.reshape(b, s, 2 * B_WIDTH), rel_bias, sinks_a[l]).reshape(t, A_WIDTH)
        qkv_b4 = qkv_b.reshape(b, 1, s, 3 * B_WIDTH)
        windows = {r: w for w, r in B_PATTERNS}
        o1, l1 = _dilated_attention(qkv_b4, qkv_b4, qkv_b4, (0, 1, 2), rel_bias, windows[1])
        o4, l4 = _dilated_attention(q4, k4, v4, (0, 0, 0), rel_bias, windows[4])
        o16, l16 = _dilated_attention(q16, k16, v16, (0, 0, 0), rel_bias, windows[16])
        x2d = _mix_out(x2d, oa, o1.reshape(t, B_WIDTH), l1.reshape(t, LANES), o4, l4, o16, l16, gates,
                       w_a_out[l][perm].astype(BF16), w_b_out[l].astype(BF16), w_o[l].astype(BF16),
                       s, tm=256)
        k_mem, v_mem = _mem_kv(mem, g_mem[l], w_xk[l].astype(BF16), w_xv[l].astype(BF16))
        x2d = _cross_attention(x2d.reshape(b, s, d), g_x[l], w_xq[l].astype(BF16), k_mem, v_mem,
                               w_xo[l].astype(BF16), tm=512).reshape(t, d)
        w_router = jnp.concatenate([w_re[l], w_rg[l],
                                    jnp.zeros((d, LANES - N_EXPERTS - N_GROUPS), F32)], axis=1)
        hp, e_t, gate_t, cnt = _router(x2d, g_moe[l], w_router, tm=512)
        counts = cnt[:, 0].astype(I32)
        padded = (counts + MOE_ROWS - 1) // MOE_ROWS * MOE_ROWS
        seg_end = jnp.cumsum(padded)
        seg_start = seg_end - padded
        n_used = (seg_end[-1] // MOE_ROWS).astype(I32).reshape(1)
        block_row = jnp.arange(n_blocks, dtype=I32) * MOE_ROWS
        block_e = jnp.minimum(jnp.sum(seg_end[None, :] <= block_row[:, None], axis=1),
                              N_EXPERTS - 1).astype(I32)
        start = jnp.broadcast_to(seg_start.astype(F32)[:, None], (N_EXPERTS, 128))
        dest = _assignment_rows(e_t, start, tm=512)
        xg = _dispatch(seg_end.astype(I32), padded.astype(I32), n_used, dest, hp, n_rows, tm=128)
        experts = jnp.arange(N_EXPERTS, dtype=I32)
        owns = padded > 0
        later = (experts[None, :] > experts[:, None]) & owns[None, :]
        next_owner = jnp.min(jnp.where(later, experts[None, :], N_EXPERTS), axis=1)
        next_owner = jnp.where(next_owner == N_EXPERTS, experts, next_owner).astype(I32)
        run_parity = ((jnp.cumsum(owns.astype(I32)) - 1) % 2).astype(I32)
        y = _experts(block_e, n_used, next_owner[block_e], run_parity[block_e], xg, w1[l], w3[l], w2[l])
        x2d = _combine(dest, x2d, gate_t.T, g_final, y, final=(l + 1 == depth), tm=256)
    return x2d.reshape(b, s, d)
```

```python
import functools
import math

import jax
import jax.numpy as jnp
import numpy as np
from jax import lax
from jax.experimental import pallas as pl
from jax.experimental.pallas import tpu as pltpu

F32 = jnp.float32
BF16 = jnp.bfloat16
I32 = jnp.int32
U32 = jnp.uint32

HEAD_DIM = 64
PAIR = 2 * HEAD_DIM
A_Q_HEADS = 16
A_KV_HEADS = 4
A_GROUP = A_Q_HEADS // A_KV_HEADS
A_WIDTH = A_Q_HEADS * HEAD_DIM
A_KV_WIDTH = A_KV_HEADS * HEAD_DIM
A_WINDOW = 128
B_HEADS = 12
B_WIDTH = B_HEADS * HEAD_DIM
LANES = 128
LANE_CHUNKS = B_WIDTH // LANES
D_MODEL = 2048
PACKED_TILE_ROWS = D_MODEL // 2 // LANES
B_PATTERNS = ((128, 1), (512, 4), (2048, 16))
BLOCK = 128
Q_BLOCKS = 4
IN_COLS = A_WIDTH + 2 * A_KV_WIDTH + 3 * B_WIDTH
REL_BUCKETS = 32
REL_MAX_DIST = 2048
X_HEADS = 4
X_HEAD_DIM = 128
X_WIDTH = X_HEADS * X_HEAD_DIM
N_GROUPS = 4
EXP_PER_GROUP = 8
N_EXPERTS = N_GROUPS * EXP_PER_GROUP
TOP_K = 2
D_FF = 512
EPS = 1e-6
NEG = -1e30
LOG2E = math.log2(math.e)
LN2 = math.log(2.0)
MOE_ROWS = 256
VMEM_LIMIT = 56 * 1024 * 1024


def _cparams(n_axes):
    return pltpu.CompilerParams(dimension_semantics=("arbitrary",) * n_axes,
                                vmem_limit_bytes=VMEM_LIMIT)


def _rms(xf, g):
    return xf * lax.rsqrt(jnp.mean(xf * xf, axis=-1, keepdims=True) + EPS) * g


def _dot_nt(a, b):
    return lax.dot_general(a, b, (((1,), (1,)), ((), ())), preferred_element_type=F32)


def _load_resident(w_hbm_ref, w_ref, sem):
    @pl.when(pl.program_id(0) == 0)
    def _():
        copy = pltpu.make_async_copy(w_hbm_ref, w_ref, sem)
        copy.start()
        copy.wait()


def _in_proj_kernel(x_ref, g_ref, cs_ref, w_hbm_ref, oa_ref, ob_ref, q4_ref, k4_ref, v4_ref, q16_ref,
                    k16_ref, v16_ref, w_ref, acc_ref, mod4_ref, sem):
    _load_resident(w_hbm_ref, w_ref, sem)
    tm = x_ref.shape[0]
    tn = B_WIDTH
    n4 = tm // 4
    h = _rms(x_ref[...], g_ref[...]).astype(BF16)
    for j in range(2):
        cols = slice(j * tn, (j + 1) * tn)
        acc = jnp.dot(h, w_ref[:, cols], preferred_element_type=F32)
        oa_ref[:, cols] = (acc * cs_ref[:, cols]).astype(BF16)
    for part, (r4_ref, r16_ref) in enumerate(((q4_ref, q16_ref), (k4_ref, k16_ref), (v4_ref, v16_ref))):
        cols = slice((2 + part) * tn, (3 + part) * tn)
        acc = jnp.dot(h, w_ref[:, cols], preferred_element_type=F32)
        if part == 0:
            acc = acc * cs_ref[:, cols]
        ob_ref[:, part * tn:(part + 1) * tn] = acc.astype(BF16)
        for cc in range(LANE_CHUNKS):
            acc_ref[part, cc] = acc[:, cc * LANES:(cc + 1) * LANES]
        for c in range(4):
            rows = [acc_ref[part, cc, pl.ds(c, n4, stride=4), :] for cc in range(LANE_CHUNKS)]
            r4_ref[0, c] = jnp.concatenate(rows, axis=1).astype(BF16)
            for cc in range(LANE_CHUNKS):
                mod4_ref[part, cc, c * n4:(c + 1) * n4, :] = rows[cc]
        for c in range(16):
            rows = [mod4_ref[part, cc, pl.ds((c % 4) * n4 + c // 4, tm // 16, stride=4), :]
                    for cc in range(LANE_CHUNKS)]
            r16_ref[0, c] = jnp.concatenate(rows, axis=1).astype(BF16)


def _in_proj(x2d, g, w, b, s, *, tm):
    t, d = x2d.shape
    tn = B_WIDTH
    tiles = s // tm
    col = jnp.arange(IN_COLS, dtype=I32)[None, :]
    q_b = A_WIDTH + 2 * A_KV_WIDTH
    is_q = (col < A_WIDTH) | ((col >= q_b) & (col < q_b + B_WIDTH))
    col_scale = jnp.where(is_q, HEAD_DIM ** -0.5 * LOG2E, 1.0).astype(F32)
    res_shape = lambda r: jax.ShapeDtypeStruct((b, r, s // r, B_WIDTH), BF16)
    res_spec = lambda r: pl.BlockSpec((1, r, tm // r, B_WIDTH), lambda i: (i // tiles, 0, i % tiles, 0))
    return pl.pallas_call(
        _in_proj_kernel,
        grid=(t // tm,),
        in_specs=[
            pl.BlockSpec((tm, d), lambda i: (i, 0)),
            pl.BlockSpec((1, d), lambda i: (0, 0)),
            pl.BlockSpec((1, IN_COLS), lambda i: (0, 0)),
            pl.BlockSpec(memory_space=pl.ANY),
        ],
        out_specs=[pl.BlockSpec((tm, 2 * tn), lambda i: (i, 0)), pl.BlockSpec((tm, 3 * tn), lambda i: (i, 0))]
                  + [res_spec(4)] * 3 + [res_spec(16)] * 3,
        out_shape=[jax.ShapeDtypeStruct((t, 2 * tn), BF16), jax.ShapeDtypeStruct((t, 3 * tn), BF16)]
                  + [res_shape(4)] * 3 + [res_shape(16)] * 3,
        scratch_shapes=[pltpu.VMEM(w.shape, BF16), pltpu.VMEM((3, LANE_CHUNKS, tm, LANES), F32),
                        pltpu.VMEM((3, LANE_CHUNKS, tm, LANES), F32), pltpu.SemaphoreType.DMA(())],
        compiler_params=_cparams(1),
        name="in_proj",
    )(x2d, g.reshape(1, d), col_scale, w)


def _gate_proj_kernel(x_ref, g_ref, w_hbm_ref, b_ref, o_ref, w_ref, sem, *, tn):
    _load_resident(w_hbm_ref, w_ref, sem)
    h = _rms(x_ref[...], g_ref[...]).astype(BF16)
    for j in range(w_ref.shape[1] // tn):
        cols = slice(j * tn, (j + 1) * tn)
        acc = jnp.dot(h, w_ref[:, cols], preferred_element_type=F32)
        o_ref[:, cols] = jax.nn.sigmoid(acc + b_ref[:, cols]).astype(o_ref.dtype)


def _gate_proj(x2d, g, w, b, *, tm, tn):
    t, d = x2d.shape
    n = w.shape[1]
    return pl.pallas_call(
        functools.partial(_gate_proj_kernel, tn=tn),
        grid=(t // tm,),
        in_specs=[
            pl.BlockSpec((tm, d), lambda i: (i, 0)),
            pl.BlockSpec((1, d), lambda i: (0, 0)),
            pl.BlockSpec(memory_space=pl.ANY),
            pl.BlockSpec((1, n), lambda i: (0, 0)),
        ],
        out_specs=pl.BlockSpec((tm, n), lambda i: (i, 0)),
        out_shape=jax.ShapeDtypeStruct((t, n), BF16),
        scratch_shapes=[pltpu.VMEM(w.shape, BF16), pltpu.SemaphoreType.DMA(())],
        compiler_params=_cparams(1),
        name="gate_proj",
    )(x2d, g.reshape(1, d), w, b.reshape(1, n))


def _bucket_runs(step, max_off):
    max_exact = REL_BUCKETS // 2
    dist = np.arange(max_off + 1) * step
    buckets = []
    for ft in (np.float32, np.float64):
        df = np.maximum(dist, 1).astype(ft)
        large = max_exact + (np.log(df / ft(max_exact)) / ft(math.log(REL_MAX_DIST / max_exact))
                             * ft(REL_BUCKETS - max_exact)).astype(np.int32)
        buckets.append(np.where(dist < max_exact, dist, np.minimum(large, REL_BUCKETS - 1)))
    assert (buckets[0] == buckets[1]).all()
    runs = []
    for off, bucket in enumerate(buckets[0].tolist()):
        if not runs or runs[-1][1] != bucket:
            runs.append((off, bucket))
    return runs


def _fill_band_bias(bias_ref, table_ref, head0, n_heads, step, max_off):
    qi = lax.broadcasted_iota(I32, (BLOCK, 2 * BLOCK), 0)
    ki = lax.broadcasted_iota(I32, (BLOCK, 2 * BLOCK), 1)
    off = qi + BLOCK - ki
    runs = _bucket_runs(step, max_off)
    for h in range(n_heads):
        cur = jnp.full((BLOCK, 2 * BLOCK), NEG, F32)
        for first_off, bucket in runs:
            cur = jnp.where(off >= first_off, table_ref[bucket, head0 + h] * LOG2E, cur)
        rest = jnp.where(off > max_off, NEG, cur)
        bias_ref[1, h] = rest
        bias_ref[0, h] = jnp.where(ki >= BLOCK, rest, NEG)


def _softmax_pv(s, v_pair, sink):
    m = jnp.max(s, axis=-1, keepdims=True)
    if sink is not None:
        m = jnp.maximum(m, sink)
    p = jnp.exp2(s - m)
    l = jnp.sum(p, axis=-1, keepdims=True)
    if sink is not None:
        l = l + jnp.exp2(sink - m)
    o = jnp.dot(p.astype(BF16), v_pair, preferred_element_type=F32)
    return o, m, l


def _swa_kernel(sink_ref, table_ref, q_ref, kp_ref, kc_ref, vp_ref, vc_ref, o_ref, bias_ref):
    @pl.when((pl.program_id(0) == 0) & (pl.program_id(1) == 0))
    def _():
        _fill_band_bias(bias_ref, table_ref, 0, A_Q_HEADS, 1, A_WINDOW - 1)

    first = jnp.minimum(pl.program_id(1), 1)
    q_all = q_ref[0]
    k_all = jnp.concatenate([kp_ref[0], kc_ref[0]], axis=0)
    v_all = jnp.concatenate([vp_ref[0], vc_ref[0]], axis=0)
    lane = lax.broadcasted_iota(I32, (1, PAIR), 1)
    low = lane < HEAD_DIM
    for blk in range(Q_BLOCKS):
        rows = slice(blk * BLOCK, (blk + 1) * BLOCK)
        q = q_all[rows]
        k = k_all[blk * BLOCK:(blk + 2) * BLOCK]
        v = v_all[blk * BLOCK:(blk + 2) * BLOCK]
        variant = first if blk == 0 else 1
        for kv_pair in range(A_KV_HEADS // 2):
            k_pair = k[:, kv_pair * PAIR:(kv_pair + 1) * PAIR]
            v_pair = v[:, kv_pair * PAIR:(kv_pair + 1) * PAIR]
            k_half = (jnp.where(low, k_pair, jnp.zeros_like(k_pair)),
                      jnp.where(low, jnp.zeros_like(k_pair), k_pair))
            for g in range(A_GROUP):
                col = (kv_pair * A_GROUP + g) * PAIR
                q_pair = q[:, col:col + PAIR]
                outs = []
                for half in range(2):
                    head = (2 * kv_pair + half) * A_GROUP + g
                    s = _dot_nt(q_pair, k_half[half]) + bias_ref[variant, head]
                    o, _, l = _softmax_pv(s, v_pair, sink_ref[head] * LOG2E)
                    outs.append(o * (1.0 / l))
                o_ref[0, rows, col:col + PAIR] = jnp.where(low, outs[0], outs[1]).astype(o_ref.dtype)


def _swa_attention(qkv3, rel_bias, sinks):
    b, s, _ = qkv3.shape
    step = Q_BLOCKS * BLOCK
    kcol = A_WIDTH // A_KV_WIDTH
    prev = lambda i: jnp.maximum(i * Q_BLOCKS - 1, 0)
    return pl.pallas_call(
        _swa_kernel,
        grid=(b, s // step),
        in_specs=[
            pl.BlockSpec(memory_space=pltpu.SMEM),
            pl.BlockSpec(memory_space=pltpu.SMEM),
            pl.BlockSpec((1, step, A_WIDTH), lambda bi, i: (bi, i, 0)),
            pl.BlockSpec((1, BLOCK, A_KV_WIDTH), lambda bi, i: (bi, prev(i), kcol)),
            pl.BlockSpec((1, step, A_KV_WIDTH), lambda bi, i: (bi, i, kcol)),
            pl.BlockSpec((1, BLOCK, A_KV_WIDTH), lambda bi, i: (bi, prev(i), kcol + 1)),
            pl.BlockSpec((1, step, A_KV_WIDTH), lambda bi, i: (bi, i, kcol + 1)),
        ],
        out_specs=pl.BlockSpec((1, step, A_WIDTH), lambda bi, i: (bi, i, 0)),
        out_shape=jax.ShapeDtypeStruct((b, s, A_WIDTH), BF16),
        scratch_shapes=[pltpu.VMEM((2, A_Q_HEADS, BLOCK, 2 * BLOCK), F32)],
        compiler_params=_cparams(2),
        name="swa_attention",
    )(sinks, rel_bias, qkv3, qkv3, qkv3, qkv3, qkv3)


def _dilated_kernel(table_ref, q_ref, kp_ref, kc_ref, vp_ref, vc_ref, o_ref, lse_ref, bias_ref, *,
                    dilation, max_off):
    @pl.when((pl.program_id(0) == 0) & (pl.program_id(1) == 0) & (pl.program_id(2) == 0))
    def _():
        _fill_band_bias(bias_ref, table_ref, A_Q_HEADS, B_HEADS, dilation, max_off)

    first = jnp.minimum(pl.program_id(2), 1)
    q_all = q_ref[...]
    k_all = jnp.concatenate([kp_ref[...], kc_ref[...]], axis=0)
    v_all = jnp.concatenate([vp_ref[...], vc_ref[...]], axis=0)
    lane = lax.broadcasted_iota(I32, (1, PAIR), 1)
    low = lane < HEAD_DIM
    for blk in range(Q_BLOCKS):
        rows = slice(blk * BLOCK, (blk + 1) * BLOCK)
        variant = first if blk == 0 else 1
        lse_tile = jnp.zeros((BLOCK, LANES), F32)
        for pair in range(B_HEADS // 2):
            col = pair * PAIR
            q_pair = q_all[rows, col:col + PAIR]
            k_pair = k_all[blk * BLOCK:(blk + 2) * BLOCK, col:col + PAIR]
            v_pair = v_all[blk * BLOCK:(blk + 2) * BLOCK, col:col + PAIR]
            k_half = (jnp.where(low, k_pair, jnp.zeros_like(k_pair)),
                      jnp.where(low, jnp.zeros_like(k_pair), k_pair))
            outs = []
            for half in range(2):
                head = 2 * pair + half
                s = _dot_nt(q_pair, k_half[half]) + bias_ref[variant, head]
                o, m, l = _softmax_pv(s, v_pair, None)
                outs.append(o * (1.0 / l))
                lse_tile = jnp.where(lane == head, m * LN2 + jnp.log(l), lse_tile)
            o_ref[rows, col:col + PAIR] = jnp.where(low, outs[0], outs[1]).astype(o_ref.dtype)
        lse_ref[rows, :] = lse_tile


def _dilated_attention(q, k, v, cols, rel_bias, window):
    b, r, n, _ = q.shape
    step = Q_BLOCKS * BLOCK
    assert n % step == 0, (n, step)
    qc, kc, vc = cols
    prev = lambda j: jnp.maximum(j * Q_BLOCKS - 1, 0)
    cur = lambda col, w=B_WIDTH: pl.BlockSpec((None, None, step, w), lambda bi, c, j: (bi, c, j, col))
    old = lambda col: pl.BlockSpec((None, None, BLOCK, B_WIDTH), lambda bi, c, j: (bi, c, prev(j), col))
    return pl.pallas_call(
        functools.partial(_dilated_kernel, dilation=r, max_off=window // r),
        grid=(b, r, n // step),
        in_specs=[pl.BlockSpec(memory_space=pltpu.SMEM), cur(qc), old(kc), cur(kc), old(vc), cur(vc)],
        out_specs=[cur(0), cur(0, LANES)],
        out_shape=[jax.ShapeDtypeStruct((b, r, n, B_WIDTH), BF16),
                   jax.ShapeDtypeStruct((b, r, n, LANES), F32)],
        scratch_shapes=[pltpu.VMEM((2, B_HEADS, BLOCK, 2 * BLOCK), F32)],
        compiler_params=_cparams(3),
        name=f"dilated_attention_r{r}",
    )(rel_bias, q, k, k, v, v)


def _mix_out_kernel(x_ref, oa_ref, o1_ref, l1_ref, o4_ref, l4_ref, o16_ref, l16_ref, ga_ref, gb_ref,
                    wa_ref, wb_ref, wo_ref, ex_ref, out_ref, seq_ref):
    tm = x_ref.shape[0]
    seq = []
    base = 0
    for r, ref in ((4, o4_ref), (4, l4_ref), (16, o16_ref), (16, l16_ref)):
        chunks = ref.shape[-1] // LANES
        for c in range(r):
            rows = ref[0, c].astype(F32)
            for cc in range(chunks):
                seq_ref[base + cc, pl.ds(c, tm // r, stride=r), :] = rows[:, cc * LANES:(cc + 1) * LANES]
        seq.append(jnp.concatenate([seq_ref[base + cc] for cc in range(chunks)], axis=1))
        base += chunks
    o2, l2, o3, l3 = seq
    l1 = l1_ref[...]
    m = jnp.maximum(jnp.maximum(l1, l2), l3)
    e1, e2, e3 = jnp.exp(l1 - m), jnp.exp(l2 - m), jnp.exp(l3 - m)
    inv = 1.0 / (e1 + e2 + e3)

    def widen(w):
        hi = w.astype(BF16)
        lo = (w - hi.astype(F32)).astype(BF16)
        return (jnp.dot(hi, ex_ref[...], preferred_element_type=F32)
                + jnp.dot(lo, ex_ref[...], preferred_element_type=F32))

    ob = (widen(e1 * inv) * o1_ref[...].astype(F32) + widen(e2 * inv) * o2 + widen(e3 * inv) * o3)
    ya = jnp.dot(oa_ref[...], wa_ref[...], preferred_element_type=F32)
    yb = jnp.dot(ob.astype(BF16), wb_ref[...], preferred_element_type=F32)
    mixed = ga_ref[...].astype(F32) * ya + gb_ref[...].astype(F32) * yb
    out_ref[...] = x_ref[...] + jnp.dot(mixed.astype(BF16), wo_ref[...], preferred_element_type=F32)


def _mix_out(x2d, oa, o1, l1, o4, l4, o16, l16, gates, wa, wb, wo, s, *, tm):
    t, d = x2d.shape
    tiles = s // tm
    row = lambda w: pl.BlockSpec((tm, w), lambda i: (i, 0))
    res = lambda r, w: pl.BlockSpec((1, r, tm // r, w), lambda i: (i // tiles, 0, i % tiles, 0))
    const = lambda shape: pl.BlockSpec(shape, lambda i: (0, 0))
    head_of_col = jnp.arange(B_WIDTH, dtype=I32)[None, :] // HEAD_DIM
    expand = (jnp.arange(LANES, dtype=I32)[:, None] == head_of_col).astype(BF16)
    return pl.pallas_call(
        _mix_out_kernel,
        grid=(t // tm,),
        in_specs=[row(d), row(A_WIDTH), row(B_WIDTH), row(LANES), res(4, B_WIDTH), res(4, LANES),
                  res(16, B_WIDTH), res(16, LANES),
                  pl.BlockSpec((tm, d), lambda i: (i, 0)), pl.BlockSpec((tm, d), lambda i: (i, 1)),
                  const(wa.shape), const(wb.shape), const(wo.shape), const(expand.shape)],
        out_specs=row(d),
        out_shape=jax.ShapeDtypeStruct((t, d), F32),
        scratch_shapes=[pltpu.VMEM((2 * (LANE_CHUNKS + 1), tm, LANES), F32)],
        compiler_params=_cparams(1),
        name="mix_out",
    )(x2d, oa, o1, l1, o4, l4, o16, l16, gates, gates, wa, wb, wo, expand)


def _mem_kv_kernel(mem_ref, g_ref, wk_ref, wv_ref, k_ref, v_ref):
    hn = _rms(mem_ref[0], g_ref[...]).astype(BF16)
    k_ref[0] = jnp.dot(hn, wk_ref[...], preferred_element_type=F32).astype(BF16)
    v_ref[0] = jnp.dot(hn, wv_ref[...], preferred_element_type=F32).astype(BF16)


def _mem_kv(mem, g, wk, wv):
    b, ml, d = mem.shape
    const = lambda shape: pl.BlockSpec(shape, lambda bi: (0,) * len(shape))
    out = jax.ShapeDtypeStruct((b, ml, X_WIDTH), BF16)
    blk = pl.BlockSpec((1, ml, X_WIDTH), lambda bi: (bi, 0, 0))
    return pl.pallas_call(
        _mem_kv_kernel,
        grid=(b,),
        in_specs=[pl.BlockSpec((1, ml, d), lambda bi: (bi, 0, 0)), const((1, d)),
                  const(wk.shape), const(wv.shape)],
        out_specs=[blk, blk],
        out_shape=[out, out],
        compiler_params=_cparams(1),
        name="mem_kv",
    )(mem, g.reshape(1, d), wk, wv)


def _cross_kernel(x_ref, g_ref, wq_ref, k_ref, v_ref, wo_ref, out_ref):
    x = x_ref[0]
    h = _rms(x, g_ref[...]).astype(BF16)
    q = (jnp.dot(h, wq_ref[...], preferred_element_type=F32) * (X_HEAD_DIM ** -0.5)).astype(BF16)
    k, v = k_ref[0], v_ref[0]
    outs = []
    for hd in range(X_HEADS):
        sl = slice(hd * X_HEAD_DIM, (hd + 1) * X_HEAD_DIM)
        s = _dot_nt(q[:, sl], k[:, sl])
        m = jnp.max(s, axis=-1, keepdims=True)
        p = jnp.exp(s - m)
        l = jnp.sum(p, axis=-1, keepdims=True)
        o = jnp.dot(p.astype(BF16), v[:, sl], preferred_element_type=F32)
        outs.append((o * (1.0 / l)).astype(BF16))
    o = jnp.concatenate(outs, axis=1)
    out_ref[0] = x + jnp.dot(o, wo_ref[...], preferred_element_type=F32)


def _cross_attention(x3, g, wq, k, v, wo, *, tm):
    b, s, d = x3.shape
    ml = k.shape[1]
    const = lambda shape: pl.BlockSpec(shape, lambda bi, i: (0,) * len(shape))
    return pl.pallas_call(
        _cross_kernel,
        grid=(b, s // tm),
        in_specs=[pl.BlockSpec((1, tm, d), lambda bi, i: (bi, i, 0)), const((1, d)), const(wq.shape),
                  pl.BlockSpec((1, ml, X_WIDTH), lambda bi, i: (bi, 0, 0)),
                  pl.BlockSpec((1, ml, X_WIDTH), lambda bi, i: (bi, 0, 0)),
                  const(wo.shape)],
        out_specs=pl.BlockSpec((1, tm, d), lambda bi, i: (bi, i, 0)),
        out_shape=jax.ShapeDtypeStruct((b, s, d), F32),
        compiler_params=_cparams(2),
        name="cross_attention",
    )(x3, g.reshape(1, d), wq, k, v, wo)


def _pack_bf16_pairs(x):
    n = x.shape[1] // 2
    bits = lax.bitcast_convert_type(x.astype(BF16).astype(F32), U32)
    return (bits[:, :n] >> 16) | (bits[:, n:] & jnp.uint32(0xFFFF0000))


def _unpack_bf16_pairs(p):
    return (lax.bitcast_convert_type(p << 16, F32),
            lax.bitcast_convert_type(p & jnp.uint32(0xFFFF0000), F32))


def _first_argmax(vals, rows, n):
    m = jnp.max(vals, axis=0, keepdims=True)
    idx = jnp.min(jnp.where(vals == m, rows, n), axis=0, keepdims=True)
    return m, idx


def _router_kernel(x_ref, g_ref, whi_ref, wlo_ref, hp_ref, e_ref, gate_ref, cnt_ref):
    h = _rms(x_ref[...], g_ref[...])
    tm, d = h.shape
    h_hi = h.astype(BF16)
    h_hi32 = h_hi.astype(F32)
    h_lo = (h - h_hi32).astype(BF16)
    logits = (jnp.dot(h_hi, whi_ref[...], preferred_element_type=F32)
              + jnp.dot(h_lo, whi_ref[...], preferred_element_type=F32)
              + jnp.dot(h_hi, wlo_ref[...], preferred_element_type=F32)).T
    rows8 = lax.broadcasted_iota(I32, (EXP_PER_GROUP, tm), 0)
    gl = jnp.where(rows8 < N_GROUPS, logits[N_EXPERTS:N_EXPERTS + 8], -jnp.inf)
    gmax, gidx = _first_argmax(gl, rows8, 8)
    g_gate = 1.0 / jnp.sum(jnp.exp(gl - gmax), axis=0, keepdims=True)
    sel = jnp.zeros((EXP_PER_GROUP, tm), F32)
    for grp in range(N_GROUPS):
        sel = jnp.where(gidx == grp, logits[grp * EXP_PER_GROUP:(grp + 1) * EXP_PER_GROUP], sel)
    v1, i1 = _first_argmax(sel, rows8, 8)
    sel2 = jnp.where(rows8 == i1, -jnp.inf, sel)
    v2, i2 = _first_argmax(sel2, rows8, 8)
    e2x = jnp.exp(v2 - v1)
    den = 1.0 + e2x
    e_ref[...] = jnp.concatenate([gidx * EXP_PER_GROUP + i1, gidx * EXP_PER_GROUP + i2], axis=0)
    gate_ref[...] = jnp.concatenate([(1.0 / den) * g_gate, (e2x / den) * g_gate], axis=0)

    bits = lax.bitcast_convert_type(h_hi32, U32)
    packed = (bits[:, :d // 2] >> 16) | (bits[:, d // 2:] & jnp.uint32(0xFFFF0000))
    for c in range(PACKED_TILE_ROWS):
        hp_ref[pl.ds(c, tm, stride=PACKED_TILE_ROWS), :] = packed[:, c * LANES:(c + 1) * LANES]

    @pl.when(pl.program_id(0) == 0)
    def _():
        cnt_ref[...] = jnp.zeros_like(cnt_ref)

    rows32 = lax.broadcasted_iota(I32, (N_EXPERTS, tm), 0)
    e = e_ref[...]
    hits = (rows32 == e[0:1]).astype(F32) + (rows32 == e[1:2]).astype(F32)
    cnt_ref[...] += jnp.sum(hits, axis=1, keepdims=True)


def _router(x2d, g, w_router, *, tm):
    t, d = x2d.shape
    w_hi = w_router.astype(BF16)
    w_lo = (w_router - w_hi.astype(F32)).astype(BF16)
    return pl.pallas_call(
        _router_kernel,
        grid=(t // tm,),
        in_specs=[pl.BlockSpec((tm, d), lambda i: (i, 0)),
                  pl.BlockSpec((1, d), lambda i: (0, 0)),
                  pl.BlockSpec((d, LANES), lambda i: (0, 0)),
                  pl.BlockSpec((d, LANES), lambda i: (0, 0))],
        out_specs=[pl.BlockSpec((tm * PACKED_TILE_ROWS, LANES), lambda i: (i, 0)),
                   pl.BlockSpec((TOP_K, tm), lambda i: (0, i)),
                   pl.BlockSpec((TOP_K, tm), lambda i: (0, i)),
                   pl.BlockSpec((N_EXPERTS, 128), lambda i: (0, 0))],
        out_shape=[jax.ShapeDtypeStruct((t * PACKED_TILE_ROWS, LANES), U32),
                   jax.ShapeDtypeStruct((TOP_K, t), I32),
                   jax.ShapeDtypeStruct((TOP_K, t), F32),
                   jax.ShapeDtypeStruct((N_EXPERTS, 128), F32)],
        compiler_params=_cparams(1),
        name="moe_router",
    )(x2d, g.reshape(1, d), w_hi, w_lo)


def _dest_kernel(e_ref, start_ref, d_ref, carry_ref):
    @pl.when(pl.program_id(0) == 0)
    def _():
        carry_ref[...] = jnp.zeros_like(carry_ref)

    e = e_ref[...]
    tm = e.shape[1]
    rows32 = lax.broadcasted_iota(I32, (N_EXPERTS, tm), 0)
    oh0 = (rows32 == e[0:1]).astype(F32)
    oh1 = (rows32 == e[1:2]).astype(F32)
    hits = oh0 + oh1
    earlier = (lax.broadcasted_iota(I32, (tm, tm), 0) < lax.broadcasted_iota(I32, (tm, tm), 1))
    prefix = jnp.dot(hits.astype(BF16), earlier.astype(BF16), preferred_element_type=F32)
    base = prefix + carry_ref[:, 0:1] + start_ref[:, 0:1]
    d0 = jnp.sum(oh0 * base, axis=0, keepdims=True)
    d1 = jnp.sum(oh1 * base, axis=0, keepdims=True)
    d_ref[...] = jnp.concatenate([d0, d1], axis=0).astype(I32)
    carry_ref[...] += jnp.sum(hits, axis=1, keepdims=True)


def _assignment_rows(e_t, start, *, tm):
    t = e_t.shape[1]
    return pl.pallas_call(
        _dest_kernel,
        grid=(t // tm,),
        in_specs=[pl.BlockSpec((TOP_K, tm), lambda i: (0, i)),
                  pl.BlockSpec((N_EXPERTS, 128), lambda i: (0, 0))],
        out_specs=pl.BlockSpec((TOP_K, tm), lambda i: (0, i)),
        out_shape=jax.ShapeDtypeStruct((TOP_K, t), I32),
        scratch_shapes=[pltpu.VMEM((N_EXPERTS, 128), F32)],
        compiler_params=_cparams(1),
        name="moe_assignment_rows",
    )(e_t, start)


def _token_rows(ref, token, rows):
    return ref.at[pl.ds(pl.multiple_of(token * rows, rows), rows)]


def _dispatch_kernel(end_ref, pad_ref, nu_ref, d_ref, hp_ref, xg_ref, zero_ref, sem, zero_sem):
    rows = PACKED_TILE_ROWS
    tm = hp_ref.shape[0] // rows
    block_rows = MOE_ROWS * rows

    @pl.when(pl.program_id(0) == 0)
    def _():
        zero_ref[...] = jnp.zeros_like(zero_ref)

        def zero_block(block):
            start = pl.multiple_of(block * block_rows, block_rows)
            return pltpu.make_async_copy(zero_ref, xg_ref.at[pl.ds(start, block_rows)], zero_sem)

        for phase in ("start", "wait"):
            for e in range(N_EXPERTS):
                @pl.when(pad_ref[e] > 0)
                def _():
                    copy = zero_block(end_ref[e] // MOE_ROWS - 1)
                    copy.start() if phase == "start" else copy.wait()

        def start_tail(block, carry):
            zero_block(block).start()
            return carry

        def wait_tail(block, carry):
            zero_block(block).wait()
            return carry

        n_blocks = xg_ref.shape[0] // block_rows
        lax.fori_loop(nu_ref[0], n_blocks, start_tail, 0)
        lax.fori_loop(nu_ref[0], n_blocks, wait_tail, 0)

    for t in range(tm):
        for k in range(TOP_K):
            pltpu.make_async_copy(_token_rows(hp_ref, t, rows), _token_rows(xg_ref, d_ref[k, t], rows),
                                  sem).start(priority=k)
    for k in range(TOP_K):
        pltpu.make_async_copy(hp_ref, xg_ref.at[pl.ds(0, tm * rows)], sem).wait()


def _dispatch(seg_end, padded, n_used, dest, hp, n_rows, *, tm):
    rows = PACKED_TILE_ROWS
    t = hp.shape[0] // rows
    return pl.pallas_call(
        _dispatch_kernel,
        grid_spec=pltpu.PrefetchScalarGridSpec(
            num_scalar_prefetch=3,
            grid=(t // tm,),
            in_specs=[pl.BlockSpec((TOP_K, tm), lambda i, *_: (0, i), memory_space=pltpu.SMEM),
                      pl.BlockSpec((tm * rows, LANES), lambda i, *_: (i, 0))],
            out_specs=pl.BlockSpec(memory_space=pl.ANY),
            scratch_shapes=[pltpu.VMEM((MOE_ROWS * rows, LANES), U32), pltpu.SemaphoreType.DMA(()),
                            pltpu.SemaphoreType.DMA(())],
        ),
        out_shape=jax.ShapeDtypeStruct((n_rows * rows, LANES), U32),
        compiler_params=_cparams(1),
        name="moe_dispatch",
    )(seg_end, padded, n_used, dest, hp)


def _expert_kernel(be_ref, nu_ref, nxt_ref, par_ref, xg_ref, w1_ref, w3_ref, w2_ref, y_ref,
                   f1_ref, f3_ref, f2_ref, w1b_ref, w3b_ref, w2b_ref, sem):
    i = pl.program_id(0)

    def weight_copies(e, slot):
        return [pltpu.make_async_copy(w_ref.at[e], f_ref.at[slot], sem.at[slot, n])
                for n, (w_ref, f_ref) in enumerate(((w1_ref, f1_ref), (w3_ref, f3_ref), (w2_ref, f2_ref)))]

    @pl.when(i < nu_ref[0])
    def _():
        e = be_ref[i]
        slot = par_ref[i]

        @pl.when((i == 0) | (e != be_ref[jnp.maximum(i - 1, 0)]))
        def _():
            @pl.when(i == 0)
            def _():
                for copy in weight_copies(e, slot):
                    copy.start()

            for copy in weight_copies(e, slot):
                copy.wait()
            w1b_ref[...] = f1_ref[slot].astype(BF16)
            w3b_ref[...] = f3_ref[slot].astype(BF16)
            w2b_ref[...] = f2_ref[slot].astype(BF16)

            @pl.when(nxt_ref[i] != e)
            def _():
                for copy in weight_copies(nxt_ref[i], 1 - slot):
                    copy.start()

        halves = [_unpack_bf16_pairs(xg_ref[pl.ds(c, MOE_ROWS, stride=PACKED_TILE_ROWS), :])
                  for c in range(PACKED_TILE_ROWS)]
        x = jnp.concatenate([lo for lo, _ in halves] + [hi for _, hi in halves], axis=1).astype(BF16)
        h1 = jnp.dot(x, w1b_ref[...], preferred_element_type=F32)
        h3 = jnp.dot(x, w3b_ref[...], preferred_element_type=F32)
        a = (jax.nn.silu(h1) * h3).astype(BF16)
        y = _pack_bf16_pairs(jnp.dot(a, w2b_ref[...], preferred_element_type=F32))
        for c in range(PACKED_TILE_ROWS):
            y_ref[pl.ds(c, MOE_ROWS, stride=PACKED_TILE_ROWS), :] = y[:, c * LANES:(c + 1) * LANES]

    @pl.when(i >= nu_ref[0])
    def _():
        y_ref[...] = jnp.zeros_like(y_ref)


def _experts(block_e, n_used, next_e, parity, xg, w1, w3, w2):
    rows = PACKED_TILE_ROWS
    n_rows = xg.shape[0] // rows
    _, d, dff = w1.shape
    nblk = n_rows // MOE_ROWS
    live = lambda i, nu: jnp.maximum(jnp.minimum(i, nu[0] - 1), 0)
    any_space = pl.BlockSpec(memory_space=pl.ANY)
    return pl.pallas_call(
        _expert_kernel,
        grid_spec=pltpu.PrefetchScalarGridSpec(
            num_scalar_prefetch=4,
            grid=(nblk,),
            in_specs=[pl.BlockSpec((MOE_ROWS * rows, LANES), lambda i, be, nu, nx, par: (live(i, nu), 0)),
                      any_space, any_space, any_space],
            out_specs=pl.BlockSpec((MOE_ROWS * rows, LANES), lambda i, be, nu, nx, par: (i, 0)),
            scratch_shapes=[pltpu.VMEM((2, d, dff), F32), pltpu.VMEM((2, d, dff), F32),
                            pltpu.VMEM((2, dff, d), F32), pltpu.VMEM((d, dff), BF16),
                            pltpu.VMEM((d, dff), BF16), pltpu.VMEM((dff, d), BF16),
                            pltpu.SemaphoreType.DMA((2, 3))],
        ),
        out_shape=jax.ShapeDtypeStruct((n_rows * rows, LANES), U32),
        compiler_params=_cparams(1),
        name="moe_experts",
    )(block_e, n_used, next_e, parity, xg, w1, w3, w2)


def _combine_kernel(dc_ref, dn_ref, x_ref, gate_ref, gf_ref, y_ref, out_ref, buf, sem, *, final):
    i = pl.program_id(0)
    n = pl.num_programs(0)
    tm = x_ref.shape[0]
    rows = PACKED_TILE_ROWS

    def issue(d_ref, slot):
        for t in range(tm):
            for k in range(TOP_K):
                pltpu.make_async_copy(_token_rows(y_ref, d_ref[k, t], rows),
                                      _token_rows(buf.at[slot, k], t, rows),
                                      sem.at[slot]).start(priority=k)

    @pl.when(i == 0)
    def _():
        issue(dc_ref, 0)

    @pl.when(i + 1 < n)
    def _():
        issue(dn_ref, (i + 1) % 2)

    slot = i % 2
    for k in range(TOP_K):
        pltpu.make_async_copy(y_ref.at[pl.ds(0, tm * rows)], buf.at[slot, k], sem.at[slot]).wait()
    g = gate_ref[...]
    g0, g1 = g[:, 0:1], g[:, 1:2]
    low, high = [], []
    for c in range(rows):
        lo0, hi0 = _unpack_bf16_pairs(buf[slot, 0, pl.ds(c, tm, stride=rows), :])
        lo1, hi1 = _unpack_bf16_pairs(buf[slot, 1, pl.ds(c, tm, stride=rows), :])
        low.append(x_ref[:, c * LANES:(c + 1) * LANES] + (g0 * lo0 + g1 * lo1))
        high.append(x_ref[:, (rows + c) * LANES:(rows + c + 1) * LANES] + (g0 * hi0 + g1 * hi1))
    y = jnp.concatenate(low + high, axis=1)
    out_ref[...] = _rms(y, gf_ref[...]) if final else y


def _combine(dest, x2d, gates_tok, g_final, y, *, final, tm):
    t, d = x2d.shape
    nt = t // tm
    return pl.pallas_call(
        functools.partial(_combine_kernel, final=final),
        grid=(nt,),
        in_specs=[pl.BlockSpec((TOP_K, tm), lambda i: (0, i), memory_space=pltpu.SMEM),
                  pl.BlockSpec((TOP_K, tm), lambda i: (0, jnp.minimum(i + 1, nt - 1)),
                               memory_space=pltpu.SMEM),
                  pl.BlockSpec((tm, d), lambda i: (i, 0)),
                  pl.BlockSpec((tm, TOP_K), lambda i: (i, 0)),
                  pl.BlockSpec((1, d), lambda i: (0, 0)),
                  pl.BlockSpec(memory_space=pl.ANY)],
        out_specs=pl.BlockSpec((tm, d), lambda i: (i, 0)),
        out_shape=jax.ShapeDtypeStruct((t, d), F32),
        scratch_shapes=[pltpu.VMEM((2, TOP_K, tm * PACKED_TILE_ROWS, LANES), U32),
                        pltpu.SemaphoreType.DMA((2,))],
        compiler_params=_cparams(1),
        name="moe_combine",
    )(dest, dest, x2d, gates_tok, g_final.reshape(1, d), y)


def _swa_q_order():
    heads = []
    for kv_pair in range(A_KV_HEADS // 2):
        for g in range(A_GROUP):
            heads += [(2 * kv_pair) * A_GROUP + g, (2 * kv_pair + 1) * A_GROUP + g]
    return jnp.asarray([h * HEAD_DIM + c for h in heads for c in range(HEAD_DIM)], I32)


def kernel(x, mem, rel_bias, g_mix, w_in, sinks_a, w_a_out, w_b_out, w_gate, b_gate, w_o,
           g_x, g_mem, w_xq, w_xk, w_xv, w_xo, g_moe, w_rg, w_re, w1, w3, w2, g_final):
    b, s, d = x.shape
    t = b * s
    depth = g_mix.shape[0]
    perm = _swa_q_order()
    n_assign = t * TOP_K
    n_rows = -(-(n_assign + N_EXPERTS * (MOE_ROWS - 1)) // MOE_ROWS) * MOE_ROWS
    n_blocks = n_rows // MOE_ROWS

    x2d = x.reshape(t, d)
    for l in range(depth):
        w_in_l = jnp.concatenate([w_in[l][:, :A_WIDTH][:, perm], w_in[l][:, A_WIDTH:]], axis=1).astype(BF16)
        qkv_a, qkv_b, q4, k4, v4, q16, k16, v16 = _in_proj(x2d, g_mix[l], w_in_l, b, s, tm=512)
        gates = _gate_proj(x2d, g_mix[l], w_gate[l].astype(BF16), b_gate[l], tm=512, tn=1024)
        oa = _swa_attention(qkv_a.reshape(b, s, 2 * B_WIDTH), rel_bias, sinks_a[l]).reshape(t, A_WIDTH)
        qkv_b4 = qkv_b.reshape(b, 1, s, 3 * B_WIDTH)
        windows = {r: w for w, r in B_PATTERNS}
        o1, l1 = _dilated_attention(qkv_b4, qkv_b4, qkv_b4, (0, 1, 2), rel_bias, windows[1])
        o4, l4 = _dilated_attention(q4, k4, v4, (0, 0, 0), rel_bias, windows[4])
        o16, l16 = _dilated_attention(q16, k16, v16, (0, 0, 0), rel_bias, windows[16])
        x2d = _mix_out(x2d, oa, o1.reshape(t, B_WIDTH), l1.reshape(t, LANES), o4, l4, o16, l16, gates,
                       w_a_out[l][perm].astype(BF16), w_b_out[l].astype(BF16), w_o[l].astype(BF16),
                       s, tm=256)
        k_mem, v_mem = _mem_kv(mem, g_mem[l], w_xk[l].astype(BF16), w_xv[l].astype(BF16))
        x2d = _cross_attention(x2d.reshape(b, s, d), g_x[l], w_xq[l].astype(BF16), k_mem, v_mem,
                               w_xo[l].astype(BF16), tm=512).reshape(t, d)
        w_router = jnp.concatenate([w_re[l], w_rg[l],
                                    jnp.zeros((d, LANES - N_EXPERTS - N_GROUPS), F32)], axis=1)
        hp, e_t, gate_t, cnt = _router(x2d, g_moe[l], w_router, tm=512)
        counts = cnt[:, 0].astype(I32)
        padded = (counts + MOE_ROWS - 1) // MOE_ROWS * MOE_ROWS
        seg_end = jnp.cumsum(padded)
        seg_start = seg_end - padded
        n_used = (seg_end[-1] // MOE_ROWS).astype(I32).reshape(1)
        block_row = jnp.arange(n_blocks, dtype=I32) * MOE_ROWS
        block_e = jnp.minimum(jnp.sum(seg_end[None, :] <= block_row[:, None], axis=1),
                              N_EXPERTS - 1).astype(I32)
        start = jnp.broadcast_to(seg_start.astype(F32)[:, None], (N_EXPERTS, 128))
        dest = _assignment_rows(e_t, start, tm=512)
        xg = _dispatch(seg_end.astype(I32), padded.astype(I32), n_used, dest, hp, n_rows, tm=128)
        experts = jnp.arange(N_EXPERTS, dtype=I32)
        owns = padded > 0
        later = (experts[None, :] > experts[:, None]) & owns[None, :]
        next_owner = jnp.min(jnp.where(later, experts[None, :], N_EXPERTS), axis=1)
        next_owner = jnp.where(next_owner == N_EXPERTS, experts, next_owner).astype(I32)
        run_parity = ((jnp.cumsum(owns.astype(I32)) - 1) % 2).astype(I32)
        y = _experts(block_e, n_used, next_owner[block_e], run_parity[block_e], xg, w1[l], w3[l], w2[l])
        x2d = _combine(dest, x2d, gate_t.T, g_final, y, final=(l + 1 == depth), tm=256)
    return x2d.reshape(b, s, d)
```

```python
import functools
import math

import jax
import jax.numpy as jnp
import numpy as np
from jax import lax
from jax.experimental import pallas as pl
from jax.experimental.pallas import tpu as pltpu

F32 = jnp.float32
BF16 = jnp.bfloat16
I32 = jnp.int32
U32 = jnp.uint32

HEAD_DIM = 64
PAIR = 2 * HEAD_DIM
A_Q_HEADS = 16
A_KV_HEADS = 4
A_GROUP = A_Q_HEADS // A_KV_HEADS
A_WIDTH = A_Q_HEADS * HEAD_DIM
A_KV_WIDTH = A_KV_HEADS * HEAD_DIM
A_WINDOW = 128
B_HEADS = 12
B_WIDTH = B_HEADS * HEAD_DIM
LANES = 128
LANE_CHUNKS = B_WIDTH // LANES
D_MODEL = 2048
PACKED_TILE_ROWS = D_MODEL // 2 // LANES
B_PATTERNS = ((128, 1), (512, 4), (2048, 16))
BLOCK = 128
Q_BLOCKS = 8
IN_COLS = A_WIDTH + 2 * A_KV_WIDTH + 3 * B_WIDTH
REL_BUCKETS = 32
REL_MAX_DIST = 2048
X_HEADS = 4
X_HEAD_DIM = 128
X_WIDTH = X_HEADS * X_HEAD_DIM
N_GROUPS = 4
EXP_PER_GROUP = 8
N_EXPERTS = N_GROUPS * EXP_PER_GROUP
TOP_K = 2
D_FF = 512
EPS = 1e-6
NEG = -1e30
LOG2E = math.log2(math.e)
LN2 = math.log(2.0)
MOE_ROWS = 256
VMEM_LIMIT = 56 * 1024 * 1024


def _cparams(n_axes):
    return pltpu.CompilerParams(dimension_semantics=("arbitrary",) * n_axes,
                                vmem_limit_bytes=VMEM_LIMIT)


def _rms(xf, g):
    return xf * lax.rsqrt(jnp.mean(xf * xf, axis=-1, keepdims=True) + EPS) * g


def _dot_nt(a, b):
    return lax.dot_general(a, b, (((1,), (1,)), ((), ())), preferred_element_type=F32)


def _load_resident(w_hbm_ref, w_ref, sem):
    @pl.when(pl.program_id(0) == 0)
    def _():
        copy = pltpu.make_async_copy(w_hbm_ref, w_ref, sem)
        copy.start()
        copy.wait()


def _in_proj_kernel(x_ref, g_ref, cs_ref, w_hbm_ref, oa_ref, ob_ref, q4_ref, k4_ref, v4_ref, q16_ref,
                    k16_ref, v16_ref, w_ref, acc_ref, mod4_ref, sem):
    _load_resident(w_hbm_ref, w_ref, sem)
    tm = x_ref.shape[0]
    tn = B_WIDTH
    n4 = tm // 4
    h = _rms(x_ref[...], g_ref[...]).astype(BF16)
    for j in range(2):
        cols = slice(j * tn, (j + 1) * tn)
        acc = jnp.dot(h, w_ref[:, cols], preferred_element_type=F32)
        oa_ref[:, cols] = (acc * cs_ref[:, cols]).astype(BF16)
    for part, (r4_ref, r16_ref) in enumerate(((q4_ref, q16_ref), (k4_ref, k16_ref), (v4_ref, v16_ref))):
        cols = slice((2 + part) * tn, (3 + part) * tn)
        acc = jnp.dot(h, w_ref[:, cols], preferred_element_type=F32)
        if part == 0:
            acc = acc * cs_ref[:, cols]
        ob_ref[:, part * tn:(part + 1) * tn] = acc.astype(BF16)
        for cc in range(LANE_CHUNKS):
            acc_ref[part, cc] = acc[:, cc * LANES:(cc + 1) * LANES]
        for c in range(4):
            rows = [acc_ref[part, cc, pl.ds(c, n4, stride=4), :] for cc in range(LANE_CHUNKS)]
            r4_ref[0, c] = jnp.concatenate(rows, axis=1).astype(BF16)
            for cc in range(LANE_CHUNKS):
                mod4_ref[part, cc, c * n4:(c + 1) * n4, :] = rows[cc]
        for c in range(16):
            rows = [mod4_ref[part, cc, pl.ds((c % 4) * n4 + c // 4, tm // 16, stride=4), :]
                    for cc in range(LANE_CHUNKS)]
            r16_ref[0, c] = jnp.concatenate(rows, axis=1).astype(BF16)


def _in_proj(x2d, g, w, b, s, *, tm):
    t, d = x2d.shape
    tn = B_WIDTH
    tiles = s // tm
    col = jnp.arange(IN_COLS, dtype=I32)[None, :]
    q_b = A_WIDTH + 2 * A_KV_WIDTH
    is_q = (col < A_WIDTH) | ((col >= q_b) & (col < q_b + B_WIDTH))
    col_scale = jnp.where(is_q, HEAD_DIM ** -0.5 * LOG2E, 1.0).astype(F32)
    res_shape = lambda r: jax.ShapeDtypeStruct((b, r, s // r, B_WIDTH), BF16)
    res_spec = lambda r: pl.BlockSpec((1, r, tm // r, B_WIDTH), lambda i: (i // tiles, 0, i % tiles, 0))
    return pl.pallas_call(
        _in_proj_kernel,
        grid=(t // tm,),
        in_specs=[
            pl.BlockSpec((tm, d), lambda i: (i, 0)),
            pl.BlockSpec((1, d), lambda i: (0, 0)),
            pl.BlockSpec((1, IN_COLS), lambda i: (0, 0)),
            pl.BlockSpec(memory_space=pl.ANY),
        ],
        out_specs=[pl.BlockSpec((tm, 2 * tn), lambda i: (i, 0)), pl.BlockSpec((tm, 3 * tn), lambda i: (i, 0))]
                  + [res_spec(4)] * 3 + [res_spec(16)] * 3,
        out_shape=[jax.ShapeDtypeStruct((t, 2 * tn), BF16), jax.ShapeDtypeStruct((t, 3 * tn), BF16)]
                  + [res_shape(4)] * 3 + [res_shape(16)] * 3,
        scratch_shapes=[pltpu.VMEM(w.shape, BF16), pltpu.VMEM((3, LANE_CHUNKS, tm, LANES), F32),
                        pltpu.VMEM((3, LANE_CHUNKS, tm, LANES), F32), pltpu.SemaphoreType.DMA(())],
        compiler_params=_cparams(1),
        name="in_proj",
    )(x2d, g.reshape(1, d), col_scale, w)


def _gate_proj_kernel(x_ref, g_ref, w_hbm_ref, b_ref, o_ref, w_ref, sem, *, tn):
    _load_resident(w_hbm_ref, w_ref, sem)
    h = _rms(x_ref[...], g_ref[...]).astype(BF16)
    for j in range(w_ref.shape[1] // tn):
        cols = slice(j * tn, (j + 1) * tn)
        acc = jnp.dot(h, w_ref[:, cols], preferred_element_type=F32)
        o_ref[:, cols] = jax.nn.sigmoid(acc + b_ref[:, cols]).astype(o_ref.dtype)


def _gate_proj(x2d, g, w, b, *, tm, tn):
    t, d = x2d.shape
    n = w.shape[1]
    return pl.pallas_call(
        functools.partial(_gate_proj_kernel, tn=tn),
        grid=(t // tm,),
        in_specs=[
            pl.BlockSpec((tm, d), lambda i: (i, 0)),
            pl.BlockSpec((1, d), lambda i: (0, 0)),
            pl.BlockSpec(memory_space=pl.ANY),
            pl.BlockSpec((1, n), lambda i: (0, 0)),
        ],
        out_specs=pl.BlockSpec((tm, n), lambda i: (i, 0)),
        out_shape=jax.ShapeDtypeStruct((t, n), BF16),
        scratch_shapes=[pltpu.VMEM(w.shape, BF16), pltpu.SemaphoreType.DMA(())],
        compiler_params=_cparams(1),
        name="gate_proj",
    )(x2d, g.reshape(1, d), w, b.reshape(1, n))


def _bucket_runs(step, max_off):
    max_exact = REL_BUCKETS // 2
    dist = np.arange(max_off + 1) * step
    buckets = []
    for ft in (np.float32, np.float64):
        df = np.maximum(dist, 1).astype(ft)
        large = max_exact + (np.log(df / ft(max_exact)) / ft(math.log(REL_MAX_DIST / max_exact))
                             * ft(REL_BUCKETS - max_exact)).astype(np.int32)
        buckets.append(np.where(dist < max_exact, dist, np.minimum(large, REL_BUCKETS - 1)))
    assert (buckets[0] == buckets[1]).all()
    runs = []
    for off, bucket in enumerate(buckets[0].tolist()):
        if not runs or runs[-1][1] != bucket:
            runs.append((off, bucket))
    return runs


def _fill_band_bias(bias_ref, table_ref, head0, n_heads, step, max_off):
    qi = lax.broadcasted_iota(I32, (BLOCK, 2 * BLOCK), 0)
    ki = lax.broadcasted_iota(I32, (BLOCK, 2 * BLOCK), 1)
    off = qi + BLOCK - ki
    runs = _bucket_runs(step, max_off)
    for h in range(n_heads):
        cur = jnp.full((BLOCK, 2 * BLOCK), NEG, F32)
        for first_off, bucket in runs:
            cur = jnp.where(off >= first_off, table_ref[bucket, head0 + h] * LOG2E, cur)
        rest = jnp.where(off > max_off, NEG, cur)
        bias_ref[1, h] = rest
        bias_ref[0, h] = jnp.where(ki >= BLOCK, rest, NEG)


def _softmax_pv(s, v_pair, sink):
    m = jnp.max(s, axis=-1, keepdims=True)
    if sink is not None:
        m = jnp.maximum(m, sink)
    p = jnp.exp2(s - m)
    l = jnp.sum(p, axis=-1, keepdims=True)
    if sink is not None:
        l = l + jnp.exp2(sink - m)
    o = jnp.dot(p.astype(BF16), v_pair, preferred_element_type=F32)
    return o, m, l


def _swa_kernel(sink_ref, table_ref, q_ref, kp_ref, kc_ref, vp_ref, vc_ref, o_ref, bias_ref):
    @pl.when((pl.program_id(0) == 0) & (pl.program_id(1) == 0))
    def _():
        _fill_band_bias(bias_ref, table_ref, 0, A_Q_HEADS, 1, A_WINDOW - 1)

    first = jnp.minimum(pl.program_id(1), 1)
    q_all = q_ref[0]
    k_all = jnp.concatenate([kp_ref[0], kc_ref[0]], axis=0)
    v_all = jnp.concatenate([vp_ref[0], vc_ref[0]], axis=0)
    lane = lax.broadcasted_iota(I32, (1, PAIR), 1)
    low = lane < HEAD_DIM
    for blk in range(q_all.shape[0] // BLOCK):
        rows = slice(blk * BLOCK, (blk + 1) * BLOCK)
        q = q_all[rows]
        k = k_all[blk * BLOCK:(blk + 2) * BLOCK]
        v = v_all[blk * BLOCK:(blk + 2) * BLOCK]
        variant = first if blk == 0 else 1
        for kv_pair in range(A_KV_HEADS // 2):
            k_pair = k[:, kv_pair * PAIR:(kv_pair + 1) * PAIR]
            v_pair = v[:, kv_pair * PAIR:(kv_pair + 1) * PAIR]
            k_half = (jnp.where(low, k_pair, jnp.zeros_like(k_pair)),
                      jnp.where(low, jnp.zeros_like(k_pair), k_pair))
            for g in range(A_GROUP):
                col = (kv_pair * A_GROUP + g) * PAIR
                q_pair = q[:, col:col + PAIR]
                outs = []
                for half in range(2):
                    head = (2 * kv_pair + half) * A_GROUP + g
                    s = _dot_nt(q_pair, k_half[half]) + bias_ref[variant, head]
                    o, _, l = _softmax_pv(s, v_pair, sink_ref[head] * LOG2E)
                    outs.append(o * (1.0 / l))
                o_ref[0, rows, col:col + PAIR] = jnp.where(low, outs[0], outs[1]).astype(o_ref.dtype)


def _swa_attention(qkv3, rel_bias, sinks):
    b, s, _ = qkv3.shape
    q_blocks = min(Q_BLOCKS, s // BLOCK)
    step = q_blocks * BLOCK
    assert s % step == 0, (s, step)
    kcol = A_WIDTH // A_KV_WIDTH
    prev = lambda i: jnp.maximum(i * q_blocks - 1, 0)
    return pl.pallas_call(
        _swa_kernel,
        grid=(b, s // step),
        in_specs=[
            pl.BlockSpec(memory_space=pltpu.SMEM),
            pl.BlockSpec(memory_space=pltpu.SMEM),
            pl.BlockSpec((1, step, A_WIDTH), lambda bi, i: (bi, i, 0)),
            pl.BlockSpec((1, BLOCK, A_KV_WIDTH), lambda bi, i: (bi, prev(i), kcol)),
            pl.BlockSpec((1, step, A_KV_WIDTH), lambda bi, i: (bi, i, kcol)),
            pl.BlockSpec((1, BLOCK, A_KV_WIDTH), lambda bi, i: (bi, prev(i), kcol + 1)),
            pl.BlockSpec((1, step, A_KV_WIDTH), lambda bi, i: (bi, i, kcol + 1)),
        ],
        out_specs=pl.BlockSpec((1, step, A_WIDTH), lambda bi, i: (bi, i, 0)),
        out_shape=jax.ShapeDtypeStruct((b, s, A_WIDTH), BF16),
        scratch_shapes=[pltpu.VMEM((2, A_Q_HEADS, BLOCK, 2 * BLOCK), F32)],
        compiler_params=_cparams(2),
        name="swa_attention",
    )(sinks, rel_bias, qkv3, qkv3, qkv3, qkv3, qkv3)


def _dilated_kernel(table_ref, q_ref, kp_ref, kc_ref, vp_ref, vc_ref, o_ref, lse_ref, bias_ref, *,
                    dilation, max_off):
    @pl.when((pl.program_id(0) == 0) & (pl.program_id(1) == 0) & (pl.program_id(2) == 0))
    def _():
        _fill_band_bias(bias_ref, table_ref, A_Q_HEADS, B_HEADS, dilation, max_off)

    first = jnp.minimum(pl.program_id(2), 1)
    q_all = q_ref[...]
    k_all = jnp.concatenate([kp_ref[...], kc_ref[...]], axis=0)
    v_all = jnp.concatenate([vp_ref[...], vc_ref[...]], axis=0)
    lane = lax.broadcasted_iota(I32, (1, PAIR), 1)
    low = lane < HEAD_DIM
    for blk in range(q_all.shape[0] // BLOCK):
        rows = slice(blk * BLOCK, (blk + 1) * BLOCK)
        variant = first if blk == 0 else 1
        lse_tile = jnp.zeros((BLOCK, LANES), F32)
        for pair in range(B_HEADS // 2):
            col = pair * PAIR
            q_pair = q_all[rows, col:col + PAIR]
            k_pair = k_all[blk * BLOCK:(blk + 2) * BLOCK, col:col + PAIR]
            v_pair = v_all[blk * BLOCK:(blk + 2) * BLOCK, col:col + PAIR]
            k_half = (jnp.where(low, k_pair, jnp.zeros_like(k_pair)),
                      jnp.where(low, jnp.zeros_like(k_pair), k_pair))
            outs = []
            for half in range(2):
                head = 2 * pair + half
                s = _dot_nt(q_pair, k_half[half]) + bias_ref[variant, head]
                o, m, l = _softmax_pv(s, v_pair, None)
                outs.append(o * (1.0 / l))
                lse_tile = jnp.where(lane == head, m * LN2 + jnp.log(l), lse_tile)
            o_ref[rows, col:col + PAIR] = jnp.where(low, outs[0], outs[1]).astype(o_ref.dtype)
        lse_ref[rows, :] = lse_tile


def _dilated_attention(q, k, v, cols, rel_bias, window):
    b, r, n, _ = q.shape
    q_blocks = min(Q_BLOCKS, n // BLOCK)
    step = q_blocks * BLOCK
    assert n % step == 0, (n, step)
    qc, kc, vc = cols
    prev = lambda j: jnp.maximum(j * q_blocks - 1, 0)
    cur = lambda col, w=B_WIDTH: pl.BlockSpec((None, None, step, w), lambda bi, c, j: (bi, c, j, col))
    old = lambda col: pl.BlockSpec((None, None, BLOCK, B_WIDTH), lambda bi, c, j: (bi, c, prev(j), col))
    return pl.pallas_call(
        functools.partial(_dilated_kernel, dilation=r, max_off=window // r),
        grid=(b, r, n // step),
        in_specs=[pl.BlockSpec(memory_space=pltpu.SMEM), cur(qc), old(kc), cur(kc), old(vc), cur(vc)],
        out_specs=[cur(0), cur(0, LANES)],
        out_shape=[jax.ShapeDtypeStruct((b, r, n, B_WIDTH), BF16),
                   jax.ShapeDtypeStruct((b, r, n, LANES), F32)],
        scratch_shapes=[pltpu.VMEM((2, B_HEADS, BLOCK, 2 * BLOCK), F32)],
        compiler_params=_cparams(3),
        name=f"dilated_attention_r{r}",
    )(rel_bias, q, k, k, v, v)


def _mix_out_kernel(x_ref, oa_ref, o1_ref, l1_ref, o4_ref, l4_ref, o16_ref, l16_ref, ga_ref, gb_ref,
                    wa_ref, wb_ref, wo_ref, ex_ref, out_ref, seq_ref):
    tm = x_ref.shape[0]
    seq = []
    base = 0
    for r, ref in ((4, o4_ref), (4, l4_ref), (16, o16_ref), (16, l16_ref)):
        chunks = ref.shape[-1] // LANES
        for c in range(r):
            rows = ref[0, c].astype(F32)
            for cc in range(chunks):
                seq_ref[base + cc, pl.ds(c, tm // r, stride=r), :] = rows[:, cc * LANES:(cc + 1) * LANES]
        seq.append(jnp.concatenate([seq_ref[base + cc] for cc in range(chunks)], axis=1))
        base += chunks
    o2, l2, o3, l3 = seq
    l1 = l1_ref[...]
    m = jnp.maximum(jnp.maximum(l1, l2), l3)
    e1, e2, e3 = jnp.exp(l1 - m), jnp.exp(l2 - m), jnp.exp(l3 - m)
    inv = 1.0 / (e1 + e2 + e3)

    def widen(w):
        hi = w.astype(BF16)
        lo = (w - hi.astype(F32)).astype(BF16)
        return (jnp.dot(hi, ex_ref[...], preferred_element_type=F32)
                + jnp.dot(lo, ex_ref[...], preferred_element_type=F32))

    ob = (widen(e1 * inv) * o1_ref[...].astype(F32) + widen(e2 * inv) * o2 + widen(e3 * inv) * o3)
    ya = jnp.dot(oa_ref[...], wa_ref[...], preferred_element_type=F32)
    yb = jnp.dot(ob.astype(BF16), wb_ref[...], preferred_element_type=F32)
    mixed = ga_ref[...].astype(F32) * ya + gb_ref[...].astype(F32) * yb
    out_ref[...] = x_ref[...] + jnp.dot(mixed.astype(BF16), wo_ref[...], preferred_element_type=F32)


def _mix_out(x2d, oa, o1, l1, o4, l4, o16, l16, gates, wa, wb, wo, s, *, tm):
    t, d = x2d.shape
    tiles = s // tm
    row = lambda w: pl.BlockSpec((tm, w), lambda i: (i, 0))
    res = lambda r, w: pl.BlockSpec((1, r, tm // r, w), lambda i: (i // tiles, 0, i % tiles, 0))
    const = lambda shape: pl.BlockSpec(shape, lambda i: (0, 0))
    head_of_col = jnp.arange(B_WIDTH, dtype=I32)[None, :] // HEAD_DIM
    expand = (jnp.arange(LANES, dtype=I32)[:, None] == head_of_col).astype(BF16)
    return pl.pallas_call(
        _mix_out_kernel,
        grid=(t // tm,),
        in_specs=[row(d), row(A_WIDTH), row(B_WIDTH), row(LANES), res(4, B_WIDTH), res(4, LANES),
                  res(16, B_WIDTH), res(16, LANES),
                  pl.BlockSpec((tm, d), lambda i: (i, 0)), pl.BlockSpec((tm, d), lambda i: (i, 1)),
                  const(wa.shape), const(wb.shape), const(wo.shape), const(expand.shape)],
        out_specs=row(d),
        out_shape=jax.ShapeDtypeStruct((t, d), F32),
        scratch_shapes=[pltpu.VMEM((2 * (LANE_CHUNKS + 1), tm, LANES), F32)],
        compiler_params=_cparams(1),
        name="mix_out",
    )(x2d, oa, o1, l1, o4, l4, o16, l16, gates, gates, wa, wb, wo, expand)


def _mem_kv_kernel(mem_ref, g_ref, wk_ref, wv_ref, k_ref, v_ref):
    hn = _rms(mem_ref[0], g_ref[...]).astype(BF16)
    k_ref[0] = jnp.dot(hn, wk_ref[...], preferred_element_type=F32).astype(BF16)
    v_ref[0] = jnp.dot(hn, wv_ref[...], preferred_element_type=F32).astype(BF16)


def _mem_kv(mem, g, wk, wv):
    b, ml, d = mem.shape
    const = lambda shape: pl.BlockSpec(shape, lambda bi: (0,) * len(shape))
    out = jax.ShapeDtypeStruct((b, ml, X_WIDTH), BF16)
    blk = pl.BlockSpec((1, ml, X_WIDTH), lambda bi: (bi, 0, 0))
    return pl.pallas_call(
        _mem_kv_kernel,
        grid=(b,),
        in_specs=[pl.BlockSpec((1, ml, d), lambda bi: (bi, 0, 0)), const((1, d)),
                  const(wk.shape), const(wv.shape)],
        out_specs=[blk, blk],
        out_shape=[out, out],
        compiler_params=_cparams(1),
        name="mem_kv",
    )(mem, g.reshape(1, d), wk, wv)


def _cross_kernel(x_ref, g_ref, wq_ref, k_ref, v_ref, wo_ref, out_ref):
    x = x_ref[0]
    h = _rms(x, g_ref[...]).astype(BF16)
    q = (jnp.dot(h, wq_ref[...], preferred_element_type=F32) * (X_HEAD_DIM ** -0.5)).astype(BF16)
    k, v = k_ref[0], v_ref[0]
    outs = []
    for hd in range(X_HEADS):
        sl = slice(hd * X_HEAD_DIM, (hd + 1) * X_HEAD_DIM)
        s = _dot_nt(q[:, sl], k[:, sl])
        m = jnp.max(s, axis=-1, keepdims=True)
        p = jnp.exp(s - m)
        l = jnp.sum(p, axis=-1, keepdims=True)
        o = jnp.dot(p.astype(BF16), v[:, sl], preferred_element_type=F32)
        outs.append((o * (1.0 / l)).astype(BF16))
    o = jnp.concatenate(outs, axis=1)
    out_ref[0] = x + jnp.dot(o, wo_ref[...], preferred_element_type=F32)


def _cross_attention(x3, g, wq, k, v, wo, *, tm):
    b, s, d = x3.shape
    ml = k.shape[1]
    const = lambda shape: pl.BlockSpec(shape, lambda bi, i: (0,) * len(shape))
    return pl.pallas_call(
        _cross_kernel,
        grid=(b, s // tm),
        in_specs=[pl.BlockSpec((1, tm, d), lambda bi, i: (bi, i, 0)), const((1, d)), const(wq.shape),
                  pl.BlockSpec((1, ml, X_WIDTH), lambda bi, i: (bi, 0, 0)),
                  pl.BlockSpec((1, ml, X_WIDTH), lambda bi, i: (bi, 0, 0)),
                  const(wo.shape)],
        out_specs=pl.BlockSpec((1, tm, d), lambda bi, i: (bi, i, 0)),
        out_shape=jax.ShapeDtypeStruct((b, s, d), F32),
        compiler_params=_cparams(2),
        name="cross_attention",
    )(x3, g.reshape(1, d), wq, k, v, wo)


def _pack_bf16_pairs(x):
    n = x.shape[1] // 2
    bits = lax.bitcast_convert_type(x.astype(BF16).astype(F32), U32)
    return (bits[:, :n] >> 16) | (bits[:, n:] & jnp.uint32(0xFFFF0000))


def _unpack_bf16_pairs(p):
    return (lax.bitcast_convert_type(p << 16, F32),
            lax.bitcast_convert_type(p & jnp.uint32(0xFFFF0000), F32))


def _first_argmax(vals, rows, n):
    m = jnp.max(vals, axis=0, keepdims=True)
    idx = jnp.min(jnp.where(vals == m, rows, n), axis=0, keepdims=True)
    return m, idx


def _router_kernel(x_ref, g_ref, whi_ref, wlo_ref, hp_ref, e_ref, gate_ref, cnt_ref):
    h = _rms(x_ref[...], g_ref[...])
    tm, d = h.shape
    h_hi = h.astype(BF16)
    h_hi32 = h_hi.astype(F32)
    h_lo = (h - h_hi32).astype(BF16)
    logits = (jnp.dot(h_hi, whi_ref[...], preferred_element_type=F32)
              + jnp.dot(h_lo, whi_ref[...], preferred_element_type=F32)
              + jnp.dot(h_hi, wlo_ref[...], preferred_element_type=F32)).T
    rows8 = lax.broadcasted_iota(I32, (EXP_PER_GROUP, tm), 0)
    gl = jnp.where(rows8 < N_GROUPS, logits[N_EXPERTS:N_EXPERTS + 8], -jnp.inf)
    gmax, gidx = _first_argmax(gl, rows8, 8)
    g_gate = 1.0 / jnp.sum(jnp.exp(gl - gmax), axis=0, keepdims=True)
    sel = jnp.zeros((EXP_PER_GROUP, tm), F32)
    for grp in range(N_GROUPS):
        sel = jnp.where(gidx == grp, logits[grp * EXP_PER_GROUP:(grp + 1) * EXP_PER_GROUP], sel)
    v1, i1 = _first_argmax(sel, rows8, 8)
    sel2 = jnp.where(rows8 == i1, -jnp.inf, sel)
    v2, i2 = _first_argmax(sel2, rows8, 8)
    e2x = jnp.exp(v2 - v1)
    den = 1.0 + e2x
    e_ref[...] = jnp.concatenate([gidx * EXP_PER_GROUP + i1, gidx * EXP_PER_GROUP + i2], axis=0)
    gate_ref[...] = jnp.concatenate([(1.0 / den) * g_gate, (e2x / den) * g_gate], axis=0)

    bits = lax.bitcast_convert_type(h_hi32, U32)
    packed = (bits[:, :d // 2] >> 16) | (bits[:, d // 2:] & jnp.uint32(0xFFFF0000))
    for c in range(PACKED_TILE_ROWS):
        hp_ref[pl.ds(c, tm, stride=PACKED_TILE_ROWS), :] = packed[:, c * LANES:(c + 1) * LANES]

    @pl.when(pl.program_id(0) == 0)
    def _():
        cnt_ref[...] = jnp.zeros_like(cnt_ref)

    rows32 = lax.broadcasted_iota(I32, (N_EXPERTS, tm), 0)
    e = e_ref[...]
    hits = (rows32 == e[0:1]).astype(F32) + (rows32 == e[1:2]).astype(F32)
    cnt_ref[...] += jnp.sum(hits, axis=1, keepdims=True)


def _router(x2d, g, w_router, *, tm):
    t, d = x2d.shape
    w_hi = w_router.astype(BF16)
    w_lo = (w_router - w_hi.astype(F32)).astype(BF16)
    return pl.pallas_call(
        _router_kernel,
        grid=(t // tm,),
        in_specs=[pl.BlockSpec((tm, d), lambda i: (i, 0)),
                  pl.BlockSpec((1, d), lambda i: (0, 0)),
                  pl.BlockSpec((d, LANES), lambda i: (0, 0)),
                  pl.BlockSpec((d, LANES), lambda i: (0, 0))],
        out_specs=[pl.BlockSpec((tm * PACKED_TILE_ROWS, LANES), lambda i: (i, 0)),
                   pl.BlockSpec((TOP_K, tm), lambda i: (0, i)),
                   pl.BlockSpec((TOP_K, tm), lambda i: (0, i)),
                   pl.BlockSpec((N_EXPERTS, 128), lambda i: (0, 0))],
        out_shape=[jax.ShapeDtypeStruct((t * PACKED_TILE_ROWS, LANES), U32),
                   jax.ShapeDtypeStruct((TOP_K, t), I32),
                   jax.ShapeDtypeStruct((TOP_K, t), F32),
                   jax.ShapeDtypeStruct((N_EXPERTS, 128), F32)],
        compiler_params=_cparams(1),
        name="moe_router",
    )(x2d, g.reshape(1, d), w_hi, w_lo)


def _dest_kernel(e_ref, start_ref, d_ref, carry_ref):
    @pl.when(pl.program_id(0) == 0)
    def _():
        carry_ref[...] = jnp.zeros_like(carry_ref)

    e = e_ref[...]
    tm = e.shape[1]
    rows32 = lax.broadcasted_iota(I32, (N_EXPERTS, tm), 0)
    oh0 = (rows32 == e[0:1]).astype(F32)
    oh1 = (rows32 == e[1:2]).astype(F32)
    hits = oh0 + oh1
    earlier = (lax.broadcasted_iota(I32, (tm, tm), 0) < lax.broadcasted_iota(I32, (tm, tm), 1))
    prefix = jnp.dot(hits.astype(BF16), earlier.astype(BF16), preferred_element_type=F32)
    base = prefix + carry_ref[:, 0:1] + start_ref[:, 0:1]
    d0 = jnp.sum(oh0 * base, axis=0, keepdims=True)
    d1 = jnp.sum(oh1 * base, axis=0, keepdims=True)
    d_ref[...] = jnp.concatenate([d0, d1], axis=0).astype(I32)
    carry_ref[...] += jnp.sum(hits, axis=1, keepdims=True)


def _assignment_rows(e_t, start, *, tm):
    t = e_t.shape[1]
    return pl.pallas_call(
        _dest_kernel,
        grid=(t // tm,),
        in_specs=[pl.BlockSpec((TOP_K, tm), lambda i: (0, i)),
                  pl.BlockSpec((N_EXPERTS, 128), lambda i: (0, 0))],
        out_specs=pl.BlockSpec((TOP_K, tm), lambda i: (0, i)),
        out_shape=jax.ShapeDtypeStruct((TOP_K, t), I32),
        scratch_shapes=[pltpu.VMEM((N_EXPERTS, 128), F32)],
        compiler_params=_cparams(1),
        name="moe_assignment_rows",
    )(e_t, start)


def _token_rows(ref, token, rows):
    return ref.at[pl.ds(pl.multiple_of(token * rows, rows), rows)]


def _dispatch_kernel(end_ref, pad_ref, nu_ref, d_ref, hp_ref, xg_ref, zero_ref, sem, zero_sem):
    rows = PACKED_TILE_ROWS
    tm = hp_ref.shape[0] // rows
    block_rows = MOE_ROWS * rows

    @pl.when(pl.program_id(0) == 0)
    def _():
        zero_ref[...] = jnp.zeros_like(zero_ref)

        def zero_block(block):
            start = pl.multiple_of(block * block_rows, block_rows)
            return pltpu.make_async_copy(zero_ref, xg_ref.at[pl.ds(start, block_rows)], zero_sem)

        for phase in ("start", "wait"):
            for e in range(N_EXPERTS):
                @pl.when(pad_ref[e] > 0)
                def _():
                    copy = zero_block(end_ref[e] // MOE_ROWS - 1)
                    copy.start() if phase == "start" else copy.wait()

        def start_tail(block, carry):
            zero_block(block).start()
            return carry

        def wait_tail(block, carry):
            zero_block(block).wait()
            return carry

        n_blocks = xg_ref.shape[0] // block_rows
        lax.fori_loop(nu_ref[0], n_blocks, start_tail, 0)
        lax.fori_loop(nu_ref[0], n_blocks, wait_tail, 0)

    for t in range(tm):
        for k in range(TOP_K):
            pltpu.make_async_copy(_token_rows(hp_ref, t, rows), _token_rows(xg_ref, d_ref[k, t], rows),
                                  sem).start(priority=k)
    for k in range(TOP_K):
        pltpu.make_async_copy(hp_ref, xg_ref.at[pl.ds(0, tm * rows)], sem).wait()


def _dispatch(seg_end, padded, n_used, dest, hp, n_rows, *, tm):
    rows = PACKED_TILE_ROWS
    t = hp.shape[0] // rows
    return pl.pallas_call(
        _dispatch_kernel,
        grid_spec=pltpu.PrefetchScalarGridSpec(
            num_scalar_prefetch=3,
            grid=(t // tm,),
            in_specs=[pl.BlockSpec((TOP_K, tm), lambda i, *_: (0, i), memory_space=pltpu.SMEM),
                      pl.BlockSpec((tm * rows, LANES), lambda i, *_: (i, 0))],
            out_specs=pl.BlockSpec(memory_space=pl.ANY),
            scratch_shapes=[pltpu.VMEM((MOE_ROWS * rows, LANES), U32), pltpu.SemaphoreType.DMA(()),
                            pltpu.SemaphoreType.DMA(())],
        ),
        out_shape=jax.ShapeDtypeStruct((n_rows * rows, LANES), U32),
        compiler_params=_cparams(1),
        name="moe_dispatch",
    )(seg_end, padded, n_used, dest, hp)


def _expert_kernel(be_ref, nu_ref, nxt_ref, par_ref, xg_ref, w1_ref, w3_ref, w2_ref, y_ref,
                   f1_ref, f3_ref, f2_ref, w1b_ref, w3b_ref, w2b_ref, sem):
    i = pl.program_id(0)

    def weight_copies(e, slot):
        return [pltpu.make_async_copy(w_ref.at[e], f_ref.at[slot], sem.at[slot, n])
                for n, (w_ref, f_ref) in enumerate(((w1_ref, f1_ref), (w3_ref, f3_ref), (w2_ref, f2_ref)))]

    @pl.when(i < nu_ref[0])
    def _():
        e = be_ref[i]
        slot = par_ref[i]

        @pl.when((i == 0) | (e != be_ref[jnp.maximum(i - 1, 0)]))
        def _():
            @pl.when(i == 0)
            def _():
                for copy in weight_copies(e, slot):
                    copy.start()

            for copy in weight_copies(e, slot):
                copy.wait()
            w1b_ref[...] = f1_ref[slot].astype(BF16)
            w3b_ref[...] = f3_ref[slot].astype(BF16)
            w2b_ref[...] = f2_ref[slot].astype(BF16)

            @pl.when(nxt_ref[i] != e)
            def _():
                for copy in weight_copies(nxt_ref[i], 1 - slot):
                    copy.start()

        halves = [_unpack_bf16_pairs(xg_ref[pl.ds(c, MOE_ROWS, stride=PACKED_TILE_ROWS), :])
                  for c in range(PACKED_TILE_ROWS)]
        x = jnp.concatenate([lo for lo, _ in halves] + [hi for _, hi in halves], axis=1).astype(BF16)
        h1 = jnp.dot(x, w1b_ref[...], preferred_element_type=F32)
        h3 = jnp.dot(x, w3b_ref[...], preferred_element_type=F32)
        a = (jax.nn.silu(h1) * h3).astype(BF16)
        y = _pack_bf16_pairs(jnp.dot(a, w2b_ref[...], preferred_element_type=F32))
        for c in range(PACKED_TILE_ROWS):
            y_ref[pl.ds(c, MOE_ROWS, stride=PACKED_TILE_ROWS), :] = y[:, c * LANES:(c + 1) * LANES]

    @pl.when(i >= nu_ref[0])
    def _():
        y_ref[...] = jnp.zeros_like(y_ref)


def _experts(block_e, n_used, next_e, parity, xg, w1, w3, w2):
    rows = PACKED_TILE_ROWS
    n_rows = xg.shape[0] // rows
    _, d, dff = w1.shape
    nblk = n_rows // MOE_ROWS
    live = lambda i, nu: jnp.maximum(jnp.minimum(i, nu[0] - 1), 0)
    any_space = pl.BlockSpec(memory_space=pl.ANY)
    return pl.pallas_call(
        _expert_kernel,
        grid_spec=pltpu.PrefetchScalarGridSpec(
            num_scalar_prefetch=4,
            grid=(nblk,),
            in_specs=[pl.BlockSpec((MOE_ROWS * rows, LANES), lambda i, be, nu, nx, par: (live(i, nu), 0)),
                      any_space, any_space, any_space],
            out_specs=pl.BlockSpec((MOE_ROWS * rows, LANES), lambda i, be, nu, nx, par: (i, 0)),
            scratch_shapes=[pltpu.VMEM((2, d, dff), F32), pltpu.VMEM((2, d, dff), F32),
                            pltpu.VMEM((2, dff, d), F32), pltpu.VMEM((d, dff), BF16),
                            pltpu.VMEM((d, dff), BF16), pltpu.VMEM((dff, d), BF16),
                            pltpu.SemaphoreType.DMA((2, 3))],
        ),
        out_shape=jax.ShapeDtypeStruct((n_rows * rows, LANES), U32),
        compiler_params=_cparams(1),
        name="moe_experts",
    )(block_e, n_used, next_e, parity, xg, w1, w3, w2)


def _combine_kernel(dc_ref, dn_ref, x_ref, gate_ref, gf_ref, y_ref, out_ref, buf, sem, *, final):
    i = pl.program_id(0)
    n = pl.num_programs(0)
    tm = x_ref.shape[0]
    rows = PACKED_TILE_ROWS

    def issue(d_ref, slot):
        for t in range(tm):
            for k in range(TOP_K):
                pltpu.make_async_copy(_token_rows(y_ref, d_ref[k, t], rows),
                                      _token_rows(buf.at[slot, k], t, rows),
                                      sem.at[slot]).start(priority=k)

    @pl.when(i == 0)
    def _():
        issue(dc_ref, 0)

    @pl.when(i + 1 < n)
    def _():
        issue(dn_ref, (i + 1) % 2)

    slot = i % 2
    for k in range(TOP_K):
        pltpu.make_async_copy(y_ref.at[pl.ds(0, tm * rows)], buf.at[slot, k], sem.at[slot]).wait()
    g = gate_ref[...]
    g0, g1 = g[:, 0:1], g[:, 1:2]
    low, high = [], []
    for c in range(rows):
        lo0, hi0 = _unpack_bf16_pairs(buf[slot, 0, pl.ds(c, tm, stride=rows), :])
        lo1, hi1 = _unpack_bf16_pairs(buf[slot, 1, pl.ds(c, tm, stride=rows), :])
        low.append(x_ref[:, c * LANES:(c + 1) * LANES] + (g0 * lo0 + g1 * lo1))
        high.append(x_ref[:, (rows + c) * LANES:(rows + c + 1) * LANES] + (g0 * hi0 + g1 * hi1))
    y = jnp.concatenate(low + high, axis=1)
    out_ref[...] = _rms(y, gf_ref[...]) if final else y


def _combine(dest, x2d, gates_tok, g_final, y, *, final, tm):
    t, d = x2d.shape
    nt = t // tm
    return pl.pallas_call(
        functools.partial(_combine_kernel, final=final),
        grid=(nt,),
        in_specs=[pl.BlockSpec((TOP_K, tm), lambda i: (0, i), memory_space=pltpu.SMEM),
                  pl.BlockSpec((TOP_K, tm), lambda i: (0, jnp.minimum(i + 1, nt - 1)),
                               memory_space=pltpu.SMEM),
                  pl.BlockSpec((tm, d), lambda i: (i, 0)),
                  pl.BlockSpec((tm, TOP_K), lambda i: (i, 0)),
                  pl.BlockSpec((1, d), lambda i: (0, 0)),
                  pl.BlockSpec(memory_space=pl.ANY)],
        out_specs=pl.BlockSpec((tm, d), lambda i: (i, 0)),
        out_shape=jax.ShapeDtypeStruct((t, d), F32),
        scratch_shapes=[pltpu.VMEM((2, TOP_K, tm * PACKED_TILE_ROWS, LANES), U32),
                        pltpu.SemaphoreType.DMA((2,))],
        compiler_params=_cparams(1),
        name="moe_combine",
    )(dest, dest, x2d, gates_tok, g_final.reshape(1, d), y)


def _swa_q_order():
    heads = []
    for kv_pair in range(A_KV_HEADS // 2):
        for g in range(A_GROUP):
            heads += [(2 * kv_pair) * A_GROUP + g, (2 * kv_pair + 1) * A_GROUP + g]
    return jnp.asarray([h * HEAD_DIM + c for h in heads for c in range(HEAD_DIM)], I32)


def kernel(x, mem, rel_bias, g_mix, w_in, sinks_a, w_a_out, w_b_out, w_gate, b_gate, w_o,
           g_x, g_mem, w_xq, w_xk, w_xv, w_xo, g_moe, w_rg, w_re, w1, w3, w2, g_final):
    b, s, d = x.shape
    t = b * s
    depth = g_mix.shape[0]
    perm = _swa_q_order()
    n_assign = t * TOP_K
    n_rows = -(-(n_assign + N_EXPERTS * (MOE_ROWS - 1)) // MOE_ROWS) * MOE_ROWS
    n_blocks = n_rows // MOE_ROWS

    x2d = x.reshape(t, d)
    for l in range(depth):
        w_in_l = jnp.concatenate([w_in[l][:, :A_WIDTH][:, perm], w_in[l][:, A_WIDTH:]], axis=1).astype(BF16)
        qkv_a, qkv_b, q4, k4, v4, q16, k16, v16 = _in_proj(x2d, g_mix[l], w_in_l, b, s, tm=512)
        gates = _gate_proj(x2d, g_mix[l], w_gate[l].astype(BF16), b_gate[l], tm=512, tn=1024)
        oa = _swa_attention(qkv_a.reshape(b, s, 2 * B_WIDTH), rel_bias, sinks_a[l]).reshape(t, A_WIDTH)
        qkv_b4 = qkv_b.reshape(b, 1, s, 3 * B_WIDTH)
        windows = {r: w for w, r in B_PATTERNS}
        o1, l1 = _dilated_attention(qkv_b4, qkv_b4, qkv_b4, (0, 1, 2), rel_bias, windows[1])
        o4, l4 = _dilated_attention(q4, k4, v4, (0, 0, 0), rel_bias, windows[4])
        o16, l16 = _dilated_attention(q16, k16, v16, (0, 0, 0), rel_bias, windows[16])
        x2d = _mix_out(x2d, oa, o1.reshape(t, B_WIDTH), l1.reshape(t, LANES), o4, l4, o16, l16, gates,
                       w_a_out[l][perm].astype(BF16), w_b_out[l].astype(BF16), w_o[l].astype(BF16),
                       s, tm=256)
        k_mem, v_mem = _mem_kv(mem, g_mem[l], w_xk[l].astype(BF16), w_xv[l].astype(BF16))
        x2d = _cross_attention(x2d.reshape(b, s, d), g_x[l], w_xq[l].astype(BF16), k_mem, v_mem,
                               w_xo[l].astype(BF16), tm=512).reshape(t, d)
        w_router = jnp.concatenate([w_re[l], w_rg[l],
                                    jnp.zeros((d, LANES - N_EXPERTS - N_GROUPS), F32)], axis=1)
        hp, e_t, gate_t, cnt = _router(x2d, g_moe[l], w_router, tm=512)
        counts = cnt[:, 0].astype(I32)
        padded = (counts + MOE_ROWS - 1) // MOE_ROWS * MOE_ROWS
        seg_end = jnp.cumsum(padded)
        seg_start = seg_end - padded
        n_used = (seg_end[-1] // MOE_ROWS).astype(I32).reshape(1)
        block_row = jnp.arange(n_blocks, dtype=I32) * MOE_ROWS
        block_e = jnp.minimum(jnp.sum(seg_end[None, :] <= block_row[:, None], axis=1),
                              N_EXPERTS - 1).astype(I32)
        start = jnp.broadcast_to(seg_start.astype(F32)[:, None], (N_EXPERTS, 128))
        dest = _assignment_rows(e_t, start, tm=512)
        xg = _dispatch(seg_end.astype(I32), padded.astype(I32), n_used, dest, hp, n_rows, tm=128)
        experts = jnp.arange(N_EXPERTS, dtype=I32)
        owns = padded > 0
        later = (experts[None, :] > experts[:, None]) & owns[None, :]
        next_owner = jnp.min(jnp.where(later, experts[None, :], N_EXPERTS), axis=1)
        next_owner = jnp.where(next_owner == N_EXPERTS, experts, next_owner).astype(I32)
        run_parity = ((jnp.cumsum(owns.astype(I32)) - 1) % 2).astype(I32)
        y = _experts(block_e, n_used, next_owner[block_e], run_parity[block_e], xg, w1[l], w3[l], w2[l])
        x2d = _combine(dest, x2d, gate_t.T, g_final, y, final=(l + 1 == depth), tm=512)
    return x2d.reshape(b, s, d)
```

```python
import functools
import math

import jax
import jax.numpy as jnp
import numpy as np
from jax import lax
from jax.experimental import pallas as pl
from jax.experimental.pallas import tpu as pltpu

F32 = jnp.float32
BF16 = jnp.bfloat16
I32 = jnp.int32
U32 = jnp.uint32

HEAD_DIM = 64
PAIR = 2 * HEAD_DIM
A_Q_HEADS = 16
A_KV_HEADS = 4
A_GROUP = A_Q_HEADS // A_KV_HEADS
A_WIDTH = A_Q_HEADS * HEAD_DIM
A_KV_WIDTH = A_KV_HEADS * HEAD_DIM
A_WINDOW = 128
B_HEADS = 12
B_WIDTH = B_HEADS * HEAD_DIM
LANES = 128
LANE_CHUNKS = B_WIDTH // LANES
D_MODEL = 2048
PACKED_TILE_ROWS = D_MODEL // 2 // LANES
B_PATTERNS = ((128, 1), (512, 4), (2048, 16))
BLOCK = 128
Q_BLOCKS = 8
IN_COLS = A_WIDTH + 2 * A_KV_WIDTH + 3 * B_WIDTH
REL_BUCKETS = 32
REL_MAX_DIST = 2048
X_HEADS = 4
X_HEAD_DIM = 128
X_WIDTH = X_HEADS * X_HEAD_DIM
N_GROUPS = 4
EXP_PER_GROUP = 8
N_EXPERTS = N_GROUPS * EXP_PER_GROUP
TOP_K = 2
D_FF = 512
EPS = 1e-6
NEG = -1e30
LOG2E = math.log2(math.e)
LN2 = math.log(2.0)
MOE_ROWS = 256
VMEM_LIMIT = 56 * 1024 * 1024


def _cparams(n_axes):
    return pltpu.CompilerParams(dimension_semantics=("arbitrary",) * n_axes,
                                vmem_limit_bytes=VMEM_LIMIT)


def _rms(xf, g):
    return xf * lax.rsqrt(jnp.mean(xf * xf, axis=-1, keepdims=True) + EPS) * g


def _dot_nt(a, b):
    return lax.dot_general(a, b, (((1,), (1,)), ((), ())), preferred_element_type=F32)


def _load_resident(w_hbm_ref, w_ref, sem):
    @pl.when(pl.program_id(0) == 0)
    def _():
        copy = pltpu.make_async_copy(w_hbm_ref, w_ref, sem)
        copy.start()
        copy.wait()


def _in_proj_kernel(x_ref, g_ref, cs_ref, w_hbm_ref, oa_ref, ob_ref, q4_ref, k4_ref, v4_ref, q16_ref,
                    k16_ref, v16_ref, w_ref, acc_ref, mod4_ref, sem):
    _load_resident(w_hbm_ref, w_ref, sem)
    tm = x_ref.shape[0]
    tn = B_WIDTH
    n4 = tm // 4
    h = _rms(x_ref[...], g_ref[...]).astype(BF16)
    for j in range(2):
        cols = slice(j * tn, (j + 1) * tn)
        acc = jnp.dot(h, w_ref[:, cols], preferred_element_type=F32)
        oa_ref[:, cols] = (acc * cs_ref[:, cols]).astype(BF16)
    for part, (r4_ref, r16_ref) in enumerate(((q4_ref, q16_ref), (k4_ref, k16_ref), (v4_ref, v16_ref))):
        cols = slice((2 + part) * tn, (3 + part) * tn)
        acc = jnp.dot(h, w_ref[:, cols], preferred_element_type=F32)
        if part == 0:
            acc = acc * cs_ref[:, cols]
        ob_ref[:, part * tn:(part + 1) * tn] = acc.astype(BF16)
        for cc in range(LANE_CHUNKS):
            acc_ref[part, cc] = acc[:, cc * LANES:(cc + 1) * LANES]
        for c in range(4):
            rows = [acc_ref[part, cc, pl.ds(c, n4, stride=4), :] for cc in range(LANE_CHUNKS)]
            r4_ref[0, c] = jnp.concatenate(rows, axis=1).astype(BF16)
            for cc in range(LANE_CHUNKS):
                mod4_ref[part, cc, c * n4:(c + 1) * n4, :] = rows[cc]
        for c in range(16):
            rows = [mod4_ref[part, cc, pl.ds((c % 4) * n4 + c // 4, tm // 16, stride=4), :]
                    for cc in range(LANE_CHUNKS)]
            r16_ref[0, c] = jnp.concatenate(rows, axis=1).astype(BF16)


def _in_proj(x2d, g, w, b, s, *, tm):
    t, d = x2d.shape
    tn = B_WIDTH
    tiles = s // tm
    col = jnp.arange(IN_COLS, dtype=I32)[None, :]
    q_b = A_WIDTH + 2 * A_KV_WIDTH
    is_q = (col < A_WIDTH) | ((col >= q_b) & (col < q_b + B_WIDTH))
    col_scale = jnp.where(is_q, HEAD_DIM ** -0.5 * LOG2E, 1.0).astype(F32)
    res_shape = lambda r: jax.ShapeDtypeStruct((b, r, s // r, B_WIDTH), BF16)
    res_spec = lambda r: pl.BlockSpec((1, r, tm // r, B_WIDTH), lambda i: (i // tiles, 0, i % tiles, 0))
    return pl.pallas_call(
        _in_proj_kernel,
        grid=(t // tm,),
        in_specs=[
            pl.BlockSpec((tm, d), lambda i: (i, 0)),
            pl.BlockSpec((1, d), lambda i: (0, 0)),
            pl.BlockSpec((1, IN_COLS), lambda i: (0, 0)),
            pl.BlockSpec(memory_space=pl.ANY),
        ],
        out_specs=[pl.BlockSpec((tm, 2 * tn), lambda i: (i, 0)), pl.BlockSpec((tm, 3 * tn), lambda i: (i, 0))]
                  + [res_spec(4)] * 3 + [res_spec(16)] * 3,
        out_shape=[jax.ShapeDtypeStruct((t, 2 * tn), BF16), jax.ShapeDtypeStruct((t, 3 * tn), BF16)]
                  + [res_shape(4)] * 3 + [res_shape(16)] * 3,
        scratch_shapes=[pltpu.VMEM(w.shape, BF16), pltpu.VMEM((3, LANE_CHUNKS, tm, LANES), F32),
                        pltpu.VMEM((3, LANE_CHUNKS, tm, LANES), F32), pltpu.SemaphoreType.DMA(())],
        compiler_params=_cparams(1),
        name="in_proj",
    )(x2d, g.reshape(1, d), col_scale, w)


def _gate_proj_kernel(x_ref, g_ref, w_hbm_ref, b_ref, o_ref, w_ref, sem, *, tn):
    _load_resident(w_hbm_ref, w_ref, sem)
    h = _rms(x_ref[...], g_ref[...]).astype(BF16)
    for j in range(w_ref.shape[1] // tn):
        cols = slice(j * tn, (j + 1) * tn)
        acc = jnp.dot(h, w_ref[:, cols], preferred_element_type=F32)
        o_ref[:, cols] = jax.nn.sigmoid(acc + b_ref[:, cols]).astype(o_ref.dtype)


def _gate_proj(x2d, g, w, b, *, tm, tn):
    t, d = x2d.shape
    n = w.shape[1]
    return pl.pallas_call(
        functools.partial(_gate_proj_kernel, tn=tn),
        grid=(t // tm,),
        in_specs=[
            pl.BlockSpec((tm, d), lambda i: (i, 0)),
            pl.BlockSpec((1, d), lambda i: (0, 0)),
            pl.BlockSpec(memory_space=pl.ANY),
            pl.BlockSpec((1, n), lambda i: (0, 0)),
        ],
        out_specs=pl.BlockSpec((tm, n), lambda i: (i, 0)),
        out_shape=jax.ShapeDtypeStruct((t, n), BF16),
        scratch_shapes=[pltpu.VMEM(w.shape, BF16), pltpu.SemaphoreType.DMA(())],
        compiler_params=_cparams(1),
        name="gate_proj",
    )(x2d, g.reshape(1, d), w, b.reshape(1, n))


def _bucket_runs(step, max_off):
    max_exact = REL_BUCKETS // 2
    dist = np.arange(max_off + 1) * step
    buckets = []
    for ft in (np.float32, np.float64):
        df = np.maximum(dist, 1).astype(ft)
        large = max_exact + (np.log(df / ft(max_exact)) / ft(math.log(REL_MAX_DIST / max_exact))
                             * ft(REL_BUCKETS - max_exact)).astype(np.int32)
        buckets.append(np.where(dist < max_exact, dist, np.minimum(large, REL_BUCKETS - 1)))
    assert (buckets[0] == buckets[1]).all()
    runs = []
    for off, bucket in enumerate(buckets[0].tolist()):
        if not runs or runs[-1][1] != bucket:
            runs.append((off, bucket))
    return runs


def _fill_band_bias(bias_ref, table_ref, head0, n_heads, step, max_off):
    qi = lax.broadcasted_iota(I32, (BLOCK, 2 * BLOCK), 0)
    ki = lax.broadcasted_iota(I32, (BLOCK, 2 * BLOCK), 1)
    off = qi + BLOCK - ki
    runs = _bucket_runs(step, max_off)
    for h in range(n_heads):
        cur = jnp.full((BLOCK, 2 * BLOCK), NEG, F32)
        for first_off, bucket in runs:
            cur = jnp.where(off >= first_off, table_ref[bucket, head0 + h] * LOG2E, cur)
        rest = jnp.where(off > max_off, NEG, cur)
        bias_ref[1, h] = rest
        bias_ref[0, h] = jnp.where(ki >= BLOCK, rest, NEG)


def _softmax_pv(s, v_pair, sink):
    m = jnp.max(s, axis=-1, keepdims=True)
    if sink is not None:
        m = jnp.maximum(m, sink)
    p = jnp.exp2(s - m)
    l = jnp.sum(p, axis=-1, keepdims=True)
    if sink is not None:
        l = l + jnp.exp2(sink - m)
    o = jnp.dot(p.astype(BF16), v_pair, preferred_element_type=F32)
    return o, m, l


def _swa_kernel(sink_ref, table_ref, q_ref, kp_ref, kc_ref, vp_ref, vc_ref, o_ref, bias_ref):
    @pl.when((pl.program_id(0) == 0) & (pl.program_id(1) == 0))
    def _():
        _fill_band_bias(bias_ref, table_ref, 0, A_Q_HEADS, 1, A_WINDOW - 1)

    first = jnp.minimum(pl.program_id(1), 1)
    q_all = q_ref[0]
    k_all = jnp.concatenate([kp_ref[0], kc_ref[0]], axis=0)
    v_all = jnp.concatenate([vp_ref[0], vc_ref[0]], axis=0)
    lane = lax.broadcasted_iota(I32, (1, PAIR), 1)
    low = lane < HEAD_DIM
    for blk in range(q_all.shape[0] // BLOCK):
        rows = slice(blk * BLOCK, (blk + 1) * BLOCK)
        q = q_all[rows]
        k = k_all[blk * BLOCK:(blk + 2) * BLOCK]
        v = v_all[blk * BLOCK:(blk + 2) * BLOCK]
        variant = first if blk == 0 else 1
        for kv_pair in range(A_KV_HEADS // 2):
            k_pair = k[:, kv_pair * PAIR:(kv_pair + 1) * PAIR]
            v_pair = v[:, kv_pair * PAIR:(kv_pair + 1) * PAIR]
            k_half = (jnp.where(low, k_pair, jnp.zeros_like(k_pair)),
                      jnp.where(low, jnp.zeros_like(k_pair), k_pair))
            for g in range(A_GROUP):
                col = (kv_pair * A_GROUP + g) * PAIR
                q_pair = q[:, col:col + PAIR]
                outs = []
                for half in range(2):
                    head = (2 * kv_pair + half) * A_GROUP + g
                    s = _dot_nt(q_pair, k_half[half]) + bias_ref[variant, head]
                    o, _, l = _softmax_pv(s, v_pair, sink_ref[head] * LOG2E)
                    outs.append(o * (1.0 / l))
                o_ref[0, rows, col:col + PAIR] = jnp.where(low, outs[0], outs[1]).astype(o_ref.dtype)


def _swa_attention(qkv3, rel_bias, sinks):
    b, s, _ = qkv3.shape
    q_blocks = min(Q_BLOCKS, s // BLOCK)
    step = q_blocks * BLOCK
    assert s % step == 0, (s, step)
    kcol = A_WIDTH // A_KV_WIDTH
    prev = lambda i: jnp.maximum(i * q_blocks - 1, 0)
    return pl.pallas_call(
        _swa_kernel,
        grid=(b, s // step),
        in_specs=[
            pl.BlockSpec(memory_space=pltpu.SMEM),
            pl.BlockSpec(memory_space=pltpu.SMEM),
            pl.BlockSpec((1, step, A_WIDTH), lambda bi, i: (bi, i, 0)),
            pl.BlockSpec((1, BLOCK, A_KV_WIDTH), lambda bi, i: (bi, prev(i), kcol)),
            pl.BlockSpec((1, step, A_KV_WIDTH), lambda bi, i: (bi, i, kcol)),
            pl.BlockSpec((1, BLOCK, A_KV_WIDTH), lambda bi, i: (bi, prev(i), kcol + 1)),
            pl.BlockSpec((1, step, A_KV_WIDTH), lambda bi, i: (bi, i, kcol + 1)),
        ],
        out_specs=pl.BlockSpec((1, step, A_WIDTH), lambda bi, i: (bi, i, 0)),
        out_shape=jax.ShapeDtypeStruct((b, s, A_WIDTH), BF16),
        scratch_shapes=[pltpu.VMEM((2, A_Q_HEADS, BLOCK, 2 * BLOCK), F32)],
        compiler_params=_cparams(2),
        name="swa_attention",
    )(sinks, rel_bias, qkv3, qkv3, qkv3, qkv3, qkv3)


def _dilated_kernel(table_ref, q_ref, kp_ref, kc_ref, vp_ref, vc_ref, o_ref, lse_ref, bias_ref, *,
                    dilation, max_off):
    @pl.when((pl.program_id(0) == 0) & (pl.program_id(1) == 0) & (pl.program_id(2) == 0))
    def _():
        _fill_band_bias(bias_ref, table_ref, A_Q_HEADS, B_HEADS, dilation, max_off)

    first = jnp.minimum(pl.program_id(2), 1)
    q_all = q_ref[...]
    k_all = jnp.concatenate([kp_ref[...], kc_ref[...]], axis=0)
    v_all = jnp.concatenate([vp_ref[...], vc_ref[...]], axis=0)
    lane = lax.broadcasted_iota(I32, (1, PAIR), 1)
    low = lane < HEAD_DIM
    for blk in range(q_all.shape[0] // BLOCK):
        rows = slice(blk * BLOCK, (blk + 1) * BLOCK)
        variant = first if blk == 0 else 1
        lse_tile = jnp.zeros((BLOCK, LANES), F32)
        for pair in range(B_HEADS // 2):
            col = pair * PAIR
            q_pair = q_all[rows, col:col + PAIR]
            k_pair = k_all[blk * BLOCK:(blk + 2) * BLOCK, col:col + PAIR]
            v_pair = v_all[blk * BLOCK:(blk + 2) * BLOCK, col:col + PAIR]
            k_half = (jnp.where(low, k_pair, jnp.zeros_like(k_pair)),
                      jnp.where(low, jnp.zeros_like(k_pair), k_pair))
            outs = []
            for half in range(2):
                head = 2 * pair + half
                s = _dot_nt(q_pair, k_half[half]) + bias_ref[variant, head]
                o, m, l = _softmax_pv(s, v_pair, None)
                outs.append(o * (1.0 / l))
                lse_tile = jnp.where(lane == head, m * LN2 + jnp.log(l), lse_tile)
            o_ref[rows, col:col + PAIR] = jnp.where(low, outs[0], outs[1]).astype(o_ref.dtype)
        lse_ref[rows, :] = lse_tile


def _dilated_attention(q, k, v, cols, rel_bias, window):
    b, r, n, _ = q.shape
    q_blocks = min(Q_BLOCKS, n // BLOCK)
    step = q_blocks * BLOCK
    assert n % step == 0, (n, step)
    qc, kc, vc = cols
    prev = lambda j: jnp.maximum(j * q_blocks - 1, 0)
    cur = lambda col, w=B_WIDTH: pl.BlockSpec((None, None, step, w), lambda bi, c, j: (bi, c, j, col))
    old = lambda col: pl.BlockSpec((None, None, BLOCK, B_WIDTH), lambda bi, c, j: (bi, c, prev(j), col))
    return pl.pallas_call(
        functools.partial(_dilated_kernel, dilation=r, max_off=window // r),
        grid=(b, r, n // step),
        in_specs=[pl.BlockSpec(memory_space=pltpu.SMEM), cur(qc), old(kc), cur(kc), old(vc), cur(vc)],
        out_specs=[cur(0), cur(0, LANES)],
        out_shape=[jax.ShapeDtypeStruct((b, r, n, B_WIDTH), BF16),
                   jax.ShapeDtypeStruct((b, r, n, LANES), F32)],
        scratch_shapes=[pltpu.VMEM((2, B_HEADS, BLOCK, 2 * BLOCK), F32)],
        compiler_params=_cparams(3),
        name=f"dilated_attention_r{r}",
    )(rel_bias, q, k, k, v, v)


def _mix_out_kernel(x_ref, oa_ref, o1_ref, l1_ref, o4_ref, l4_ref, o16_ref, l16_ref, ga_ref, gb_ref,
                    wa_ref, wb_ref, wo_ref, ex_ref, out_ref, seq_ref):
    tm = x_ref.shape[0]
    seq = []
    base = 0
    for r, ref in ((4, o4_ref), (4, l4_ref), (16, o16_ref), (16, l16_ref)):
        chunks = ref.shape[-1] // LANES
        for c in range(r):
            rows = ref[0, c].astype(F32)
            for cc in range(chunks):
                seq_ref[base + cc, pl.ds(c, tm // r, stride=r), :] = rows[:, cc * LANES:(cc + 1) * LANES]
        seq.append(jnp.concatenate([seq_ref[base + cc] for cc in range(chunks)], axis=1))
        base += chunks
    o2, l2, o3, l3 = seq
    l1 = l1_ref[...]
    m = jnp.maximum(jnp.maximum(l1, l2), l3)
    e1, e2, e3 = jnp.exp(l1 - m), jnp.exp(l2 - m), jnp.exp(l3 - m)
    inv = 1.0 / (e1 + e2 + e3)

    def widen(w):
        hi = w.astype(BF16)
        lo = (w - hi.astype(F32)).astype(BF16)
        return (jnp.dot(hi, ex_ref[...], preferred_element_type=F32)
                + jnp.dot(lo, ex_ref[...], preferred_element_type=F32))

    ob = (widen(e1 * inv) * o1_ref[...].astype(F32) + widen(e2 * inv) * o2 + widen(e3 * inv) * o3)
    ya = jnp.dot(oa_ref[...], wa_ref[...], preferred_element_type=F32)
    yb = jnp.dot(ob.astype(BF16), wb_ref[...], preferred_element_type=F32)
    mixed = ga_ref[...].astype(F32) * ya + gb_ref[...].astype(F32) * yb
    out_ref[...] = x_ref[...] + jnp.dot(mixed.astype(BF16), wo_ref[...], preferred_element_type=F32)


def _mix_out(x2d, oa, o1, l1, o4, l4, o16, l16, gates, wa, wb, wo, s, *, tm):
    t, d = x2d.shape
    tiles = s // tm
    row = lambda w: pl.BlockSpec((tm, w), lambda i: (i, 0))
    res = lambda r, w: pl.BlockSpec((1, r, tm // r, w), lambda i: (i // tiles, 0, i % tiles, 0))
    const = lambda shape: pl.BlockSpec(shape, lambda i: (0, 0))
    head_of_col = jnp.arange(B_WIDTH, dtype=I32)[None, :] // HEAD_DIM
    expand = (jnp.arange(LANES, dtype=I32)[:, None] == head_of_col).astype(BF16)
    return pl.pallas_call(
        _mix_out_kernel,
        grid=(t // tm,),
        in_specs=[row(d), row(A_WIDTH), row(B_WIDTH), row(LANES), res(4, B_WIDTH), res(4, LANES),
                  res(16, B_WIDTH), res(16, LANES),
                  pl.BlockSpec((tm, d), lambda i: (i, 0)), pl.BlockSpec((tm, d), lambda i: (i, 1)),
                  const(wa.shape), const(wb.shape), const(wo.shape), const(expand.shape)],
        out_specs=row(d),
        out_shape=jax.ShapeDtypeStruct((t, d), F32),
        scratch_shapes=[pltpu.VMEM((2 * (LANE_CHUNKS + 1), tm, LANES), F32)],
        compiler_params=_cparams(1),
        name="mix_out",
    )(x2d, oa, o1, l1, o4, l4, o16, l16, gates, gates, wa, wb, wo, expand)


def _mem_kv_kernel(mem_ref, g_ref, wk_ref, wv_ref, k_ref, v_ref):
    hn = _rms(mem_ref[0], g_ref[...]).astype(BF16)
    k_ref[0] = jnp.dot(hn, wk_ref[...], preferred_element_type=F32).astype(BF16)
    v_ref[0] = jnp.dot(hn, wv_ref[...], preferred_element_type=F32).astype(BF16)


def _mem_kv(mem, g, wk, wv):
    b, ml, d = mem.shape
    const = lambda shape: pl.BlockSpec(shape, lambda bi: (0,) * len(shape))
    out = jax.ShapeDtypeStruct((b, ml, X_WIDTH), BF16)
    blk = pl.BlockSpec((1, ml, X_WIDTH), lambda bi: (bi, 0, 0))
    return pl.pallas_call(
        _mem_kv_kernel,
        grid=(b,),
        in_specs=[pl.BlockSpec((1, ml, d), lambda bi: (bi, 0, 0)), const((1, d)),
                  const(wk.shape), const(wv.shape)],
        out_specs=[blk, blk],
        out_shape=[out, out],
        compiler_params=_cparams(1),
        name="mem_kv",
    )(mem, g.reshape(1, d), wk, wv)


def _cross_kernel(x_ref, g_ref, wq_ref, k_ref, v_ref, wo_ref, out_ref):
    x = x_ref[0]
    h = _rms(x, g_ref[...]).astype(BF16)
    q = (jnp.dot(h, wq_ref[...], preferred_element_type=F32) * (X_HEAD_DIM ** -0.5)).astype(BF16)
    k, v = k_ref[0], v_ref[0]
    outs = []
    for hd in range(X_HEADS):
        sl = slice(hd * X_HEAD_DIM, (hd + 1) * X_HEAD_DIM)
        s = _dot_nt(q[:, sl], k[:, sl])
        m = jnp.max(s, axis=-1, keepdims=True)
        p = jnp.exp(s - m)
        l = jnp.sum(p, axis=-1, keepdims=True)
        o = jnp.dot(p.astype(BF16), v[:, sl], preferred_element_type=F32)
        outs.append((o * (1.0 / l)).astype(BF16))
    o = jnp.concatenate(outs, axis=1)
    out_ref[0] = x + jnp.dot(o, wo_ref[...], preferred_element_type=F32)


def _cross_attention(x3, g, wq, k, v, wo, *, tm):
    b, s, d = x3.shape
    ml = k.shape[1]
    const = lambda shape: pl.BlockSpec(shape, lambda bi, i: (0,) * len(shape))
    return pl.pallas_call(
        _cross_kernel,
        grid=(b, s // tm),
        in_specs=[pl.BlockSpec((1, tm, d), lambda bi, i: (bi, i, 0)), const((1, d)), const(wq.shape),
                  pl.BlockSpec((1, ml, X_WIDTH), lambda bi, i: (bi, 0, 0)),
                  pl.BlockSpec((1, ml, X_WIDTH), lambda bi, i: (bi, 0, 0)),
                  const(wo.shape)],
        out_specs=pl.BlockSpec((1, tm, d), lambda bi, i: (bi, i, 0)),
        out_shape=jax.ShapeDtypeStruct((b, s, d), F32),
        compiler_params=_cparams(2),
        name="cross_attention",
    )(x3, g.reshape(1, d), wq, k, v, wo)


def _pack_bf16_pairs(x):
    n = x.shape[1] // 2
    bits = lax.bitcast_convert_type(x.astype(BF16).astype(F32), U32)
    return (bits[:, :n] >> 16) | (bits[:, n:] & jnp.uint32(0xFFFF0000))


def _unpack_bf16_pairs(p):
    return (lax.bitcast_convert_type(p << 16, F32),
            lax.bitcast_convert_type(p & jnp.uint32(0xFFFF0000), F32))


def _first_argmax(vals, rows, n):
    m = jnp.max(vals, axis=0, keepdims=True)
    idx = jnp.min(jnp.where(vals == m, rows, n), axis=0, keepdims=True)
    return m, idx


def _router_kernel(x_ref, g_ref, whi_ref, wlo_ref, hp_ref, e_ref, gate_ref, cnt_ref):
    h = _rms(x_ref[...], g_ref[...])
    tm, d = h.shape
    h_hi = h.astype(BF16)
    h_hi32 = h_hi.astype(F32)
    h_lo = (h - h_hi32).astype(BF16)
    logits = (jnp.dot(h_hi, whi_ref[...], preferred_element_type=F32)
              + jnp.dot(h_lo, whi_ref[...], preferred_element_type=F32)
              + jnp.dot(h_hi, wlo_ref[...], preferred_element_type=F32)).T
    rows8 = lax.broadcasted_iota(I32, (EXP_PER_GROUP, tm), 0)
    gl = jnp.where(rows8 < N_GROUPS, logits[N_EXPERTS:N_EXPERTS + 8], -jnp.inf)
    gmax, gidx = _first_argmax(gl, rows8, 8)
    g_gate = 1.0 / jnp.sum(jnp.exp(gl - gmax), axis=0, keepdims=True)
    sel = jnp.zeros((EXP_PER_GROUP, tm), F32)
    for grp in range(N_GROUPS):
        sel = jnp.where(gidx == grp, logits[grp * EXP_PER_GROUP:(grp + 1) * EXP_PER_GROUP], sel)
    v1, i1 = _first_argmax(sel, rows8, 8)
    sel2 = jnp.where(rows8 == i1, -jnp.inf, sel)
    v2, i2 = _first_argmax(sel2, rows8, 8)
    e2x = jnp.exp(v2 - v1)
    den = 1.0 + e2x
    e_ref[...] = jnp.concatenate([gidx * EXP_PER_GROUP + i1, gidx * EXP_PER_GROUP + i2], axis=0)
    gate_ref[...] = jnp.concatenate([(1.0 / den) * g_gate, (e2x / den) * g_gate], axis=0)

    bits = lax.bitcast_convert_type(h_hi32, U32)
    packed = (bits[:, :d // 2] >> 16) | (bits[:, d // 2:] & jnp.uint32(0xFFFF0000))
    for c in range(PACKED_TILE_ROWS):
        hp_ref[pl.ds(c, tm, stride=PACKED_TILE_ROWS), :] = packed[:, c * LANES:(c + 1) * LANES]

    @pl.when(pl.program_id(0) == 0)
    def _():
        cnt_ref[...] = jnp.zeros_like(cnt_ref)

    rows32 = lax.broadcasted_iota(I32, (N_EXPERTS, tm), 0)
    e = e_ref[...]
    hits = (rows32 == e[0:1]).astype(F32) + (rows32 == e[1:2]).astype(F32)
    cnt_ref[...] += jnp.sum(hits, axis=1, keepdims=True)


def _router(x2d, g, w_router, *, tm):
    t, d = x2d.shape
    w_hi = w_router.astype(BF16)
    w_lo = (w_router - w_hi.astype(F32)).astype(BF16)
    return pl.pallas_call(
        _router_kernel,
        grid=(t // tm,),
        in_specs=[pl.BlockSpec((tm, d), lambda i: (i, 0)),
                  pl.BlockSpec((1, d), lambda i: (0, 0)),
                  pl.BlockSpec((d, LANES), lambda i: (0, 0)),
                  pl.BlockSpec((d, LANES), lambda i: (0, 0))],
        out_specs=[pl.BlockSpec((tm * PACKED_TILE_ROWS, LANES), lambda i: (i, 0)),
                   pl.BlockSpec((TOP_K, tm), lambda i: (0, i)),
                   pl.BlockSpec((TOP_K, tm), lambda i: (0, i)),
                   pl.BlockSpec((N_EXPERTS, 128), lambda i: (0, 0))],
        out_shape=[jax.ShapeDtypeStruct((t * PACKED_TILE_ROWS, LANES), U32),
                   jax.ShapeDtypeStruct((TOP_K, t), I32),
                   jax.ShapeDtypeStruct((TOP_K, t), F32),
                   jax.ShapeDtypeStruct((N_EXPERTS, 128), F32)],
        compiler_params=_cparams(1),
        name="moe_router",
    )(x2d, g.reshape(1, d), w_hi, w_lo)


def _dest_kernel(e_ref, start_ref, d_ref, carry_ref):
    @pl.when(pl.program_id(0) == 0)
    def _():
        carry_ref[...] = jnp.zeros_like(carry_ref)

    e = e_ref[...]
    tm = e.shape[1]
    rows32 = lax.broadcasted_iota(I32, (N_EXPERTS, tm), 0)
    oh0 = (rows32 == e[0:1]).astype(F32)
    oh1 = (rows32 == e[1:2]).astype(F32)
    hits = oh0 + oh1
    earlier = (lax.broadcasted_iota(I32, (tm, tm), 0) < lax.broadcasted_iota(I32, (tm, tm), 1))
    prefix = jnp.dot(hits.astype(BF16), earlier.astype(BF16), preferred_element_type=F32)
    base = prefix + carry_ref[:, 0:1] + start_ref[:, 0:1]
    d0 = jnp.sum(oh0 * base, axis=0, keepdims=True)
    d1 = jnp.sum(oh1 * base, axis=0, keepdims=True)
    d_ref[...] = jnp.concatenate([d0, d1], axis=0).astype(I32)
    carry_ref[...] += jnp.sum(hits, axis=1, keepdims=True)


def _assignment_rows(e_t, start, *, tm):
    t = e_t.shape[1]
    return pl.pallas_call(
        _dest_kernel,
        grid=(t // tm,),
        in_specs=[pl.BlockSpec((TOP_K, tm), lambda i: (0, i)),
                  pl.BlockSpec((N_EXPERTS, 128), lambda i: (0, 0))],
        out_specs=pl.BlockSpec((TOP_K, tm), lambda i: (0, i)),
        out_shape=jax.ShapeDtypeStruct((TOP_K, t), I32),
        scratch_shapes=[pltpu.VMEM((N_EXPERTS, 128), F32)],
        compiler_params=_cparams(1),
        name="moe_assignment_rows",
    )(e_t, start)


def _token_rows(ref, token, rows):
    return ref.at[pl.ds(pl.multiple_of(token * rows, rows), rows)]


def _dispatch_kernel(end_ref, pad_ref, nu_ref, d_ref, hp_ref, xg_ref, zero_ref, sem, zero_sem):
    rows = PACKED_TILE_ROWS
    tm = hp_ref.shape[0] // rows
    block_rows = MOE_ROWS * rows

    @pl.when(pl.program_id(0) == 0)
    def _():
        zero_ref[...] = jnp.zeros_like(zero_ref)

        def zero_block(block):
            start = pl.multiple_of(block * block_rows, block_rows)
            return pltpu.make_async_copy(zero_ref, xg_ref.at[pl.ds(start, block_rows)], zero_sem)

        for phase in ("start", "wait"):
            for e in range(N_EXPERTS):
                @pl.when(pad_ref[e] > 0)
                def _():
                    copy = zero_block(end_ref[e] // MOE_ROWS - 1)
                    copy.start() if phase == "start" else copy.wait()

        def start_tail(block, carry):
            zero_block(block).start()
            return carry

        def wait_tail(block, carry):
            zero_block(block).wait()
            return carry

        n_blocks = xg_ref.shape[0] // block_rows
        lax.fori_loop(nu_ref[0], n_blocks, start_tail, 0)
        lax.fori_loop(nu_ref[0], n_blocks, wait_tail, 0)

    for t in range(tm):
        for k in range(TOP_K):
            pltpu.make_async_copy(_token_rows(hp_ref, t, rows), _token_rows(xg_ref, d_ref[k, t], rows),
                                  sem).start(priority=k)
    for k in range(TOP_K):
        pltpu.make_async_copy(hp_ref, xg_ref.at[pl.ds(0, tm * rows)], sem).wait()


def _dispatch(seg_end, padded, n_used, dest, hp, n_rows, *, tm):
    rows = PACKED_TILE_ROWS
    t = hp.shape[0] // rows
    return pl.pallas_call(
        _dispatch_kernel,
        grid_spec=pltpu.PrefetchScalarGridSpec(
            num_scalar_prefetch=3,
            grid=(t // tm,),
            in_specs=[pl.BlockSpec((TOP_K, tm), lambda i, *_: (0, i), memory_space=pltpu.SMEM),
                      pl.BlockSpec((tm * rows, LANES), lambda i, *_: (i, 0))],
            out_specs=pl.BlockSpec(memory_space=pl.ANY),
            scratch_shapes=[pltpu.VMEM((MOE_ROWS * rows, LANES), U32), pltpu.SemaphoreType.DMA(()),
                            pltpu.SemaphoreType.DMA(())],
        ),
        out_shape=jax.ShapeDtypeStruct((n_rows * rows, LANES), U32),
        compiler_params=_cparams(1),
        name="moe_dispatch",
    )(seg_end, padded, n_used, dest, hp)


def _expert_kernel(be_ref, nu_ref, nxt_ref, par_ref, xg_ref, w1_ref, w3_ref, w2_ref, y_ref,
                   f1_ref, f3_ref, f2_ref, w1b_ref, w3b_ref, w2b_ref, sem):
    i = pl.program_id(0)

    def weight_copies(e, slot):
        return [pltpu.make_async_copy(w_ref.at[e], f_ref.at[slot], sem.at[slot, n])
                for n, (w_ref, f_ref) in enumerate(((w1_ref, f1_ref), (w3_ref, f3_ref), (w2_ref, f2_ref)))]

    @pl.when(i < nu_ref[0])
    def _():
        e = be_ref[i]
        slot = par_ref[i]

        @pl.when((i == 0) | (e != be_ref[jnp.maximum(i - 1, 0)]))
        def _():
            @pl.when(i == 0)
            def _():
                for copy in weight_copies(e, slot):
                    copy.start()

            for copy in weight_copies(e, slot):
                copy.wait()
            w1b_ref[...] = f1_ref[slot].astype(BF16)
            w3b_ref[...] = f3_ref[slot].astype(BF16)
            w2b_ref[...] = f2_ref[slot].astype(BF16)

            @pl.when(nxt_ref[i] != e)
            def _():
                for copy in weight_copies(nxt_ref[i], 1 - slot):
                    copy.start()

        halves = [_unpack_bf16_pairs(xg_ref[pl.ds(c, MOE_ROWS, stride=PACKED_TILE_ROWS), :])
                  for c in range(PACKED_TILE_ROWS)]
        x = jnp.concatenate([lo for lo, _ in halves] + [hi for _, hi in halves], axis=1).astype(BF16)
        h1 = jnp.dot(x, w1b_ref[...], preferred_element_type=F32)
        h3 = jnp.dot(x, w3b_ref[...], preferred_element_type=F32)
        a = (jax.nn.silu(h1) * h3).astype(BF16)
        y = _pack_bf16_pairs(jnp.dot(a, w2b_ref[...], preferred_element_type=F32))
        for c in range(PACKED_TILE_ROWS):
            y_ref[pl.ds(c, MOE_ROWS, stride=PACKED_TILE_ROWS), :] = y[:, c * LANES:(c + 1) * LANES]

    @pl.when(i >= nu_ref[0])
    def _():
        y_ref[...] = jnp.zeros_like(y_ref)


def _experts(block_e, n_used, next_e, parity, xg, w1, w3, w2):
    rows = PACKED_TILE_ROWS
    n_rows = xg.shape[0] // rows
    _, d, dff = w1.shape
    nblk = n_rows // MOE_ROWS
    live = lambda i, nu: jnp.maximum(jnp.minimum(i, nu[0] - 1), 0)
    any_space = pl.BlockSpec(memory_space=pl.ANY)
    return pl.pallas_call(
        _expert_kernel,
        grid_spec=pltpu.PrefetchScalarGridSpec(
            num_scalar_prefetch=4,
            grid=(nblk,),
            in_specs=[pl.BlockSpec((MOE_ROWS * rows, LANES), lambda i, be, nu, nx, par: (live(i, nu), 0)),
                      any_space, any_space, any_space],
            out_specs=pl.BlockSpec((MOE_ROWS * rows, LANES), lambda i, be, nu, nx, par: (i, 0)),
            scratch_shapes=[pltpu.VMEM((2, d, dff), F32), pltpu.VMEM((2, d, dff), F32),
                            pltpu.VMEM((2, dff, d), F32), pltpu.VMEM((d, dff), BF16),
                            pltpu.VMEM((d, dff), BF16), pltpu.VMEM((dff, d), BF16),
                            pltpu.SemaphoreType.DMA((2, 3))],
        ),
        out_shape=jax.ShapeDtypeStruct((n_rows * rows, LANES), U32),
        compiler_params=_cparams(1),
        name="moe_experts",
    )(block_e, n_used, next_e, parity, xg, w1, w3, w2)


def _combine_kernel(dc_ref, dn_ref, x_ref, gate_ref, gf_ref, y_ref, out_ref, buf, sem, *, final):
    i = pl.program_id(0)
    n = pl.num_programs(0)
    tm = x_ref.shape[0]
    rows = PACKED_TILE_ROWS

    def issue(d_ref, slot):
        for t in range(tm):
            for k in range(TOP_K):
                pltpu.make_async_copy(_token_rows(y_ref, d_ref[k, t], rows),
                                      _token_rows(buf.at[slot, k], t, rows),
                                      sem.at[slot]).start(priority=k)

    @pl.when(i == 0)
    def _():
        issue(dc_ref, 0)

    @pl.when(i + 1 < n)
    def _():
        issue(dn_ref, (i + 1) % 2)

    slot = i % 2
    for k in range(TOP_K):
        pltpu.make_async_copy(y_ref.at[pl.ds(0, tm * rows)], buf.at[slot, k], sem.at[slot]).wait()
    g = gate_ref[...]
    g0, g1 = g[:, 0:1], g[:, 1:2]
    low, high = [], []
    for c in range(rows):
        lo0, hi0 = _unpack_bf16_pairs(buf[slot, 0, pl.ds(c, tm, stride=rows), :])
        lo1, hi1 = _unpack_bf16_pairs(buf[slot, 1, pl.ds(c, tm, stride=rows), :])
        low.append(x_ref[:, c * LANES:(c + 1) * LANES] + (g0 * lo0 + g1 * lo1))
        high.append(x_ref[:, (rows + c) * LANES:(rows + c + 1) * LANES] + (g0 * hi0 + g1 * hi1))
    y = jnp.concatenate(low + high, axis=1)
    out_ref[...] = _rms(y, gf_ref[...]) if final else y


def _combine(dest, x2d, gates_tok, g_final, y, *, final, tm):
    t, d = x2d.shape
    nt = t // tm
    return pl.pallas_call(
        functools.partial(_combine_kernel, final=final),
        grid=(nt,),
        in_specs=[pl.BlockSpec((TOP_K, tm), lambda i: (0, i), memory_space=pltpu.SMEM),
                  pl.BlockSpec((TOP_K, tm), lambda i: (0, jnp.minimum(i + 1, nt - 1)),
                               memory_space=pltpu.SMEM),
                  pl.BlockSpec((tm, d), lambda i: (i, 0)),
                  pl.BlockSpec((tm, TOP_K), lambda i: (i, 0)),
                  pl.BlockSpec((1, d), lambda i: (0, 0)),
                  pl.BlockSpec(memory_space=pl.ANY)],
        out_specs=pl.BlockSpec((tm, d), lambda i: (i, 0)),
        out_shape=jax.ShapeDtypeStruct((t, d), F32),
        scratch_shapes=[pltpu.VMEM((2, TOP_K, tm * PACKED_TILE_ROWS, LANES), U32),
                        pltpu.SemaphoreType.DMA((2,))],
        compiler_params=_cparams(1),
        name="moe_combine",
    )(dest, dest, x2d, gates_tok, g_final.reshape(1, d), y)


def _swa_q_order():
    heads = []
    for kv_pair in range(A_KV_HEADS // 2):
        for g in range(A_GROUP):
            heads += [(2 * kv_pair) * A_GROUP + g, (2 * kv_pair + 1) * A_GROUP + g]
    return jnp.asarray([h * HEAD_DIM + c for h in heads for c in range(HEAD_DIM)], I32)


def kernel(x, mem, rel_bias, g_mix, w_in, sinks_a, w_a_out, w_b_out, w_gate, b_gate, w_o,
           g_x, g_mem, w_xq, w_xk, w_xv, w_xo, g_moe, w_rg, w_re, w1, w3, w2, g_final):
    b, s, d = x.shape
    t = b * s
    depth = g_mix.shape[0]
    perm = _swa_q_order()
    n_assign = t * TOP_K
    n_rows = -(-(n_assign + N_EXPERTS * (MOE_ROWS - 1)) // MOE_ROWS) * MOE_ROWS
    n_blocks = n_rows // MOE_ROWS

    x2d = x.reshape(t, d)
    for l in range(depth):
        w_in_l = jnp.concatenate([w_in[l][:, :A_WIDTH][:, perm], w_in[l][:, A_WIDTH:]], axis=1).astype(BF16)
        qkv_a, qkv_b, q4, k4, v4, q16, k16, v16 = _in_proj(x2d, g_mix[l], w_in_l, b, s, tm=512)
        gates = _gate_proj(x2d, g_mix[l], w_gate[l].astype(BF16), b_gate[l], tm=512, tn=1024)
        oa = _swa_attention(qkv_a.reshape(b, s, 2 * B_WIDTH), rel_bias, sinks_a[l]).reshape(t, A_WIDTH)
        qkv_b4 = qkv_b.reshape(b, 1, s, 3 * B_WIDTH)
        windows = {r: w for w, r in B_PATTERNS}
        o1, l1 = _dilated_attention(qkv_b4, qkv_b4, qkv_b4, (0, 1, 2), rel_bias, windows[1])
        o4, l4 = _dilated_attention(q4, k4, v4, (0, 0, 0), rel_bias, windows[4])
        o16, l16 = _dilated_attention(q16, k16, v16, (0, 0, 0), rel_bias, windows[16])
        x2d = _mix_out(x2d, oa, o1.reshape(t, B_WIDTH), l1.reshape(t, LANES), o4, l4, o16, l16, gates,
                       w_a_out[l][perm].astype(BF16), w_b_out[l].astype(BF16), w_o[l].astype(BF16),
                       s, tm=256)
        k_mem, v_mem = _mem_kv(mem, g_mem[l], w_xk[l].astype(BF16), w_xv[l].astype(BF16))
        x2d = _cross_attention(x2d.reshape(b, s, d), g_x[l], w_xq[l].astype(BF16), k_mem, v_mem,
                               w_xo[l].astype(BF16), tm=512).reshape(t, d)
        w_router = jnp.concatenate([w_re[l], w_rg[l],
                                    jnp.zeros((d, LANES - N_EXPERTS - N_GROUPS), F32)], axis=1)
        hp, e_t, gate_t, cnt = _router(x2d, g_moe[l], w_router, tm=512)
        counts = cnt[:, 0].astype(I32)
        padded = (counts + MOE_ROWS - 1) // MOE_ROWS * MOE_ROWS
        seg_end = jnp.cumsum(padded)
        seg_start = seg_end - padded
        n_used = (seg_end[-1] // MOE_ROWS).astype(I32).reshape(1)
        block_row = jnp.arange(n_blocks, dtype=I32) * MOE_ROWS
        block_e = jnp.minimum(jnp.sum(seg_end[None, :] <= block_row[:, None], axis=1),
                              N_EXPERTS - 1).astype(I32)
        start = jnp.broadcast_to(seg_start.astype(F32)[:, None], (N_EXPERTS, 128))
        dest = _assignment_rows(e_t, start, tm=512)
        xg = _dispatch(seg_end.astype(I32), padded.astype(I32), n_used, dest, hp, n_rows, tm=128)
        experts = jnp.arange(N_EXPERTS, dtype=I32)
        owns = padded > 0
        later = (experts[None, :] > experts[:, None]) & owns[None, :]
        next_owner = jnp.min(jnp.where(later, experts[None, :], N_EXPERTS), axis=1)
        next_owner = jnp.where(next_owner == N_EXPERTS, experts, next_owner).astype(I32)
        run_parity = ((jnp.cumsum(owns.astype(I32)) - 1) % 2).astype(I32)
        y = _experts(block_e, n_used, next_owner[block_e], run_parity[block_e], xg, w1[l], w3[l], w2[l])
        x2d = _combine(dest, x2d, gate_t.T, g_final, y, final=(l + 1 == depth), tm=256)
    return x2d.reshape(b, s, d)
```
